```python
import jax
import jax.numpy as jnp
from jax import lax
import numpy as np


D_MODEL = 2048
BATCH = 4
SEQ = 4096
DEPTH = 4

A_WIDTH = D_MODEL // 2
B_WIDTH = D_MODEL - A_WIDTH
A_GROUPS = 8
A_GROUP_DIM = A_WIDTH // A_GROUPS
CHUNK = 128
CONV_WIDTH = 3
ATTN_HEAD_DIM = 128
ATTN_HEADS = D_MODEL // ATTN_HEAD_DIM
DILATED_BRANCHES = ((128, 1), (512, 4), (2048, 16))
ATTN_BLOCK = 128
ROPE_THETA = 10000.0
FFN_DIM = 4 * D_MODEL
N_EVEN = (DEPTH + 1) // 2
N_ODD = DEPTH // 2
RMS_EPS = 1e-6
LN_EPS = 1e-5

kernel_name = 'hybrid_gmlp_shortconv_dilated_attn_trunk'


def rmsnorm(x, g):
    xf = x.astype(jnp.float32)
    y = xf * lax.rsqrt(jnp.mean(xf * xf, axis=-1, keepdims=True) + RMS_EPS)
    return (y * g.astype(jnp.float32)).astype(x.dtype)


def rope(t, positions):
    dh = t.shape[-1]
    half = dh // 2
    inv_freq = ROPE_THETA ** (-jnp.arange(half, dtype=jnp.float32) * 2.0 / dh)
    ang = positions.astype(jnp.float32)[:, None] * inv_freq[None, :]
    cos = jnp.cos(ang)[None, :, None, :]
    sin = jnp.sin(ang)[None, :, None, :]
    t1 = t[..., :half].astype(jnp.float32)
    t2 = t[..., half:].astype(jnp.float32)
    out = jnp.concatenate([t1 * cos - t2 * sin, t2 * cos + t1 * sin], axis=-1)
    return out.astype(t.dtype)


def gmlp_shortconv_mixer(h, w_in, w_s, b_s, conv_w, w_out):
    bsz, s, _ = h.shape
    proj = h @ w_in
    a_u, a_v, g_b, g_c, b_x = jnp.split(
        proj, [A_WIDTH, 2 * A_WIDTH, 2 * A_WIDTH + B_WIDTH, 2 * A_WIDTH + 2 * B_WIDTH], axis=-1)

    a_u = jax.nn.gelu(a_u)
    a_v = jax.nn.gelu(a_v)
    vf = a_v.astype(jnp.float32)
    mu = jnp.mean(vf, axis=-1, keepdims=True)
    var = jnp.mean(jnp.square(vf - mu), axis=-1, keepdims=True)
    vn = ((vf - mu) * lax.rsqrt(var + LN_EPS)).astype(h.dtype)
    n_chunks = s // CHUNK
    vn = vn.reshape(bsz, n_chunks, CHUNK, A_GROUPS, A_GROUP_DIM)
    causal = jnp.tril(jnp.ones((CHUNK, CHUNK), dtype=bool))
    w_causal = jnp.where(causal[None], w_s, 0)
    mixed = jnp.einsum('gts,bnsgc->bntgc', w_causal, vn) + b_s.T[None, None, :, :, None]
    a_out = a_u * mixed.reshape(bsz, s, A_WIDTH)

    z = g_c * b_x
    zp = jnp.pad(z, ((0, 0), (CONV_WIDTH - 1, 0), (0, 0)))
    y = conv_w[0] * zp[:, 0:s]
    for tap in range(1, CONV_WIDTH):
        y = y + conv_w[tap] * zp[:, tap:tap + s]
    b_out = g_b * y

    return jnp.concatenate([a_out, b_out], axis=-1) @ w_out


def dilated_branch(q, k, v, window, dilation):
    bsz, s, nh, dh = q.shape
    n_steps = window // dilation
    blk = ATTN_BLOCK
    L = s // dilation
    nb = -(-L // blk)
    Lp = nb * blk

    def to_blocks(t):
        t = t.reshape(bsz, L, dilation, nh, dh).transpose(0, 2, 3, 1, 4)
        t = jnp.pad(t, ((0, 0), (0, 0), (0, 0), (0, Lp - L), (0, 0)))
        return t.reshape(bsz, dilation, nh, nb, blk, dh)

    def with_prev_block(t):
        prev = jnp.pad(t, ((0, 0), (0, 0), (0, 0), (1, 0), (0, 0), (0, 0)))[:, :, :, :-1]
        return jnp.concatenate([prev, t], axis=4)

    qb = to_blocks(q)
    kk = with_prev_block(to_blocks(k))
    vv = with_prev_block(to_blocks(v))

    scores = jnp.einsum('brhnqc,brhnkc->brhnqk', qb, kk).astype(jnp.float32) * (dh ** -0.5)
    qi = jnp.arange(blk)[:, None]
    ki = jnp.arange(2 * blk)[None, :]
    step = qi + blk - ki
    band = (step >= 0) & (step <= n_steps)
    key_idx = jnp.arange(nb)[:, None, None] * blk - blk + ki[None]
    mask = band[None] & (key_idx >= 0)
    scores = jnp.where(mask, scores, -jnp.inf)
    m = jnp.max(scores, axis=-1, keepdims=True)
    p = jnp.exp(scores - m)
    denom = jnp.sum(p, axis=-1, keepdims=True)
    o = jnp.einsum('brhnqk,brhnkc->brhnqc', (p / denom).astype(v.dtype), vv)
    lse = (m + jnp.log(denom))[..., 0]

    o = o.reshape(bsz, dilation, nh, Lp, dh)[:, :, :, :L].transpose(0, 3, 1, 2, 4).reshape(bsz, s, nh, dh)
    lse = lse.reshape(bsz, dilation, nh, Lp)[..., :L].transpose(0, 3, 1, 2).reshape(bsz, s, nh)
    return o, lse


def dilated_attention_mixer(h, w_qkv, w_o, positions):
    bsz, s, _ = h.shape
    qkv = h @ w_qkv
    q, k, v = jnp.split(qkv, 3, axis=-1)
    q = rope(q.reshape(bsz, s, ATTN_HEADS, ATTN_HEAD_DIM), positions)
    k = rope(k.reshape(bsz, s, ATTN_HEADS, ATTN_HEAD_DIM), positions)
    v = v.reshape(bsz, s, ATTN_HEADS, ATTN_HEAD_DIM)
    outs = []
    lses = []
    for window, dilation in DILATED_BRANCHES:
        o_i, lse_i = dilated_branch(q, k, v, window, dilation)
        outs.append(o_i.astype(jnp.float32))
        lses.append(lse_i)
    alpha = jax.nn.softmax(jnp.stack(lses, axis=0), axis=0)
    o = jnp.einsum('ibsh,ibshc->bshc', alpha, jnp.stack(outs, axis=0))
    return o.astype(h.dtype).reshape(bsz, s, ATTN_HEADS * ATTN_HEAD_DIM) @ w_o


def squared_relu_mlp(h, w_up, w_down):
    return jnp.square(jax.nn.relu(h @ w_up)) @ w_down


def setup_inputs(seed: int = 0) -> dict:
    key = jax.random.key(seed)
    ks = jax.random.split(key, 14)
    f32 = jnp.float32
    d = D_MODEL
    x = jax.random.normal(ks[0], (BATCH, SEQ, d), f32)
    norm_mix_pre = 1.0 + 0.05 * jax.random.normal(ks[1], (DEPTH, d), f32)
    norm_mix_post = 1.0 + 0.05 * jax.random.normal(ks[2], (DEPTH, d), f32)
    norm_mlp_pre = 1.0 + 0.05 * jax.random.normal(ks[3], (DEPTH, d), f32)
    norm_mlp_post = 1.0 + 0.05 * jax.random.normal(ks[4], (DEPTH, d), f32)
    in_cols = 2 * A_WIDTH + 3 * B_WIDTH
    w_in_ab = jax.random.normal(ks[5], (N_EVEN, d, in_cols), f32) * d ** -0.5
    w_spatial = jax.random.normal(ks[6], (N_EVEN, A_GROUPS, CHUNK, CHUNK), f32) * (0.5 * CHUNK ** -0.5)
    b_spatial = 1.0 + 0.1 * jax.random.normal(ks[7], (N_EVEN, A_GROUPS, CHUNK), f32)
    conv_w = jax.random.normal(ks[8], (N_EVEN, CONV_WIDTH, B_WIDTH), f32) * CONV_WIDTH ** -0.5
    w_out_ab = jax.random.normal(ks[9], (N_EVEN, A_WIDTH + B_WIDTH, d), f32) * (A_WIDTH + B_WIDTH) ** -0.5
    attn_width = ATTN_HEADS * ATTN_HEAD_DIM
    w_qkv = jax.random.normal(ks[10], (N_ODD, d, 3 * attn_width), f32) * d ** -0.5
    w_o = jax.random.normal(ks[11], (N_ODD, attn_width, d), f32) * attn_width ** -0.5
    w_up = jax.random.normal(ks[12], (DEPTH, d, FFN_DIM), f32) * d ** -0.5
    w_down = jax.random.normal(ks[13], (DEPTH, FFN_DIM, d), f32) * FFN_DIM ** -0.5
    return {'x': x, 'norm_mix_pre': norm_mix_pre, 'norm_mix_post': norm_mix_post,
            'norm_mlp_pre': norm_mlp_pre, 'norm_mlp_post': norm_mlp_post,
            'w_in_ab': w_in_ab, 'w_spatial': w_spatial, 'b_spatial': b_spatial,
            'conv_w': conv_w, 'w_out_ab': w_out_ab, 'w_qkv': w_qkv, 'w_o': w_o,
            'w_up': w_up, 'w_down': w_down}


def reference(x, norm_mix_pre, norm_mix_post, norm_mlp_pre, norm_mlp_post,
              w_in_ab, w_spatial, b_spatial, conv_w, w_out_ab, w_qkv, w_o,
              w_up, w_down):
    positions = jnp.arange(x.shape[1], dtype=jnp.int32)
    h = x
    for layer in range(DEPTH):
        hn = rmsnorm(h, norm_mix_pre[layer])
        if layer % 2 == 0:
            e = layer // 2
            mix = gmlp_shortconv_mixer(hn, w_in_ab[e], w_spatial[e], b_spatial[e], conv_w[e], w_out_ab[e])
        else:
            o = layer // 2
            mix = dilated_attention_mixer(hn, w_qkv[o], w_o[o], positions)
        h = h + rmsnorm(mix, norm_mix_post[layer])
        f = squared_relu_mlp(rmsnorm(h, norm_mlp_pre[layer]), w_up[layer], w_down[layer])
        h = h + rmsnorm(f, norm_mlp_post[layer])
    return h
```

```python
import functools
import math

import jax
import jax.numpy as jnp
from jax import lax
from jax.experimental import pallas as pl
from jax.experimental.pallas import tpu as pltpu

F32 = jnp.float32
BF16 = jnp.bfloat16

D_MODEL = 2048
A_WIDTH = D_MODEL // 2
B_WIDTH = D_MODEL - A_WIDTH
A_GROUPS = 8
A_GROUP_DIM = A_WIDTH // A_GROUPS
CHUNK = 128
CONV_WIDTH = 3
HEAD_DIM = 128
N_HEADS = D_MODEL // HEAD_DIM
DILATED_BRANCHES = ((128, 1), (512, 4), (2048, 16))
ATTN_BLOCK = 128
ROPE_THETA = 10000.0
FFN_DIM = 4 * D_MODEL
RMS_EPS = 1e-6
LN_EPS = 1e-5

V7X_LANES = 128
V7X_VMEM_LIMIT_BYTES = 56 * 1024 * 1024

RMS_MATMUL_TM = 1024
RMS_MATMUL_TN = 512
MIXER_TM = 256
RES_TM = 512
MLP_TM = 1024
MLP_TF = 512
CONV_HALO = 8


def _compiler_params(semantics):
    return pltpu.CompilerParams(dimension_semantics=semantics,
                                vmem_limit_bytes=V7X_VMEM_LIMIT_BYTES)


def _rmsnorm(x, g):
    ms = jnp.mean(x * x, axis=-1, keepdims=True)
    return (x * lax.rsqrt(ms + RMS_EPS)) * g


def _rms_matmul_kernel(x_ref, g_ref, w_ref, *rest, epilogue, n_extra):
    extra_refs = rest[:n_extra]
    o_ref = rest[n_extra]
    hn_ref = rest[n_extra + 1]
    j = pl.program_id(1)

    @pl.when(j == 0)
    def _():
        hn_ref[...] = _rmsnorm(x_ref[...], g_ref[...]).astype(BF16)

    p = jnp.dot(hn_ref[...], w_ref[...], preferred_element_type=F32)
    epilogue(j, p, extra_refs, o_ref)


def _rms_matmul(x, g, w, epilogue, extra=(), extra_specs=(), out_dtype=F32):
    m, k = x.shape
    n = w.shape[1]
    tm, tn = RMS_MATMUL_TM, RMS_MATMUL_TN
    assert m % tm == 0 and n % tn == 0
    kern = functools.partial(_rms_matmul_kernel, epilogue=epilogue, n_extra=len(extra))
    return pl.pallas_call(
        kern,
        grid=(m // tm, n // tn),
        in_specs=[pl.BlockSpec((tm, k), lambda i, j: (i, 0)),
                  pl.BlockSpec((1, k), lambda i, j: (0, 0)),
                  pl.BlockSpec((k, tn), lambda i, j: (0, j)),
                  *extra_specs],
        out_specs=pl.BlockSpec((tm, tn), lambda i, j: (i, j)),
        out_shape=jax.ShapeDtypeStruct((m, n), out_dtype),
        scratch_shapes=[pltpu.VMEM((tm, k), BF16)],
        compiler_params=_compiler_params(("parallel", "arbitrary")),
        name="rms_matmul",
    )(x, g, w, *extra)


def _gelu_epilogue(j, p, extra_refs, o_ref, *, n_gelu_tiles):
    del extra_refs

    @pl.when(j < n_gelu_tiles)
    def _():
        o_ref[...] = jax.nn.gelu(p)

    @pl.when(j >= n_gelu_tiles)
    def _():
        o_ref[...] = p


def _rope_epilogue(j, p, extra_refs, o_ref, *, n_rope_tiles):
    cos_ref, sin_ref = extra_refs

    @pl.when(j < n_rope_tiles)
    def _():
        cos = cos_ref[...]
        sin = sin_ref[...]
        for hh in range(p.shape[1] // HEAD_DIM):
            t = p[:, hh * HEAD_DIM:(hh + 1) * HEAD_DIM]
            o_ref[:, hh * HEAD_DIM:(hh + 1) * HEAD_DIM] = (
                t * cos + pltpu.roll(t, HEAD_DIM // 2, 1) * sin)

    @pl.when(j >= n_rope_tiles)
    def _():
        o_ref[...] = p


def _mixer_kernel(au_ref, av_ref, gb_ref, gc_ref, bx_ref, gch_ref, bxh_ref,
                  ws_ref, bias_ref, cw_ref, o_ref, z_ref, *, tiles_per_seq):
    tm = au_ref.shape[0]
    i = pl.program_id(0)

    av = av_ref[...]
    mu = jnp.mean(av, axis=-1, keepdims=True)
    cen = av - mu
    var = jnp.mean(cen * cen, axis=-1, keepdims=True)
    vn = (cen * lax.rsqrt(var + LN_EPS)).astype(BF16)
    row = lax.broadcasted_iota(jnp.int32, (CHUNK, CHUNK), 0)
    col = lax.broadcasted_iota(jnp.int32, (CHUNK, CHUNK), 1)
    causal = col <= row
    for g in range(A_GROUPS):
        cs = slice(g * A_GROUP_DIM, (g + 1) * A_GROUP_DIM)
        w_causal = jnp.where(causal, ws_ref[g], 0.0).astype(BF16)
        for c in range(tm // CHUNK):
            rs = slice(c * CHUNK, (c + 1) * CHUNK)
            mixed = jnp.dot(w_causal, vn[rs, cs], preferred_element_type=F32) + bias_ref[:, cs]
            o_ref[rs, cs] = (au_ref[rs, cs] * mixed).astype(BF16)

    z = gc_ref[...] * bx_ref[...]
    at_seq_start = (i % tiles_per_seq) == 0
    z_halo = jnp.where(at_seq_start, 0.0, gch_ref[...] * bxh_ref[...])
    z_ref[0:CONV_HALO, :] = z_halo
    z_ref[CONV_HALO:CONV_HALO + tm, :] = z
    y = (cw_ref[2:3, :] * z
         + cw_ref[1:2, :] * z_ref[CONV_HALO - 1:CONV_HALO - 1 + tm, :]
         + cw_ref[0:1, :] * z_ref[CONV_HALO - 2:CONV_HALO - 2 + tm, :])
    o_ref[:, A_WIDTH:] = (gb_ref[...] * y).astype(BF16)


def _mixer(proj, w_spatial, bias_full, conv_w, seq):
    m = proj.shape[0]
    tm = MIXER_TM
    assert seq % tm == 0 and tm % CHUNK == 0 and A_WIDTH == B_WIDTH
    tiles_per_seq = seq // tm
    halo_blocks = tm // CONV_HALO
    seg = lambda s: pl.BlockSpec((tm, A_WIDTH), lambda i, s=s: (i, s))
    halo = lambda s: pl.BlockSpec(
        (CONV_HALO, A_WIDTH), lambda i, s=s: (jnp.maximum(i * halo_blocks - 1, 0), s))
    kern = functools.partial(_mixer_kernel, tiles_per_seq=tiles_per_seq)
    return pl.pallas_call(
        kern,
        grid=(m // tm,),
        in_specs=[seg(0), seg(1), seg(2), seg(3), seg(4), halo(3), halo(4),
                  pl.BlockSpec((A_GROUPS, CHUNK, CHUNK), lambda i: (0, 0, 0)),
                  pl.BlockSpec((CHUNK, A_WIDTH), lambda i: (0, 0)),
                  pl.BlockSpec((CONV_WIDTH, B_WIDTH), lambda i: (0, 0))],
        out_specs=pl.BlockSpec((tm, D_MODEL), lambda i: (i, 0)),
        out_shape=jax.ShapeDtypeStruct((m, D_MODEL), BF16),
        scratch_shapes=[pltpu.VMEM((CONV_HALO + tm, B_WIDTH), F32)],
        compiler_params=_compiler_params(("parallel",)),
        name="mixer",
    )(proj, proj, proj, proj, proj, proj, proj, w_spatial, bias_full, conv_w)


def _attn_kernel(q_ref, k_ref, v_ref, o_ref, qs_ref, ks_ref, vs_ref, acc_ref, m_ref, l_ref,
                 *, seq, dilations):
    blk = ATTN_BLOCK
    scale = HEAD_DIM ** -0.5
    acc_ref[...] = jnp.zeros_like(acc_ref)
    l_ref[...] = jnp.zeros_like(l_ref)
    m_ref[...] = jnp.full_like(m_ref, -jnp.inf)

    qi = lax.broadcasted_iota(jnp.int32, (blk, 2 * blk), 0)
    ki = lax.broadcasted_iota(jnp.int32, (blk, 2 * blk), 1)
    step = qi + blk - ki
    band = (step >= 0) & (step <= blk)
    in_cur = ki >= blk

    for d in dilations:
        sub_len = seq // d
        n_blocks = sub_len // blk

        def residue_body(r, carry, d=d, sub_len=sub_len, n_blocks=n_blocks):
            rows_all = pl.ds(r, sub_len, stride=d)
            qs_ref[0:sub_len, :] = q_ref[rows_all, :].astype(BF16)
            ks_ref[0:sub_len, :] = k_ref[rows_all, :].astype(BF16)
            vs_ref[0:sub_len, :] = v_ref[rows_all, :].astype(BF16)

            def block_body(n, carry2):
                cur = pl.multiple_of(n * blk, blk)
                prev = pl.multiple_of(jnp.maximum(n - 1, 0) * blk, blk)
                qb = qs_ref[pl.ds(cur, blk), :]
                kw = jnp.concatenate([ks_ref[pl.ds(prev, blk), :], ks_ref[pl.ds(cur, blk), :]], axis=0)
                vw = jnp.concatenate([vs_ref[pl.ds(prev, blk), :], vs_ref[pl.ds(cur, blk), :]], axis=0)
                s = lax.dot_general(qb, kw, (((1,), (1,)), ((), ())),
                                    preferred_element_type=F32) * scale
                mask = band & (in_cur | (n > 0))
                s = jnp.where(mask, s, -jnp.inf)
                rows = pl.ds(r + d * cur, blk, stride=d)
                m_old = m_ref[rows, :]
                l_old = l_ref[rows, :]
                a_old = acc_ref[rows, :]
                m_blk = jnp.max(s, axis=-1, keepdims=True)
                m_new = jnp.maximum(m_old, m_blk)
                p = jnp.exp(s - jnp.concatenate([m_new, m_new], axis=1))
                corr = jnp.exp(m_old - m_new)
                l_ref[rows, :] = l_old * corr + jnp.sum(p, axis=-1, keepdims=True)
                acc_ref[rows, :] = a_old * corr + jnp.dot(p.astype(BF16), vw,
                                                          preferred_element_type=F32)
                m_ref[rows, :] = m_new
                return carry2

            lax.fori_loop(0, n_blocks, block_body, 0)
            return carry

        lax.fori_loop(0, d, residue_body, 0)

    o_ref[...] = acc_ref[...] / l_ref[...]


def _attention(qkv, batch, seq):
    dilations = tuple(d for _, d in DILATED_BRANCHES)
    for window, d in DILATED_BRANCHES:
        assert window // d == ATTN_BLOCK and seq % (d * ATTN_BLOCK) == 0
    qkv3 = qkv.reshape(batch, seq, 3 * D_MODEL)
    spec = lambda part: pl.BlockSpec((None, seq, HEAD_DIM),
                                     lambda b, h, part=part: (b, 0, part * N_HEADS + h))
    kern = functools.partial(_attn_kernel, seq=seq, dilations=dilations)
    out = pl.pallas_call(
        kern,
        grid=(batch, N_HEADS),
        in_specs=[spec(0), spec(1), spec(2)],
        out_specs=pl.BlockSpec((None, seq, HEAD_DIM), lambda b, h: (b, 0, h)),
        out_shape=jax.ShapeDtypeStruct((batch, seq, D_MODEL), F32),
        scratch_shapes=[pltpu.VMEM((seq, HEAD_DIM), BF16)] * 3
                       + [pltpu.VMEM((seq, HEAD_DIM), F32)] * 3,
        compiler_params=_compiler_params(("parallel", "parallel")),
        name="attention",
    )(qkv3, qkv3, qkv3)
    return out.reshape(batch * seq, D_MODEL)


def _matmul_rms_res_kernel(a_ref, w_ref, g_ref, h_ref, o_ref):
    f = jnp.dot(a_ref[...].astype(BF16), w_ref[...], preferred_element_type=F32)
    o_ref[...] = h_ref[...] + _rmsnorm(f, g_ref[...])


def _matmul_rms_res(a, w, g, h):
    m, k = a.shape
    n = w.shape[1]
    tm = RES_TM
    assert m % tm == 0
    return pl.pallas_call(
        _matmul_rms_res_kernel,
        grid=(m // tm,),
        in_specs=[pl.BlockSpec((tm, k), lambda i: (i, 0)),
                  pl.BlockSpec((k, n), lambda i: (0, 0)),
                  pl.BlockSpec((1, n), lambda i: (0, 0)),
                  pl.BlockSpec((tm, n), lambda i: (i, 0))],
        out_specs=pl.BlockSpec((tm, n), lambda i: (i, 0)),
        out_shape=jax.ShapeDtypeStruct((m, n), F32),
        compiler_params=_compiler_params(("parallel",)),
        name="matmul_rms_res",
    )(a, w, g, h)


def _mlp_kernel(x_ref, g_pre_ref, w_up_ref, w_down_ref, g_post_ref, o_ref, hn_ref):
    k = pl.program_id(1)

    @pl.when(k == 0)
    def _():
        hn_ref[...] = _rmsnorm(x_ref[...], g_pre_ref[...]).astype(BF16)

    u = jnp.dot(hn_ref[...], w_up_ref[...], preferred_element_type=F32)
    u = jnp.square(jnp.maximum(u, 0.0)).astype(BF16)
    part = jnp.dot(u, w_down_ref[...], preferred_element_type=F32)

    @pl.when(k == 0)
    def _():
        o_ref[...] = part

    @pl.when(k > 0)
    def _():
        o_ref[...] += part

    @pl.when(k == pl.num_programs(1) - 1)
    def _():
        o_ref[...] = x_ref[...] + _rmsnorm(o_ref[...], g_post_ref[...])


def _mlp(h, g_pre, w_up, w_down, g_post):
    m, d = h.shape
    f = w_up.shape[1]
    tm, tf = MLP_TM, MLP_TF
    assert m % tm == 0 and f % tf == 0
    return pl.pallas_call(
        _mlp_kernel,
        grid=(m // tm, f // tf),
        in_specs=[pl.BlockSpec((tm, d), lambda i, k: (i, 0), pipeline_mode=pl.Buffered(1)),
                  pl.BlockSpec((1, d), lambda i, k: (0, 0)),
                  pl.BlockSpec((d, tf), lambda i, k: (0, k)),
                  pl.BlockSpec((tf, d), lambda i, k: (k, 0)),
                  pl.BlockSpec((1, d), lambda i, k: (0, 0))],
        out_specs=pl.BlockSpec((tm, d), lambda i, k: (i, 0)),
        out_shape=jax.ShapeDtypeStruct((m, d), F32),
        scratch_shapes=[pltpu.VMEM((tm, d), BF16)],
        compiler_params=_compiler_params(("parallel", "arbitrary")),
        name="mlp",
    )(h, g_pre, w_up, w_down, g_post)


def _rope_tables(seq):
    half = HEAD_DIM // 2
    inv_freq = ROPE_THETA ** (-jnp.arange(half, dtype=F32) * 2.0 / HEAD_DIM)
    ang = jnp.arange(seq, dtype=jnp.int32).astype(F32)[:, None] * inv_freq[None, :]
    cos = jnp.cos(ang)
    sin = jnp.sin(ang)
    return jnp.concatenate([cos, cos], axis=-1), jnp.concatenate([-sin, sin], axis=-1)


def kernel(x, norm_mix_pre, norm_mix_post, norm_mlp_pre, norm_mlp_post, w_in_ab, w_spatial,
           b_spatial, conv_w, w_out_ab, w_qkv, w_o, w_up, w_down):
    batch, seq, d = x.shape
    assert d == D_MODEL
    depth = norm_mix_pre.shape[0]
    m = batch * seq
    h = x.reshape(m, d)
    cos_tab, sin_tab = _rope_tables(seq)
    tiles_per_seq = seq // RMS_MATMUL_TM
    rope_specs = [pl.BlockSpec((RMS_MATMUL_TM, HEAD_DIM), lambda i, j: (i % tiles_per_seq, 0))] * 2
    gelu_ep = functools.partial(_gelu_epilogue, n_gelu_tiles=2 * A_WIDTH // RMS_MATMUL_TN)
    rope_ep = functools.partial(_rope_epilogue, n_rope_tiles=2 * D_MODEL // RMS_MATMUL_TN)

    for layer in range(depth):
        g_pre = norm_mix_pre[layer][None, :]
        g_post = norm_mix_post[layer][None, :]
        if layer % 2 == 0:
            e = layer // 2
            proj = _rms_matmul(h, g_pre, w_in_ab[e].astype(BF16), gelu_ep)
            bias_full = jnp.repeat(b_spatial[e].T, A_GROUP_DIM, axis=1)
            mixed = _mixer(proj, w_spatial[e], bias_full, conv_w[e], seq)
            h = _matmul_rms_res(mixed, w_out_ab[e].astype(BF16), g_post, h)
        else:
            o = layer // 2
            qkv = _rms_matmul(h, g_pre, w_qkv[o].astype(BF16), rope_ep,
                              extra=(cos_tab, sin_tab), extra_specs=rope_specs)
            att = _attention(qkv, batch, seq)
            h = _matmul_rms_res(att, w_o[o].astype(BF16), g_post, h)
        h = _mlp(h, norm_mlp_pre[layer][None, :], w_up[layer].astype(BF16),
                 w_down[layer].astype(BF16), norm_mlp_post[layer][None, :])
    return h.reshape(batch, seq, d)
```

```python
import functools
import math

import jax
import jax.numpy as jnp
import numpy as np
from jax import lax
from jax.experimental import pallas as pl
from jax.experimental.pallas import tpu as pltpu

F32 = jnp.float32
BF16 = jnp.bfloat16

D_MODEL = 2048
A_WIDTH = D_MODEL // 2
B_WIDTH = D_MODEL - A_WIDTH
A_GROUPS = 8
A_GROUP_DIM = A_WIDTH // A_GROUPS
CHUNK = 128
CONV_WIDTH = 3
HEAD_DIM = 128
N_HEADS = D_MODEL // HEAD_DIM
DILATED_BRANCHES = ((128, 1), (512, 4), (2048, 16))
ATTN_BLOCK = 128
ATTN_PERM = 16
ATTN_PERM_STEP = 4
ATTN_GROUP = 16
ROPE_THETA = 10000.0
FFN_DIM = 4 * D_MODEL
RMS_EPS = 1e-6
LN_EPS = 1e-5

V7X_LANES = 128
V7X_VMEM_LIMIT_BYTES = 56 * 1024 * 1024

RMS_MATMUL_TM = 1024
RMS_MATMUL_TN = 512
MIXER_TM = 256
RES_TM = 512
MLP_TM = 1024
MLP_TF = 512
CONV_HALO = 8


def _compiler_params(semantics):
    return pltpu.CompilerParams(dimension_semantics=semantics,
                                vmem_limit_bytes=V7X_VMEM_LIMIT_BYTES)


def _rmsnorm(x, g):
    ms = jnp.mean(x * x, axis=-1, keepdims=True)
    return (x * lax.rsqrt(ms + RMS_EPS)) * g


def _rms_matmul_kernel(x_ref, g_ref, w_ref, *rest, epilogue, n_extra):
    extra_refs = rest[:n_extra]
    o_ref = rest[n_extra]
    hn_ref = rest[n_extra + 1]
    j = pl.program_id(1)

    @pl.when(j == 0)
    def _():
        hn_ref[...] = _rmsnorm(x_ref[...], g_ref[...]).astype(BF16)

    p = jnp.dot(hn_ref[...], w_ref[...], preferred_element_type=F32)
    epilogue(j, p, extra_refs, o_ref)


def _rms_matmul(x, g, w, epilogue, extra=(), extra_specs=(), out_dtype=F32):
    m, k = x.shape
    n = w.shape[1]
    tm, tn = RMS_MATMUL_TM, RMS_MATMUL_TN
    assert m % tm == 0 and n % tn == 0
    kern = functools.partial(_rms_matmul_kernel, epilogue=epilogue, n_extra=len(extra))
    return pl.pallas_call(
        kern,
        grid=(m // tm, n // tn),
        in_specs=[pl.BlockSpec((tm, k), lambda i, j: (i, 0)),
                  pl.BlockSpec((1, k), lambda i, j: (0, 0)),
                  pl.BlockSpec((k, tn), lambda i, j: (0, j)),
                  *extra_specs],
        out_specs=pl.BlockSpec((tm, tn), lambda i, j: (i, j)),
        out_shape=jax.ShapeDtypeStruct((m, n), out_dtype),
        scratch_shapes=[pltpu.VMEM((tm, k), BF16)],
        compiler_params=_compiler_params(("parallel", "arbitrary")),
        name="rms_matmul",
    )(x, g, w, *extra)


def _gelu_epilogue(j, p, extra_refs, o_ref, *, n_gelu_tiles):
    del extra_refs

    @pl.when(j < n_gelu_tiles)
    def _():
        o_ref[...] = jax.nn.gelu(p)

    @pl.when(j >= n_gelu_tiles)
    def _():
        o_ref[...] = p


def _rope_epilogue(j, p, extra_refs, o_ref, *, n_rope_tiles):
    cos_ref, sin_ref = extra_refs

    @pl.when(j < n_rope_tiles)
    def _():
        cos = cos_ref[...]
        sin = sin_ref[...]
        for hh in range(p.shape[1] // HEAD_DIM):
            t = p[:, hh * HEAD_DIM:(hh + 1) * HEAD_DIM]
            o_ref[:, hh * HEAD_DIM:(hh + 1) * HEAD_DIM] = (
                t * cos + pltpu.roll(t, HEAD_DIM // 2, 1) * sin)

    @pl.when(j >= n_rope_tiles)
    def _():
        o_ref[...] = p


def _mixer_kernel(au_ref, av_ref, gb_ref, gc_ref, bx_ref, gch_ref, bxh_ref,
                  ws_ref, bias_ref, cw_ref, o_ref, z_ref, *, tiles_per_seq):
    tm = au_ref.shape[0]
    i = pl.program_id(0)

    av = av_ref[...]
    mu = jnp.mean(av, axis=-1, keepdims=True)
    cen = av - mu
    var = jnp.mean(cen * cen, axis=-1, keepdims=True)
    vn = (cen * lax.rsqrt(var + LN_EPS)).astype(BF16)
    row = lax.broadcasted_iota(jnp.int32, (CHUNK, CHUNK), 0)
    col = lax.broadcasted_iota(jnp.int32, (CHUNK, CHUNK), 1)
    causal = col <= row
    for g in range(A_GROUPS):
        cs = slice(g * A_GROUP_DIM, (g + 1) * A_GROUP_DIM)
        w_causal = jnp.where(causal, ws_ref[g], 0.0).astype(BF16)
        for c in range(tm // CHUNK):
            rs = slice(c * CHUNK, (c + 1) * CHUNK)
            mixed = jnp.dot(w_causal, vn[rs, cs], preferred_element_type=F32) + bias_ref[:, cs]
            o_ref[rs, cs] = (au_ref[rs, cs] * mixed).astype(BF16)

    z = gc_ref[...] * bx_ref[...]
    at_seq_start = (i % tiles_per_seq) == 0
    z_halo = jnp.where(at_seq_start, 0.0, gch_ref[...] * bxh_ref[...])
    z_ref[0:CONV_HALO, :] = z_halo
    z_ref[CONV_HALO:CONV_HALO + tm, :] = z
    y = (cw_ref[2:3, :] * z
         + cw_ref[1:2, :] * z_ref[CONV_HALO - 1:CONV_HALO - 1 + tm, :]
         + cw_ref[0:1, :] * z_ref[CONV_HALO - 2:CONV_HALO - 2 + tm, :])
    o_ref[:, A_WIDTH:] = (gb_ref[...] * y).astype(BF16)


def _mixer(proj, w_spatial, bias_full, conv_w, seq):
    m = proj.shape[0]
    tm = MIXER_TM
    assert seq % tm == 0 and tm % CHUNK == 0 and A_WIDTH == B_WIDTH
    tiles_per_seq = seq // tm
    halo_blocks = tm // CONV_HALO
    seg = lambda s: pl.BlockSpec((tm, A_WIDTH), lambda i, s=s: (i, s))
    halo = lambda s: pl.BlockSpec(
        (CONV_HALO, A_WIDTH), lambda i, s=s: (jnp.maximum(i * halo_blocks - 1, 0), s))
    kern = functools.partial(_mixer_kernel, tiles_per_seq=tiles_per_seq)
    return pl.pallas_call(
        kern,
        grid=(m // tm,),
        in_specs=[seg(0), seg(1), seg(2), seg(3), seg(4), halo(3), halo(4),
                  pl.BlockSpec((A_GROUPS, CHUNK, CHUNK), lambda i: (0, 0, 0)),
                  pl.BlockSpec((CHUNK, A_WIDTH), lambda i: (0, 0)),
                  pl.BlockSpec((CONV_WIDTH, B_WIDTH), lambda i: (0, 0))],
        out_specs=pl.BlockSpec((tm, D_MODEL), lambda i: (i, 0)),
        out_shape=jax.ShapeDtypeStruct((m, D_MODEL), BF16),
        scratch_shapes=[pltpu.VMEM((CONV_HALO + tm, B_WIDTH), F32)],
        compiler_params=_compiler_params(("parallel",)),
        name="mixer",
    )(proj, proj, proj, proj, proj, proj, proj, w_spatial, bias_full, conv_w)


def _attn_band_bias():
    blk = ATTN_BLOCK
    tables = []
    for _, d in DILATED_BRANCHES:
        pieces = ATTN_PERM // d
        rows = blk // pieces
        i = np.arange(blk)
        run = i // rows
        if pieces == ATTN_PERM:
            run = ATTN_PERM_STEP * (run % ATTN_PERM_STEP) + run // ATTN_PERM_STEP
        uq = pieces * (i % rows) + run
        uk = np.concatenate([uq, blk + uq])
        step = uq[:, None] + blk - uk[None, :]
        valid = (step >= 0) & (step <= blk)
        first = valid & (uk[None, :] >= blk)
        tables.append(np.stack([np.where(valid, 0.0, -np.inf), np.where(first, 0.0, -np.inf)]))
    return np.stack(tables).astype(np.float32)


def _attn_block(qb, k_prev, k_cur, v_prev, v_cur, bias, m_old, l_old, a_old):
    kw = jnp.concatenate([k_prev, k_cur], axis=0)
    vw = jnp.concatenate([v_prev, v_cur], axis=0)
    s = lax.dot_general(qb, kw, (((1,), (1,)), ((), ())), preferred_element_type=F32)
    s = s * (HEAD_DIM ** -0.5 * math.log2(math.e)) + bias
    m_blk = jnp.max(s, axis=-1, keepdims=True)
    m_new = jnp.maximum(m_old, m_blk)
    p = jnp.exp2(s - jnp.concatenate([m_new, m_new], axis=1))
    corr = jnp.exp2(m_old - m_new)
    l_new = l_old * corr + jnp.sum(p, axis=-1, keepdims=True)
    a_new = a_old * corr + jnp.dot(p.astype(BF16), vw, preferred_element_type=F32)
    return m_new, l_new, a_new


def _attn_kernel(q_ref, k_ref, v_ref, bias_ref, o_ref, qp_ref, kp_ref, vp_ref,
                 acc_ref, m_ref, l_ref, tmp_ref, *, seq, dilations):
    blk = ATTN_BLOCK
    lp = seq // ATTN_PERM
    step = ATTN_PERM_STEP
    lq = seq // step
    for src_ref, dst_ref in ((q_ref, qp_ref), (k_ref, kp_ref), (v_ref, vp_ref)):
        for lo in range(step):
            tmp_ref[lo * lq:(lo + 1) * lq, :] = src_ref[pl.ds(lo, lq, stride=step), :]
        for lo in range(step):
            for hi in range(step):
                run = lo * step + hi
                dst_ref[run * lp:(run + 1) * lp, :] = tmp_ref[pl.ds(lo * lq + hi, lp, stride=step), :]
    acc_ref[...] = jnp.zeros_like(acc_ref)
    l_ref[...] = jnp.zeros_like(l_ref)
    m_ref[...] = jnp.full_like(m_ref, -jnp.inf)

    for bi, d in enumerate(dilations):
        pieces = ATTN_PERM // d
        rows = blk // pieces
        nb = lp // rows
        gn = min(nb, ATTN_GROUP)
        gs = ATTN_GROUP // gn
        n_groups = nb // gn
        assert nb % gn == 0 and d % gs == 0

        def starts(r_sub, n, d=d, pieces=pieces, rows=rows):
            return [pl.multiple_of((r_sub * pieces + a) * lp + n * rows, 8) for a in range(pieces)]

        def gather(ref, st, rows=rows):
            parts = [ref[pl.ds(s0, rows), :] for s0 in st]
            return parts[0] if len(parts) == 1 else jnp.concatenate(parts, axis=0)

        def scatter(ref, st, val, rows=rows):
            for a, s0 in enumerate(st):
                ref[pl.ds(s0, rows), :] = val[a * rows:(a + 1) * rows]

        def body(it, carry, bi=bi, gn=gn, gs=gs, n_groups=n_groups):
            sg = it // n_groups
            n0 = (it % n_groups) * gn
            work = []
            for si in range(gs):
                r_sub = sg * gs + si
                st_prev = starts(r_sub, jnp.maximum(n0 - 1, 0))
                kb = [gather(kp_ref, st_prev).astype(BF16)]
                vb = [gather(vp_ref, st_prev).astype(BF16)]
                for j in range(gn):
                    st = starts(r_sub, n0 + j)
                    kb.append(gather(kp_ref, st).astype(BF16))
                    vb.append(gather(vp_ref, st).astype(BF16))
                    if j > 0:
                        bias = bias_ref[bi, 0]
                    elif n_groups == 1:
                        bias = bias_ref[bi, 1]
                    else:
                        bias = bias_ref[bi, jnp.where(n0 == 0, 1, 0)]
                    work.append((st, gather(qp_ref, st).astype(BF16), kb[j], kb[j + 1],
                                 vb[j], vb[j + 1], bias,
                                 gather(m_ref, st), gather(l_ref, st), gather(acc_ref, st)))
            results = [(w[0],) + _attn_block(*w[1:]) for w in work]
            for st, m_new, l_new, a_new in results:
                scatter(m_ref, st, m_new)
                scatter(l_ref, st, l_new)
                scatter(acc_ref, st, a_new)
            return carry

        lax.fori_loop(0, (d // gs) * n_groups, body, 0)

    for lo in range(step):
        for hi in range(step):
            src = slice((lo * step + hi) * lp, (lo * step + hi + 1) * lp)
            tmp_ref[pl.ds(lo * lq + hi, lp, stride=step), :] = acc_ref[src, :] / l_ref[src, :]
    for lo in range(step):
        o_ref[pl.ds(lo, lq, stride=step), :] = tmp_ref[lo * lq:(lo + 1) * lq, :]


def _attention(qkv, batch, seq):
    dilations = tuple(d for _, d in DILATED_BRANCHES)
    for window, d in DILATED_BRANCHES:
        assert window // d == ATTN_BLOCK and seq % (d * ATTN_BLOCK) == 0
        assert ATTN_PERM % d == 0 and ATTN_BLOCK % (ATTN_PERM // d) == 0
    qkv3 = qkv.reshape(batch, seq, 3 * D_MODEL)
    bias = jnp.asarray(_attn_band_bias())
    spec = lambda part: pl.BlockSpec((None, seq, HEAD_DIM),
                                     lambda b, h, part=part: (b, 0, part * N_HEADS + h))
    kern = functools.partial(_attn_kernel, seq=seq, dilations=dilations)
    out = pl.pallas_call(
        kern,
        grid=(batch, N_HEADS),
        in_specs=[spec(0), spec(1), spec(2),
                  pl.BlockSpec(bias.shape, lambda b, h: (0, 0, 0, 0))],
        out_specs=pl.BlockSpec((None, seq, HEAD_DIM), lambda b, h: (b, 0, h)),
        out_shape=jax.ShapeDtypeStruct((batch, seq, D_MODEL), F32),
        scratch_shapes=[pltpu.VMEM((seq, HEAD_DIM), F32)] * 7,
        compiler_params=_compiler_params(("parallel", "parallel")),
        name="attention",
    )(qkv3, qkv3, qkv3, bias)
    return out.reshape(batch * seq, D_MODEL)


def _matmul_rms_res_kernel(a_ref, w_ref, g_ref, h_ref, o_ref):
    f = jnp.dot(a_ref[...].astype(BF16), w_ref[...], preferred_element_type=F32)
    o_ref[...] = h_ref[...] + _rmsnorm(f, g_ref[...])


def _matmul_rms_res(a, w, g, h):
    m, k = a.shape
    n = w.shape[1]
    tm = RES_TM
    assert m % tm == 0
    return pl.pallas_call(
        _matmul_rms_res_kernel,
        grid=(m // tm,),
        in_specs=[pl.BlockSpec((tm, k), lambda i: (i, 0)),
                  pl.BlockSpec((k, n), lambda i: (0, 0)),
                  pl.BlockSpec((1, n), lambda i: (0, 0)),
                  pl.BlockSpec((tm, n), lambda i: (i, 0))],
        out_specs=pl.BlockSpec((tm, n), lambda i: (i, 0)),
        out_shape=jax.ShapeDtypeStruct((m, n), F32),
        compiler_params=_compiler_params(("parallel",)),
        name="matmul_rms_res",
    )(a, w, g, h)


def _mlp_kernel(x_ref, g_pre_ref, w_up_ref, w_down_ref, g_post_ref, o_ref, hn_ref):
    k = pl.program_id(1)

    @pl.when(k == 0)
    def _():
        hn_ref[...] = _rmsnorm(x_ref[...], g_pre_ref[...]).astype(BF16)

    u = jnp.dot(hn_ref[...], w_up_ref[...], preferred_element_type=F32)
    u = jnp.square(jnp.maximum(u, 0.0)).astype(BF16)
    part = jnp.dot(u, w_down_ref[...], preferred_element_type=F32)

    @pl.when(k == 0)
    def _():
        o_ref[...] = part

    @pl.when(k > 0)
    def _():
        o_ref[...] += part

    @pl.when(k == pl.num_programs(1) - 1)
    def _():
        o_ref[...] = x_ref[...] + _rmsnorm(o_ref[...], g_post_ref[...])


def _mlp(h, g_pre, w_up, w_down, g_post):
    m, d = h.shape
    f = w_up.shape[1]
    tm, tf = MLP_TM, MLP_TF
    assert m % tm == 0 and f % tf == 0
    return pl.pallas_call(
        _mlp_kernel,
        grid=(m // tm, f // tf),
        in_specs=[pl.BlockSpec((tm, d), lambda i, k: (i, 0), pipeline_mode=pl.Buffered(1)),
                  pl.BlockSpec((1, d), lambda i, k: (0, 0)),
                  pl.BlockSpec((d, tf), lambda i, k: (0, k)),
                  pl.BlockSpec((tf, d), lambda i, k: (k, 0)),
                  pl.BlockSpec((1, d), lambda i, k: (0, 0))],
        out_specs=pl.BlockSpec((tm, d), lambda i, k: (i, 0)),
        out_shape=jax.ShapeDtypeStruct((m, d), F32),
        scratch_shapes=[pltpu.VMEM((tm, d), BF16)],
        compiler_params=_compiler_params(("parallel", "arbitrary")),
        name="mlp",
    )(h, g_pre, w_up, w_down, g_post)


def _rope_tables(seq):
    half = HEAD_DIM // 2
    inv_freq = ROPE_THETA ** (-jnp.arange(half, dtype=F32) * 2.0 / HEAD_DIM)
    ang = jnp.arange(seq, dtype=jnp.int32).astype(F32)[:, None] * inv_freq[None, :]
    cos = jnp.cos(ang)
    sin = jnp.sin(ang)
    return jnp.concatenate([cos, cos], axis=-1), jnp.concatenate([-sin, sin], axis=-1)


def kernel(x, norm_mix_pre, norm_mix_post, norm_mlp_pre, norm_mlp_post, w_in_ab, w_spatial,
           b_spatial, conv_w, w_out_ab, w_qkv, w_o, w_up, w_down):
    batch, seq, d = x.shape
    assert d == D_MODEL
    depth = norm_mix_pre.shape[0]
    m = batch * seq
    h = x.reshape(m, d)
    cos_tab, sin_tab = _rope_tables(seq)
    tiles_per_seq = seq // RMS_MATMUL_TM
    rope_specs = [pl.BlockSpec((RMS_MATMUL_TM, HEAD_DIM), lambda i, j: (i % tiles_per_seq, 0))] * 2
    gelu_ep = functools.partial(_gelu_epilogue, n_gelu_tiles=2 * A_WIDTH // RMS_MATMUL_TN)
    rope_ep = functools.partial(_rope_epilogue, n_rope_tiles=2 * D_MODEL // RMS_MATMUL_TN)

    for layer in range(depth):
        g_pre = norm_mix_pre[layer][None, :]
        g_post = norm_mix_post[layer][None, :]
        if layer % 2 == 0:
            e = layer // 2
            proj = _rms_matmul(h, g_pre, w_in_ab[e].astype(BF16), gelu_ep)
            bias_full = jnp.repeat(b_spatial[e].T, A_GROUP_DIM, axis=1)
            mixed = _mixer(proj, w_spatial[e], bias_full, conv_w[e], seq)
            h = _matmul_rms_res(mixed, w_out_ab[e].astype(BF16), g_post, h)
        else:
            o = layer // 2
            qkv = _rms_matmul(h, g_pre, w_qkv[o].astype(BF16), rope_ep,
                              extra=(cos_tab, sin_tab), extra_specs=rope_specs)
            att = _attention(qkv, batch, seq)
            h = _matmul_rms_res(att, w_o[o].astype(BF16), g_post, h)
        h = _mlp(h, norm_mlp_pre[layer][None, :], w_up[layer].astype(BF16),
                 w_down[layer].astype(BF16), norm_mlp_post[layer][None, :])
    return h.reshape(batch, seq, d)
```

```python
import functools
import math

import jax
import jax.numpy as jnp
import numpy as np
from jax import lax
from jax.experimental import pallas as pl
from jax.experimental.pallas import tpu as pltpu

F32 = jnp.float32
BF16 = jnp.bfloat16

D_MODEL = 2048
A_WIDTH = D_MODEL // 2
B_WIDTH = D_MODEL - A_WIDTH
A_GROUPS = 8
A_GROUP_DIM = A_WIDTH // A_GROUPS
CHUNK = 128
CONV_WIDTH = 3
HEAD_DIM = 128
N_HEADS = D_MODEL // HEAD_DIM
DILATED_BRANCHES = ((128, 1), (512, 4), (2048, 16))
ATTN_BLOCK = 128
ATTN_PERM = 16
ATTN_PERM_STEP = 4
ATTN_GROUP = 16
ROPE_THETA = 10000.0
FFN_DIM = 4 * D_MODEL
RMS_EPS = 1e-6
LN_EPS = 1e-5

V7X_LANES = 128
V7X_VMEM_LIMIT_BYTES = 56 * 1024 * 1024

RMS_MATMUL_TM = 1024
RMS_MATMUL_TN = 512
MIXER_TM = 256
RES_TM = 512
MLP_TM = 1024
MLP_TF = 512
CONV_HALO = 8


def _compiler_params(semantics):
    return pltpu.CompilerParams(dimension_semantics=semantics,
                                vmem_limit_bytes=V7X_VMEM_LIMIT_BYTES)


def _rmsnorm(x, g):
    ms = jnp.mean(x * x, axis=-1, keepdims=True)
    return (x * lax.rsqrt(ms + RMS_EPS)) * g


def _rms_matmul_kernel(x_ref, g_ref, w_ref, *rest, epilogue, n_extra):
    extra_refs = rest[:n_extra]
    o_ref = rest[n_extra]
    hn_ref = rest[n_extra + 1]
    j = pl.program_id(1)

    @pl.when(j == 0)
    def _():
        hn_ref[...] = _rmsnorm(x_ref[...], g_ref[...]).astype(BF16)

    p = jnp.dot(hn_ref[...], w_ref[...], preferred_element_type=F32)
    epilogue(j, p, extra_refs, o_ref)


def _rms_matmul(x, g, w, epilogue, extra=(), extra_specs=(), out_dtype=F32):
    m, k = x.shape
    n = w.shape[1]
    tm, tn = RMS_MATMUL_TM, RMS_MATMUL_TN
    assert m % tm == 0 and n % tn == 0
    kern = functools.partial(_rms_matmul_kernel, epilogue=epilogue, n_extra=len(extra))
    return pl.pallas_call(
        kern,
        grid=(m // tm, n // tn),
        in_specs=[pl.BlockSpec((tm, k), lambda i, j: (i, 0)),
                  pl.BlockSpec((1, k), lambda i, j: (0, 0)),
                  pl.BlockSpec((k, tn), lambda i, j: (0, j)),
                  *extra_specs],
        out_specs=pl.BlockSpec((tm, tn), lambda i, j: (i, j)),
        out_shape=jax.ShapeDtypeStruct((m, n), out_dtype),
        scratch_shapes=[pltpu.VMEM((tm, k), BF16)],
        compiler_params=_compiler_params(("parallel", "arbitrary")),
        name="rms_matmul",
    )(x, g, w, *extra)


def _gelu_epilogue(j, p, extra_refs, o_ref, *, n_gelu_tiles):
    del extra_refs
    o_ref[...] = jnp.where(j < n_gelu_tiles, jax.nn.gelu(p), p)


def _rope_epilogue(j, p, extra_refs, o_ref, *, n_rope_tiles):
    cos_ref, sin_ref = extra_refs

    is_rope = j < n_rope_tiles
    cos = jnp.where(is_rope, cos_ref[...], 1.0)
    sin = jnp.where(is_rope, sin_ref[...], 0.0)
    for hh in range(p.shape[1] // HEAD_DIM):
        t = p[:, hh * HEAD_DIM:(hh + 1) * HEAD_DIM]
        o_ref[:, hh * HEAD_DIM:(hh + 1) * HEAD_DIM] = (
            t * cos + pltpu.roll(t, HEAD_DIM // 2, 1) * sin)


def _mixer_kernel(au_ref, av_ref, gb_ref, gc_ref, bx_ref, gch_ref, bxh_ref,
                  ws_ref, bias_ref, cw_ref, o_ref, z_ref, *, tiles_per_seq):
    tm = au_ref.shape[0]
    i = pl.program_id(0)

    av = av_ref[...]
    mu = jnp.mean(av, axis=-1, keepdims=True)
    cen = av - mu
    var = jnp.mean(cen * cen, axis=-1, keepdims=True)
    vn = (cen * lax.rsqrt(var + LN_EPS)).astype(BF16)
    row = lax.broadcasted_iota(jnp.int32, (CHUNK, CHUNK), 0)
    col = lax.broadcasted_iota(jnp.int32, (CHUNK, CHUNK), 1)
    causal = col <= row
    for g in range(A_GROUPS):
        cs = slice(g * A_GROUP_DIM, (g + 1) * A_GROUP_DIM)
        w_causal = jnp.where(causal, ws_ref[g], 0.0).astype(BF16)
        for c in range(tm // CHUNK):
            rs = slice(c * CHUNK, (c + 1) * CHUNK)
            mixed = jnp.dot(w_causal, vn[rs, cs], preferred_element_type=F32) + bias_ref[:, cs]
            o_ref[rs, cs] = (au_ref[rs, cs] * mixed).astype(BF16)

    z = gc_ref[...] * bx_ref[...]
    at_seq_start = (i % tiles_per_seq) == 0
    z_halo = jnp.where(at_seq_start, 0.0, gch_ref[...] * bxh_ref[...])
    z_ref[0:CONV_HALO, :] = z_halo
    z_ref[CONV_HALO:CONV_HALO + tm, :] = z
    y = (cw_ref[2:3, :] * z
         + cw_ref[1:2, :] * z_ref[CONV_HALO - 1:CONV_HALO - 1 + tm, :]
         + cw_ref[0:1, :] * z_ref[CONV_HALO - 2:CONV_HALO - 2 + tm, :])
    o_ref[:, A_WIDTH:] = (gb_ref[...] * y).astype(BF16)


def _mixer(proj, w_spatial, bias_full, conv_w, seq):
    m = proj.shape[0]
    tm = MIXER_TM
    assert seq % tm == 0 and tm % CHUNK == 0 and A_WIDTH == B_WIDTH
    tiles_per_seq = seq // tm
    halo_blocks = tm // CONV_HALO
    seg = lambda s: pl.BlockSpec((tm, A_WIDTH), lambda i, s=s: (i, s))
    halo = lambda s: pl.BlockSpec(
        (CONV_HALO, A_WIDTH), lambda i, s=s: (jnp.maximum(i * halo_blocks - 1, 0), s))
    kern = functools.partial(_mixer_kernel, tiles_per_seq=tiles_per_seq)
    return pl.pallas_call(
        kern,
        grid=(m // tm,),
        in_specs=[seg(0), seg(1), seg(2), seg(3), seg(4), halo(3), halo(4),
                  pl.BlockSpec((A_GROUPS, CHUNK, CHUNK), lambda i: (0, 0, 0)),
                  pl.BlockSpec((CHUNK, A_WIDTH), lambda i: (0, 0)),
                  pl.BlockSpec((CONV_WIDTH, B_WIDTH), lambda i: (0, 0))],
        out_specs=pl.BlockSpec((tm, D_MODEL), lambda i: (i, 0)),
        out_shape=jax.ShapeDtypeStruct((m, D_MODEL), BF16),
        scratch_shapes=[pltpu.VMEM((CONV_HALO + tm, B_WIDTH), F32)],
        compiler_params=_compiler_params(("parallel",)),
        name="mixer",
    )(proj, proj, proj, proj, proj, proj, proj, w_spatial, bias_full, conv_w)


def _attn_band_bias():
    blk = ATTN_BLOCK
    tables = []
    for _, d in DILATED_BRANCHES:
        pieces = ATTN_PERM // d
        rows = blk // pieces
        i = np.arange(blk)
        run = i // rows
        if pieces == ATTN_PERM:
            run = ATTN_PERM_STEP * (run % ATTN_PERM_STEP) + run // ATTN_PERM_STEP
        uq = pieces * (i % rows) + run
        uk = np.concatenate([uq, blk + uq])
        step = uq[:, None] + blk - uk[None, :]
        valid = (step >= 0) & (step <= blk)
        first = valid & (uk[None, :] >= blk)
        tables.append(np.stack([np.where(valid, 0.0, -np.inf), np.where(first, 0.0, -np.inf)]))
    return np.stack(tables).astype(np.float32)


def _attn_block(qb, k_prev, k_cur, v_prev, v_cur, bias, m_old, l_old, a_old):
    kw = jnp.concatenate([k_prev, k_cur], axis=0)
    vw = jnp.concatenate([v_prev, v_cur], axis=0)
    s = lax.dot_general(qb, kw, (((1,), (1,)), ((), ())), preferred_element_type=F32)
    s = s * (HEAD_DIM ** -0.5 * math.log2(math.e)) + bias
    m_blk = jnp.max(s, axis=-1, keepdims=True)
    m_new = jnp.maximum(m_old, m_blk)
    p = jnp.exp2(s - jnp.concatenate([m_new, m_new], axis=1))
    corr = jnp.exp2(m_old - m_new)
    l_new = l_old * corr + jnp.sum(p, axis=-1, keepdims=True)
    a_new = a_old * corr + jnp.dot(p.astype(BF16), vw, preferred_element_type=F32)
    return m_new, l_new, a_new


def _attn_kernel(q_ref, k_ref, v_ref, bias_ref, o_ref, qp_ref, kp_ref, vp_ref,
                 acc_ref, m_ref, l_ref, tmp_ref, *, seq, dilations):
    blk = ATTN_BLOCK
    lp = seq // ATTN_PERM
    step = ATTN_PERM_STEP
    lq = seq // step
    for src_ref, dst_ref in ((q_ref, qp_ref), (k_ref, kp_ref), (v_ref, vp_ref)):
        for lo in range(step):
            tmp_ref[lo * lq:(lo + 1) * lq, :] = src_ref[pl.ds(lo, lq, stride=step), :]
        for lo in range(step):
            for hi in range(step):
                run = lo * step + hi
                dst_ref[run * lp:(run + 1) * lp, :] = tmp_ref[pl.ds(lo * lq + hi, lp, stride=step), :]
    acc_ref[...] = jnp.zeros_like(acc_ref)
    l_ref[...] = jnp.zeros_like(l_ref)
    m_ref[...] = jnp.full_like(m_ref, -jnp.inf)

    for bi, d in enumerate(dilations):
        pieces = ATTN_PERM // d
        rows = blk // pieces
        nb = lp // rows
        gn = min(nb, ATTN_GROUP)
        gs = ATTN_GROUP // gn
        n_groups = nb // gn
        assert nb % gn == 0 and d % gs == 0

        def starts(r_sub, n, d=d, pieces=pieces, rows=rows):
            return [pl.multiple_of((r_sub * pieces + a) * lp + n * rows, 8) for a in range(pieces)]

        def gather(ref, st, rows=rows):
            parts = [ref[pl.ds(s0, rows), :] for s0 in st]
            return parts[0] if len(parts) == 1 else jnp.concatenate(parts, axis=0)

        def scatter(ref, st, val, rows=rows):
            for a, s0 in enumerate(st):
                ref[pl.ds(s0, rows), :] = val[a * rows:(a + 1) * rows]

        def body(it, carry, bi=bi, gn=gn, gs=gs, n_groups=n_groups):
            sg = it // n_groups
            n0 = (it % n_groups) * gn
            work = []
            for si in range(gs):
                r_sub = sg * gs + si
                st_prev = starts(r_sub, jnp.maximum(n0 - 1, 0))
                kb = [gather(kp_ref, st_prev).astype(BF16)]
                vb = [gather(vp_ref, st_prev).astype(BF16)]
                for j in range(gn):
                    st = starts(r_sub, n0 + j)
                    kb.append(gather(kp_ref, st).astype(BF16))
                    vb.append(gather(vp_ref, st).astype(BF16))
                    if j > 0:
                        bias = bias_ref[bi, 0]
                    elif n_groups == 1:
                        bias = bias_ref[bi, 1]
                    else:
                        bias = bias_ref[bi, jnp.where(n0 == 0, 1, 0)]
                    work.append((st, gather(qp_ref, st).astype(BF16), kb[j], kb[j + 1],
                                 vb[j], vb[j + 1], bias,
                                 gather(m_ref, st), gather(l_ref, st), gather(acc_ref, st)))
            results = [(w[0],) + _attn_block(*w[1:]) for w in work]
            for st, m_new, l_new, a_new in results:
                scatter(m_ref, st, m_new)
                scatter(l_ref, st, l_new)
                scatter(acc_ref, st, a_new)
            return carry

        lax.fori_loop(0, (d // gs) * n_groups, body, 0)

    for lo in range(step):
        for hi in range(step):
            src = slice((lo * step + hi) * lp, (lo * step + hi + 1) * lp)
            tmp_ref[pl.ds(lo * lq + hi, lp, stride=step), :] = acc_ref[src, :] / l_ref[src, :]
    for lo in range(step):
        o_ref[pl.ds(lo, lq, stride=step), :] = tmp_ref[lo * lq:(lo + 1) * lq, :]


def _attention(qkv, batch, seq):
    dilations = tuple(d for _, d in DILATED_BRANCHES)
    for window, d in DILATED_BRANCHES:
        assert window // d == ATTN_BLOCK and seq % (d * ATTN_BLOCK) == 0
        assert ATTN_PERM % d == 0 and ATTN_BLOCK % (ATTN_PERM // d) == 0
    qkv3 = qkv.reshape(batch, seq, 3 * D_MODEL)
    bias = jnp.asarray(_attn_band_bias())
    spec = lambda part: pl.BlockSpec((None, seq, HEAD_DIM),
                                     lambda b, h, part=part: (b, 0, part * N_HEADS + h))
    kern = functools.partial(_attn_kernel, seq=seq, dilations=dilations)
    out = pl.pallas_call(
        kern,
        grid=(batch, N_HEADS),
        in_specs=[spec(0), spec(1), spec(2),
                  pl.BlockSpec(bias.shape, lambda b, h: (0, 0, 0, 0))],
        out_specs=pl.BlockSpec((None, seq, HEAD_DIM), lambda b, h: (b, 0, h)),
        out_shape=jax.ShapeDtypeStruct((batch, seq, D_MODEL), F32),
        scratch_shapes=[pltpu.VMEM((seq, HEAD_DIM), F32)] * 7,
        compiler_params=_compiler_params(("parallel", "parallel")),
        name="attention",
    )(qkv3, qkv3, qkv3, bias)
    return out.reshape(batch * seq, D_MODEL)


def _matmul_rms_res_kernel(a_ref, w_ref, g_ref, h_ref, o_ref):
    f = jnp.dot(a_ref[...].astype(BF16), w_ref[...], preferred_element_type=F32)
    o_ref[...] = h_ref[...] + _rmsnorm(f, g_ref[...])


def _matmul_rms_res(a, w, g, h):
    m, k = a.shape
    n = w.shape[1]
    tm = RES_TM
    assert m % tm == 0
    return pl.pallas_call(
        _matmul_rms_res_kernel,
        grid=(m // tm,),
        in_specs=[pl.BlockSpec((tm, k), lambda i: (i, 0)),
                  pl.BlockSpec((k, n), lambda i: (0, 0)),
                  pl.BlockSpec((1, n), lambda i: (0, 0)),
                  pl.BlockSpec((tm, n), lambda i: (i, 0))],
        out_specs=pl.BlockSpec((tm, n), lambda i: (i, 0)),
        out_shape=jax.ShapeDtypeStruct((m, n), F32),
        compiler_params=_compiler_params(("parallel",)),
        name="matmul_rms_res",
    )(a, w, g, h)


def _mlp_kernel(x_ref, g_pre_ref, w_up_ref, w_down_ref, g_post_ref, o_ref, hn_ref):
    k = pl.program_id(1)

    @pl.when(k == 0)
    def _():
        hn_ref[...] = _rmsnorm(x_ref[...], g_pre_ref[...]).astype(BF16)
        o_ref[...] = jnp.zeros_like(o_ref)

    u = jnp.dot(hn_ref[...], w_up_ref[...], preferred_element_type=F32)
    u = jnp.square(jnp.maximum(u, 0.0)).astype(BF16)
    o_ref[...] += jnp.dot(u, w_down_ref[...], preferred_element_type=F32)

    @pl.when(k == pl.num_programs(1) - 1)
    def _():
        o_ref[...] = x_ref[...] + _rmsnorm(o_ref[...], g_post_ref[...])


def _mlp(h, g_pre, w_up, w_down, g_post):
    m, d = h.shape
    f = w_up.shape[1]
    tm, tf = MLP_TM, MLP_TF
    assert m % tm == 0 and f % tf == 0
    return pl.pallas_call(
        _mlp_kernel,
        grid=(m // tm, f // tf),
        in_specs=[pl.BlockSpec((tm, d), lambda i, k: (i, 0), pipeline_mode=pl.Buffered(1)),
                  pl.BlockSpec((1, d), lambda i, k: (0, 0)),
                  pl.BlockSpec((d, tf), lambda i, k: (0, k)),
                  pl.BlockSpec((tf, d), lambda i, k: (k, 0)),
                  pl.BlockSpec((1, d), lambda i, k: (0, 0))],
        out_specs=pl.BlockSpec((tm, d), lambda i, k: (i, 0)),
        out_shape=jax.ShapeDtypeStruct((m, d), F32),
        scratch_shapes=[pltpu.VMEM((tm, d), BF16)],
        compiler_params=_compiler_params(("parallel", "arbitrary")),
        name="mlp",
    )(h, g_pre, w_up, w_down, g_post)


def _rope_tables(seq):
    half = HEAD_DIM // 2
    inv_freq = ROPE_THETA ** (-jnp.arange(half, dtype=F32) * 2.0 / HEAD_DIM)
    ang = jnp.arange(seq, dtype=jnp.int32).astype(F32)[:, None] * inv_freq[None, :]
    cos = jnp.cos(ang)
    sin = jnp.sin(ang)
    return jnp.concatenate([cos, cos], axis=-1), jnp.concatenate([-sin, sin], axis=-1)


def kernel(x, norm_mix_pre, norm_mix_post, norm_mlp_pre, norm_mlp_post, w_in_ab, w_spatial,
           b_spatial, conv_w, w_out_ab, w_qkv, w_o, w_up, w_down):
    batch, seq, d = x.shape
    assert d == D_MODEL
    depth = norm_mix_pre.shape[0]
    m = batch * seq
    h = x.reshape(m, d)
    cos_tab, sin_tab = _rope_tables(seq)
    tiles_per_seq = seq // RMS_MATMUL_TM
    rope_specs = [pl.BlockSpec((RMS_MATMUL_TM, HEAD_DIM), lambda i, j: (i % tiles_per_seq, 0))] * 2
    gelu_ep = functools.partial(_gelu_epilogue, n_gelu_tiles=2 * A_WIDTH // RMS_MATMUL_TN)
    rope_ep = functools.partial(_rope_epilogue, n_rope_tiles=2 * D_MODEL // RMS_MATMUL_TN)

    for layer in range(depth):
        g_pre = norm_mix_pre[layer][None, :]
        g_post = norm_mix_post[layer][None, :]
        if layer % 2 == 0:
            e = layer // 2
            proj = _rms_matmul(h, g_pre, w_in_ab[e].astype(BF16), gelu_ep)
            bias_full = jnp.repeat(b_spatial[e].T, A_GROUP_DIM, axis=1)
            mixed = _mixer(proj, w_spatial[e], bias_full, conv_w[e], seq)
            h = _matmul_rms_res(mixed, w_out_ab[e].astype(BF16), g_post, h)
        else:
            o = layer // 2
            qkv = _rms_matmul(h, g_pre, w_qkv[o].astype(BF16), rope_ep,
                              extra=(cos_tab, sin_tab), extra_specs=rope_specs)
            att = _attention(qkv, batch, seq)
            h = _matmul_rms_res(att, w_o[o].astype(BF16), g_post, h)
        h = _mlp(h, norm_mlp_pre[layer][None, :], w_up[layer].astype(BF16),
                 w_down[layer].astype(BF16), norm_mlp_post[layer][None, :])
    return h.reshape(batch, seq, d)
```

```python
import functools
import math

import jax
import jax.numpy as jnp
import numpy as np
from jax import lax
from jax.experimental import pallas as pl
from jax.experimental.pallas import tpu as pltpu

F32 = jnp.float32
BF16 = jnp.bfloat16

D_MODEL = 2048
A_WIDTH = D_MODEL // 2
B_WIDTH = D_MODEL - A_WIDTH
A_GROUPS = 8
A_GROUP_DIM = A_WIDTH // A_GROUPS
CHUNK = 128
CONV_WIDTH = 3
HEAD_DIM = 128
N_HEADS = D_MODEL // HEAD_DIM
DILATED_BRANCHES = ((128, 1), (512, 4), (2048, 16))
ATTN_BLOCK = 128
ATTN_PERM = 16
ATTN_PERM_STEP = 4
ATTN_GROUP = 16
ROPE_THETA = 10000.0
FFN_DIM = 4 * D_MODEL
RMS_EPS = 1e-6
LN_EPS = 1e-5

V7X_LANES = 128
V7X_VMEM_LIMIT_BYTES = 56 * 1024 * 1024

RMS_MATMUL_TM = 1024
RMS_MATMUL_TN = 512
MIXER_TM = 256
RES_TM = 512
MLP_TM = 1024
MLP_TF = 512
CONV_HALO = 8


def _compiler_params(semantics):
    return pltpu.CompilerParams(dimension_semantics=semantics,
                                vmem_limit_bytes=V7X_VMEM_LIMIT_BYTES)


def _rmsnorm(x, g):
    ms = jnp.mean(x * x, axis=-1, keepdims=True)
    return (x * lax.rsqrt(ms + RMS_EPS)) * g


def _rms_matmul_kernel(x_ref, g_ref, w_ref, *rest, epilogue, n_special, n_extra):
    extra_refs = rest[:n_extra]
    o_ref = rest[n_extra]
    hn_ref = rest[n_extra + 1]
    j = pl.program_id(1)

    @pl.when(j == 0)
    def _():
        hn_ref[...] = _rmsnorm(x_ref[...], g_ref[...]).astype(BF16)

    def product():
        return jnp.dot(hn_ref[...], w_ref[...], preferred_element_type=F32)

    @pl.when(j < n_special)
    def _():
        epilogue(product(), extra_refs, o_ref)

    @pl.when(j >= n_special)
    def _():
        o_ref[...] = product()


def _rms_matmul(x, g, w, li, epilogue, special_cols, extra=(), extra_specs=(), out_dtype=F32):
    m, k = x.shape
    n = w.shape[2]
    tm, tn = RMS_MATMUL_TM, RMS_MATMUL_TN
    assert m % tm == 0 and n % tn == 0 and special_cols % tn == 0
    kern = functools.partial(_rms_matmul_kernel, epilogue=epilogue,
                             n_special=special_cols // tn, n_extra=len(extra))
    return pl.pallas_call(
        kern,
        grid=(m // tm, n // tn),
        in_specs=[pl.BlockSpec((tm, k), lambda i, j: (i, 0)),
                  pl.BlockSpec((1, k), lambda i, j: (0, 0)),
                  pl.BlockSpec((None, k, tn), lambda i, j: (li, 0, j)),
                  *extra_specs],
        out_specs=pl.BlockSpec((tm, tn), lambda i, j: (i, j)),
        out_shape=jax.ShapeDtypeStruct((m, n), out_dtype),
        scratch_shapes=[pltpu.VMEM((tm, k), BF16)],
        compiler_params=_compiler_params(("parallel", "arbitrary")),
        name="rms_matmul",
    )(x, g, w, *extra)


def _gelu_epilogue(p, extra_refs, o_ref):
    del extra_refs
    o_ref[...] = jax.nn.gelu(p)


def _rope_epilogue(p, extra_refs, o_ref):
    cos_ref, sin_ref = extra_refs
    cos = cos_ref[...]
    sin = sin_ref[...]
    for hh in range(p.shape[1] // HEAD_DIM):
        t = p[:, hh * HEAD_DIM:(hh + 1) * HEAD_DIM]
        o_ref[:, hh * HEAD_DIM:(hh + 1) * HEAD_DIM] = (
            t * cos + pltpu.roll(t, HEAD_DIM // 2, 1) * sin)


def _mixer_kernel(au_ref, av_ref, gb_ref, gc_ref, bx_ref, gch_ref, bxh_ref,
                  ws_ref, bias_ref, cw_ref, o_ref, z_ref, *, tiles_per_seq):
    tm = au_ref.shape[0]
    i = pl.program_id(0)

    av = av_ref[...]
    mu = jnp.mean(av, axis=-1, keepdims=True)
    cen = av - mu
    var = jnp.mean(cen * cen, axis=-1, keepdims=True)
    vn = (cen * lax.rsqrt(var + LN_EPS)).astype(BF16)
    row = lax.broadcasted_iota(jnp.int32, (CHUNK, CHUNK), 0)
    col = lax.broadcasted_iota(jnp.int32, (CHUNK, CHUNK), 1)
    causal = col <= row
    for g in range(A_GROUPS):
        cs = slice(g * A_GROUP_DIM, (g + 1) * A_GROUP_DIM)
        w_causal = jnp.where(causal, ws_ref[g], 0.0).astype(BF16)
        for c in range(tm // CHUNK):
            rs = slice(c * CHUNK, (c + 1) * CHUNK)
            mixed = jnp.dot(w_causal, vn[rs, cs], preferred_element_type=F32) + bias_ref[:, cs]
            o_ref[rs, cs] = (au_ref[rs, cs] * mixed).astype(BF16)

    z = gc_ref[...] * bx_ref[...]
    at_seq_start = (i % tiles_per_seq) == 0
    z_halo = jnp.where(at_seq_start, 0.0, gch_ref[...] * bxh_ref[...])
    z_ref[0:CONV_HALO, :] = z_halo
    z_ref[CONV_HALO:CONV_HALO + tm, :] = z
    y = (cw_ref[2:3, :] * z
         + cw_ref[1:2, :] * z_ref[CONV_HALO - 1:CONV_HALO - 1 + tm, :]
         + cw_ref[0:1, :] * z_ref[CONV_HALO - 2:CONV_HALO - 2 + tm, :])
    o_ref[:, A_WIDTH:] = (gb_ref[...] * y).astype(BF16)


def _mixer(proj, w_spatial, bias_full, conv_w, seq):
    m = proj.shape[0]
    tm = MIXER_TM
    assert seq % tm == 0 and tm % CHUNK == 0 and A_WIDTH == B_WIDTH
    tiles_per_seq = seq // tm
    halo_blocks = tm // CONV_HALO
    seg = lambda s: pl.BlockSpec((tm, A_WIDTH), lambda i, s=s: (i, s))
    halo = lambda s: pl.BlockSpec(
        (CONV_HALO, A_WIDTH), lambda i, s=s: (jnp.maximum(i * halo_blocks - 1, 0), s))
    kern = functools.partial(_mixer_kernel, tiles_per_seq=tiles_per_seq)
    return pl.pallas_call(
        kern,
        grid=(m // tm,),
        in_specs=[seg(0), seg(1), seg(2), seg(3), seg(4), halo(3), halo(4),
                  pl.BlockSpec((A_GROUPS, CHUNK, CHUNK), lambda i: (0, 0, 0)),
                  pl.BlockSpec((CHUNK, A_WIDTH), lambda i: (0, 0)),
                  pl.BlockSpec((CONV_WIDTH, B_WIDTH), lambda i: (0, 0))],
        out_specs=pl.BlockSpec((tm, D_MODEL), lambda i: (i, 0)),
        out_shape=jax.ShapeDtypeStruct((m, D_MODEL), BF16),
        scratch_shapes=[pltpu.VMEM((CONV_HALO + tm, B_WIDTH), F32)],
        compiler_params=_compiler_params(("parallel",)),
        name="mixer",
    )(proj, proj, proj, proj, proj, proj, proj, w_spatial, bias_full, conv_w)


def _attn_band_bias():
    blk = ATTN_BLOCK
    tables = []
    for _, d in DILATED_BRANCHES:
        pieces = ATTN_PERM // d
        rows = blk // pieces
        i = np.arange(blk)
        run = i // rows
        if pieces == ATTN_PERM:
            run = ATTN_PERM_STEP * (run % ATTN_PERM_STEP) + run // ATTN_PERM_STEP
        uq = pieces * (i % rows) + run
        uk = np.concatenate([uq, blk + uq])
        step = uq[:, None] + blk - uk[None, :]
        valid = (step >= 0) & (step <= blk)
        first = valid & (uk[None, :] >= blk)
        tables.append(np.stack([np.where(valid, 0.0, -np.inf), np.where(first, 0.0, -np.inf)]))
    return np.stack(tables).astype(np.float32)


def _attn_block(qb, k_prev, k_cur, v_prev, v_cur, bias, m_old, l_old, a_old):
    kw = jnp.concatenate([k_prev, k_cur], axis=0)
    vw = jnp.concatenate([v_prev, v_cur], axis=0)
    s = lax.dot_general(qb, kw, (((1,), (1,)), ((), ())), preferred_element_type=F32)
    s = s * (HEAD_DIM ** -0.5 * math.log2(math.e)) + bias
    m_blk = jnp.max(s, axis=-1, keepdims=True)
    m_new = jnp.maximum(m_old, m_blk)
    p = jnp.exp2(s - jnp.concatenate([m_new, m_new], axis=1))
    corr = jnp.exp2(m_old - m_new)
    l_new = l_old * corr + jnp.sum(p, axis=-1, keepdims=True)
    a_new = a_old * corr + jnp.dot(p.astype(BF16), vw, preferred_element_type=F32)
    return m_new, l_new, a_new


def _attn_kernel(q_ref, k_ref, v_ref, bias_ref, o_ref, qp_ref, kp_ref, vp_ref,
                 acc_ref, m_ref, l_ref, tmp_ref, *, seq, dilations):
    blk = ATTN_BLOCK
    lp = seq // ATTN_PERM
    step = ATTN_PERM_STEP
    lq = seq // step
    for src_ref, dst_ref in ((q_ref, qp_ref), (k_ref, kp_ref), (v_ref, vp_ref)):
        for lo in range(step):
            tmp_ref[lo * lq:(lo + 1) * lq, :] = src_ref[pl.ds(lo, lq, stride=step), :]
        for lo in range(step):
            for hi in range(step):
                run = lo * step + hi
                dst_ref[run * lp:(run + 1) * lp, :] = tmp_ref[pl.ds(lo * lq + hi, lp, stride=step), :]
    acc_ref[...] = jnp.zeros_like(acc_ref)
    l_ref[...] = jnp.zeros_like(l_ref)
    m_ref[...] = jnp.full_like(m_ref, -jnp.inf)

    for bi, d in enumerate(dilations):
        pieces = ATTN_PERM // d
        rows = blk // pieces
        nb = lp // rows
        gn = min(nb, ATTN_GROUP)
        gs = ATTN_GROUP // gn
        n_groups = nb // gn
        assert nb % gn == 0 and d % gs == 0

        def starts(r_sub, n, d=d, pieces=pieces, rows=rows):
            return [pl.multiple_of((r_sub * pieces + a) * lp + n * rows, 8) for a in range(pieces)]

        def gather(ref, st, rows=rows):
            parts = [ref[pl.ds(s0, rows), :] for s0 in st]
            return parts[0] if len(parts) == 1 else jnp.concatenate(parts, axis=0)

        def scatter(ref, st, val, rows=rows):
            for a, s0 in enumerate(st):
                ref[pl.ds(s0, rows), :] = val[a * rows:(a + 1) * rows]

        def body(it, carry, bi=bi, gn=gn, gs=gs, n_groups=n_groups):
            sg = it // n_groups
            n0 = (it % n_groups) * gn
            work = []
            for si in range(gs):
                r_sub = sg * gs + si
                st_prev = starts(r_sub, jnp.maximum(n0 - 1, 0))
                kb = [gather(kp_ref, st_prev).astype(BF16)]
                vb = [gather(vp_ref, st_prev).astype(BF16)]
                for j in range(gn):
                    st = starts(r_sub, n0 + j)
                    kb.append(gather(kp_ref, st).astype(BF16))
                    vb.append(gather(vp_ref, st).astype(BF16))
                    if j > 0:
                        bias = bias_ref[bi, 0]
                    elif n_groups == 1:
                        bias = bias_ref[bi, 1]
                    else:
                        bias = bias_ref[bi, jnp.where(n0 == 0, 1, 0)]
                    work.append((st, gather(qp_ref, st).astype(BF16), kb[j], kb[j + 1],
                                 vb[j], vb[j + 1], bias,
                                 gather(m_ref, st), gather(l_ref, st), gather(acc_ref, st)))
            results = [(w[0],) + _attn_block(*w[1:]) for w in work]
            for st, m_new, l_new, a_new in results:
                scatter(m_ref, st, m_new)
                scatter(l_ref, st, l_new)
                scatter(acc_ref, st, a_new)
            return carry

        lax.fori_loop(0, (d // gs) * n_groups, body, 0)

    for lo in range(step):
        for hi in range(step):
            src = slice((lo * step + hi) * lp, (lo * step + hi + 1) * lp)
            tmp_ref[pl.ds(lo * lq + hi, lp, stride=step), :] = acc_ref[src, :] / l_ref[src, :]
    for lo in range(step):
        o_ref[pl.ds(lo, lq, stride=step), :] = tmp_ref[lo * lq:(lo + 1) * lq, :]


def _attention(qkv, batch, seq):
    dilations = tuple(d for _, d in DILATED_BRANCHES)
    for window, d in DILATED_BRANCHES:
        assert window // d == ATTN_BLOCK and seq % (d * ATTN_BLOCK) == 0
        assert ATTN_PERM % d == 0 and ATTN_BLOCK % (ATTN_PERM // d) == 0
    qkv3 = qkv.reshape(batch, seq, 3 * D_MODEL)
    bias = jnp.asarray(_attn_band_bias())
    spec = lambda part: pl.BlockSpec((None, seq, HEAD_DIM),
                                     lambda b, h, part=part: (b, 0, part * N_HEADS + h))
    kern = functools.partial(_attn_kernel, seq=seq, dilations=dilations)
    out = pl.pallas_call(
        kern,
        grid=(batch, N_HEADS),
        in_specs=[spec(0), spec(1), spec(2),
                  pl.BlockSpec(bias.shape, lambda b, h: (0, 0, 0, 0))],
        out_specs=pl.BlockSpec((None, seq, HEAD_DIM), lambda b, h: (b, 0, h)),
        out_shape=jax.ShapeDtypeStruct((batch, seq, D_MODEL), F32),
        scratch_shapes=[pltpu.VMEM((seq, HEAD_DIM), F32)] * 7,
        compiler_params=_compiler_params(("parallel", "parallel")),
        name="attention",
    )(qkv3, qkv3, qkv3, bias)
    return out.reshape(batch * seq, D_MODEL)


def _matmul_rms_res_kernel(a_ref, w_ref, g_ref, h_ref, o_ref):
    f = jnp.dot(a_ref[...].astype(BF16), w_ref[...], preferred_element_type=F32)
    o_ref[...] = h_ref[...] + _rmsnorm(f, g_ref[...])


def _matmul_rms_res(a, w, li, g, h):
    m, k = a.shape
    n = w.shape[2]
    tm = RES_TM
    assert m % tm == 0
    return pl.pallas_call(
        _matmul_rms_res_kernel,
        grid=(m // tm,),
        in_specs=[pl.BlockSpec((tm, k), lambda i: (i, 0)),
                  pl.BlockSpec((None, k, n), lambda i: (li, 0, 0)),
                  pl.BlockSpec((1, n), lambda i: (0, 0)),
                  pl.BlockSpec((tm, n), lambda i: (i, 0))],
        out_specs=pl.BlockSpec((tm, n), lambda i: (i, 0)),
        out_shape=jax.ShapeDtypeStruct((m, n), F32),
        compiler_params=_compiler_params(("parallel",)),
        name="matmul_rms_res",
    )(a, w, g, h)


def _mlp_kernel(x_ref, g_pre_ref, w_up_ref, w_down_ref, g_post_ref, o_ref, hn_ref):
    k = pl.program_id(1)

    @pl.when(k == 0)
    def _():
        hn_ref[...] = _rmsnorm(x_ref[...], g_pre_ref[...]).astype(BF16)
        o_ref[...] = jnp.zeros_like(o_ref)

    u = jnp.dot(hn_ref[...], w_up_ref[...], preferred_element_type=F32)
    u = jnp.square(jnp.maximum(u, 0.0)).astype(BF16)
    o_ref[...] += jnp.dot(u, w_down_ref[...], preferred_element_type=F32)

    @pl.when(k == pl.num_programs(1) - 1)
    def _():
        o_ref[...] = x_ref[...] + _rmsnorm(o_ref[...], g_post_ref[...])


def _mlp(h, g_pre, w_up, w_down, li, g_post):
    m, d = h.shape
    f = w_up.shape[2]
    tm, tf = MLP_TM, MLP_TF
    assert m % tm == 0 and f % tf == 0
    return pl.pallas_call(
        _mlp_kernel,
        grid=(m // tm, f // tf),
        in_specs=[pl.BlockSpec((tm, d), lambda i, k: (i, 0), pipeline_mode=pl.Buffered(1)),
                  pl.BlockSpec((1, d), lambda i, k: (0, 0)),
                  pl.BlockSpec((None, d, tf), lambda i, k: (li, 0, k)),
                  pl.BlockSpec((None, tf, d), lambda i, k: (li, k, 0)),
                  pl.BlockSpec((1, d), lambda i, k: (0, 0))],
        out_specs=pl.BlockSpec((tm, d), lambda i, k: (i, 0)),
        out_shape=jax.ShapeDtypeStruct((m, d), F32),
        scratch_shapes=[pltpu.VMEM((tm, d), BF16)],
        compiler_params=_compiler_params(("parallel", "arbitrary")),
        name="mlp",
    )(h, g_pre, w_up, w_down, g_post)


def _rope_tables(seq):
    half = HEAD_DIM // 2
    inv_freq = ROPE_THETA ** (-jnp.arange(half, dtype=F32) * 2.0 / HEAD_DIM)
    ang = jnp.arange(seq, dtype=jnp.int32).astype(F32)[:, None] * inv_freq[None, :]
    cos = jnp.cos(ang)
    sin = jnp.sin(ang)
    return jnp.concatenate([cos, cos], axis=-1), jnp.concatenate([-sin, sin], axis=-1)


def kernel(x, norm_mix_pre, norm_mix_post, norm_mlp_pre, norm_mlp_post, w_in_ab, w_spatial,
           b_spatial, conv_w, w_out_ab, w_qkv, w_o, w_up, w_down):
    batch, seq, d = x.shape
    assert d == D_MODEL
    depth = norm_mix_pre.shape[0]
    m = batch * seq
    h = x.reshape(m, d)
    cos_tab, sin_tab = _rope_tables(seq)
    tiles_per_seq = seq // RMS_MATMUL_TM
    rope_specs = [pl.BlockSpec((RMS_MATMUL_TM, HEAD_DIM), lambda i, j: (i % tiles_per_seq, 0))] * 2

    w_in_bf, w_out_bf, w_qkv_bf, w_o_bf, w_up_bf, w_down_bf = (
        w.astype(BF16) for w in (w_in_ab, w_out_ab, w_qkv, w_o, w_up, w_down))

    for layer in range(depth):
        g_pre = norm_mix_pre[layer][None, :]
        g_post = norm_mix_post[layer][None, :]
        if layer % 2 == 0:
            e = layer // 2
            proj = _rms_matmul(h, g_pre, w_in_bf, e, _gelu_epilogue, 2 * A_WIDTH)
            bias_full = jnp.repeat(b_spatial[e].T, A_GROUP_DIM, axis=1)
            mixed = _mixer(proj, w_spatial[e], bias_full, conv_w[e], seq)
            h = _matmul_rms_res(mixed, w_out_bf, e, g_post, h)
        else:
            o = layer // 2
            qkv = _rms_matmul(h, g_pre, w_qkv_bf, o, _rope_epilogue, 2 * D_MODEL,
                              extra=(cos_tab, sin_tab), extra_specs=rope_specs)
            att = _attention(qkv, batch, seq)
            h = _matmul_rms_res(att, w_o_bf, o, g_post, h)
        h = _mlp(h, norm_mlp_pre[layer][None, :], w_up_bf, w_down_bf, layer,
                 norm_mlp_post[layer][None, :])
    return h.reshape(batch, seq, d)
```

```python
import functools
import math

import jax
import jax.numpy as jnp
import numpy as np
from jax import lax
from jax.experimental import pallas as pl
from jax.experimental.pallas import tpu as pltpu

F32 = jnp.float32
BF16 = jnp.bfloat16

D_MODEL = 2048
A_WIDTH = D_MODEL // 2
B_WIDTH = D_MODEL - A_WIDTH
A_GROUPS = 8
A_GROUP_DIM = A_WIDTH // A_GROUPS
CHUNK = 128
CONV_WIDTH = 3
HEAD_DIM = 128
N_HEADS = D_MODEL // HEAD_DIM
DILATED_BRANCHES = ((128, 1), (512, 4), (2048, 16))
ATTN_BLOCK = 128
ATTN_PERM = 16
ATTN_PERM_STEP = 4
ATTN_GROUP = 32
ROPE_THETA = 10000.0
FFN_DIM = 4 * D_MODEL
RMS_EPS = 1e-6
LN_EPS = 1e-5

V7X_LANES = 128
BF16_SUBLANES = 16
V7X_VMEM_LIMIT_BYTES = 56 * 1024 * 1024

RMS_MATMUL_TM = 1024
RMS_MATMUL_TN = 512
MIXER_TM = 256
RES_TM = 512
MLP_TM = 1024
MLP_TF = 512
CONV_HALO = 8


def _compiler_params(semantics):
    return pltpu.CompilerParams(dimension_semantics=semantics,
                                vmem_limit_bytes=V7X_VMEM_LIMIT_BYTES)


def _rmsnorm(x, g):
    ms = jnp.mean(x * x, axis=-1, keepdims=True)
    return (x * lax.rsqrt(ms + RMS_EPS)) * g


def _rms_matmul_kernel(x_ref, g_ref, w_ref, *rest, epilogue, n_special, n_extra):
    extra_refs = rest[:n_extra]
    o_ref = rest[n_extra]
    hn_ref = rest[n_extra + 1]
    j = pl.program_id(1)

    @pl.when(j == 0)
    def _():
        hn_ref[...] = _rmsnorm(x_ref[...], g_ref[...]).astype(BF16)

    def product():
        return jnp.dot(hn_ref[...], w_ref[...], preferred_element_type=F32)

    @pl.when(j < n_special)
    def _():
        epilogue(product(), extra_refs, o_ref)

    @pl.when(j >= n_special)
    def _():
        o_ref[...] = product()


def _rms_matmul(x, g, w, li, epilogue, special_cols, extra=(), extra_specs=(), out_dtype=F32):
    m, k = x.shape
    n = w.shape[2]
    tm, tn = RMS_MATMUL_TM, RMS_MATMUL_TN
    assert m % tm == 0 and n % tn == 0 and special_cols % tn == 0
    kern = functools.partial(_rms_matmul_kernel, epilogue=epilogue,
                             n_special=special_cols // tn, n_extra=len(extra))
    return pl.pallas_call(
        kern,
        grid=(m // tm, n // tn),
        in_specs=[pl.BlockSpec((tm, k), lambda i, j: (i, 0)),
                  pl.BlockSpec((1, k), lambda i, j: (0, 0)),
                  pl.BlockSpec((None, k, tn), lambda i, j: (li, 0, j)),
                  *extra_specs],
        out_specs=pl.BlockSpec((tm, tn), lambda i, j: (i, j)),
        out_shape=jax.ShapeDtypeStruct((m, n), out_dtype),
        scratch_shapes=[pltpu.VMEM((tm, k), BF16)],
        compiler_params=_compiler_params(("parallel", "arbitrary")),
        name="rms_matmul",
    )(x, g, w, *extra)


def _gelu_epilogue(p, extra_refs, o_ref):
    del extra_refs
    o_ref[...] = jax.nn.gelu(p)


def _rope_epilogue(p, extra_refs, o_ref):
    cos_ref, sin_ref = extra_refs
    cos = cos_ref[...]
    sin = sin_ref[...]
    for hh in range(p.shape[1] // HEAD_DIM):
        t = p[:, hh * HEAD_DIM:(hh + 1) * HEAD_DIM]
        o_ref[:, hh * HEAD_DIM:(hh + 1) * HEAD_DIM] = (
            t * cos + pltpu.roll(t, HEAD_DIM // 2, 1) * sin)


def _mixer_kernel(au_ref, av_ref, gb_ref, gc_ref, bx_ref, gch_ref, bxh_ref,
                  ws_ref, bias_ref, cw_ref, o_ref, z_ref, *, tiles_per_seq):
    tm = au_ref.shape[0]
    i = pl.program_id(0)

    av = av_ref[...]
    mu = jnp.mean(av, axis=-1, keepdims=True)
    cen = av - mu
    var = jnp.mean(cen * cen, axis=-1, keepdims=True)
    vn = (cen * lax.rsqrt(var + LN_EPS)).astype(BF16)
    row = lax.broadcasted_iota(jnp.int32, (CHUNK, CHUNK), 0)
    col = lax.broadcasted_iota(jnp.int32, (CHUNK, CHUNK), 1)
    causal = col <= row
    for g in range(A_GROUPS):
        cs = slice(g * A_GROUP_DIM, (g + 1) * A_GROUP_DIM)
        w_causal = jnp.where(causal, ws_ref[g], 0.0).astype(BF16)
        for c in range(tm // CHUNK):
            rs = slice(c * CHUNK, (c + 1) * CHUNK)
            mixed = jnp.dot(w_causal, vn[rs, cs], preferred_element_type=F32) + bias_ref[:, cs]
            o_ref[rs, cs] = (au_ref[rs, cs] * mixed).astype(BF16)

    z = gc_ref[...] * bx_ref[...]
    at_seq_start = (i % tiles_per_seq) == 0
    z_halo = jnp.where(at_seq_start, 0.0, gch_ref[...] * bxh_ref[...])
    z_ref[0:CONV_HALO, :] = z_halo
    z_ref[CONV_HALO:CONV_HALO + tm, :] = z
    y = (cw_ref[2:3, :] * z
         + cw_ref[1:2, :] * z_ref[CONV_HALO - 1:CONV_HALO - 1 + tm, :]
         + cw_ref[0:1, :] * z_ref[CONV_HALO - 2:CONV_HALO - 2 + tm, :])
    o_ref[:, A_WIDTH:] = (gb_ref[...] * y).astype(BF16)


def _mixer(proj, w_spatial, bias_full, conv_w, seq):
    m = proj.shape[0]
    tm = MIXER_TM
    assert seq % tm == 0 and tm % CHUNK == 0 and A_WIDTH == B_WIDTH
    tiles_per_seq = seq // tm
    halo_blocks = tm // CONV_HALO
    seg = lambda s: pl.BlockSpec((tm, A_WIDTH), lambda i, s=s: (i, s))
    halo = lambda s: pl.BlockSpec(
        (CONV_HALO, A_WIDTH), lambda i, s=s: (jnp.maximum(i * halo_blocks - 1, 0), s))
    kern = functools.partial(_mixer_kernel, tiles_per_seq=tiles_per_seq)
    return pl.pallas_call(
        kern,
        grid=(m // tm,),
        in_specs=[seg(0), seg(1), seg(2), seg(3), seg(4), halo(3), halo(4),
                  pl.BlockSpec((A_GROUPS, CHUNK, CHUNK), lambda i: (0, 0, 0)),
                  pl.BlockSpec((CHUNK, A_WIDTH), lambda i: (0, 0)),
                  pl.BlockSpec((CONV_WIDTH, B_WIDTH), lambda i: (0, 0))],
        out_specs=pl.BlockSpec((tm, D_MODEL), lambda i: (i, 0)),
        out_shape=jax.ShapeDtypeStruct((m, D_MODEL), BF16),
        scratch_shapes=[pltpu.VMEM((CONV_HALO + tm, B_WIDTH), F32)],
        compiler_params=_compiler_params(("parallel",)),
        name="mixer",
    )(proj, proj, proj, proj, proj, proj, proj, w_spatial, bias_full, conv_w)


def _attn_band_bias():
    blk = ATTN_BLOCK
    tables = []
    for _, d in DILATED_BRANCHES:
        pieces = ATTN_PERM // d
        rows = blk // pieces
        i = np.arange(blk)
        run = i // rows
        if pieces == ATTN_PERM:
            run = ATTN_PERM_STEP * (run % ATTN_PERM_STEP) + run // ATTN_PERM_STEP
        uq = pieces * (i % rows) + run
        uk = np.concatenate([uq, blk + uq])
        step = uq[:, None] + blk - uk[None, :]
        valid = (step >= 0) & (step <= blk)
        first = valid & (uk[None, :] >= blk)
        tables.append(np.stack([np.where(valid, 0.0, -np.inf), np.where(first, 0.0, -np.inf)]))
    return np.stack(tables).astype(np.float32)


def _attn_block(qb, k_prev, k_cur, v_prev, v_cur, bias, old):
    kw = jnp.concatenate([k_prev, k_cur], axis=0)
    vw = jnp.concatenate([v_prev, v_cur], axis=0)
    s = lax.dot_general(qb, kw, (((1,), (1,)), ((), ())), preferred_element_type=F32) + bias
    m_blk = jnp.max(s, axis=-1, keepdims=True)
    m_new = jnp.broadcast_to(m_blk, (s.shape[0], HEAD_DIM))
    if old is not None:
        m_new = jnp.maximum(old[0], m_new)
    p = jnp.exp2(s - jnp.concatenate([m_new, m_new], axis=1)).astype(BF16)
    v_ones = jnp.concatenate([vw, jnp.ones_like(vw)], axis=1)
    pv = jnp.dot(p, v_ones, preferred_element_type=F32)
    a_new, l_new = pv[:, :HEAD_DIM], pv[:, HEAD_DIM:]
    if old is not None:
        corr = jnp.exp2(old[0] - m_new)
        l_new = old[1] * corr + l_new
        a_new = old[2] * corr + a_new
    return m_new, l_new, a_new


def _attn_kernel(q_ref, k_ref, v_ref, bias_ref, o_ref, qp_ref, kp_ref, vp_ref,
                 qh_ref, kh_ref, vh_ref, acc_ref, m_ref, l_ref, tmp_ref, *, seq, dilations):
    blk = ATTN_BLOCK
    lp = seq // ATTN_PERM
    step = ATTN_PERM_STEP
    lq = seq // step
    q_scale = HEAD_DIM ** -0.5 * math.log2(math.e)
    for src_ref, dst_ref, half_ref, scale in ((q_ref, qp_ref, qh_ref, q_scale),
                                              (k_ref, kp_ref, kh_ref, None),
                                              (v_ref, vp_ref, vh_ref, None)):
        for lo in range(step):
            tmp_ref[lo * lq:(lo + 1) * lq, :] = src_ref[pl.ds(lo, lq, stride=step), :]
        for lo in range(step):
            for hi in range(step):
                run = lo * step + hi
                rows = tmp_ref[pl.ds(lo * lq + hi, lp, stride=step), :]
                if scale is not None:
                    rows = rows * scale
                dst_ref[run * lp:(run + 1) * lp, :] = rows
                half_ref[run * lp:(run + 1) * lp, :] = rows.astype(BF16)

    for bi, d in enumerate(dilations):
        pieces = ATTN_PERM // d
        rows = blk // pieces
        nb = lp // rows
        gn = min(nb, ATTN_GROUP)
        gs = ATTN_GROUP // gn
        n_groups = nb // gn
        assert nb % gn == 0 and d % gs == 0
        packed = rows % BF16_SUBLANES == 0
        q_src, k_src, v_src = (qh_ref, kh_ref, vh_ref) if packed else (qp_ref, kp_ref, vp_ref)

        def starts(r_sub, n, d=d, pieces=pieces, rows=rows):
            return [pl.multiple_of((r_sub * pieces + a) * lp + n * rows, rows) for a in range(pieces)]

        def gather(ref, st, rows=rows):
            parts = [ref[pl.ds(s0, rows), :] for s0 in st]
            return parts[0] if len(parts) == 1 else jnp.concatenate(parts, axis=0)

        def gather_half(ref, st):
            return gather(ref, st).astype(BF16)

        def scatter(ref, st, val, rows=rows):
            for a, s0 in enumerate(st):
                ref[pl.ds(s0, rows), :] = val[a * rows:(a + 1) * rows]

        def body(it, carry, bi=bi, gn=gn, gs=gs, n_groups=n_groups,
                 q_src=q_src, k_src=k_src, v_src=v_src):
            sg = it // n_groups
            n0 = (it % n_groups) * gn
            work = []
            for si in range(gs):
                r_sub = sg * gs + si
                st_prev = starts(r_sub, jnp.maximum(n0 - 1, 0))
                kb = [gather_half(k_src, st_prev)]
                vb = [gather_half(v_src, st_prev)]
                for j in range(gn):
                    st = starts(r_sub, n0 + j)
                    kb.append(gather_half(k_src, st))
                    vb.append(gather_half(v_src, st))
                    if j > 0:
                        bias = bias_ref[bi, 0]
                    elif n_groups == 1:
                        bias = bias_ref[bi, 1]
                    else:
                        bias = bias_ref[bi, jnp.where(n0 == 0, 1, 0)]
                    old = None if bi == 0 else (gather(m_ref, st), gather(l_ref, st),
                                                gather(acc_ref, st))
                    work.append((st, gather_half(q_src, st), kb[j], kb[j + 1],
                                 vb[j], vb[j + 1], bias, old))
            results = [(w[0],) + _attn_block(*w[1:]) for w in work]
            for st, m_new, l_new, a_new in results:
                scatter(m_ref, st, m_new)
                scatter(l_ref, st, l_new)
                scatter(acc_ref, st, a_new)
            return carry

        lax.fori_loop(0, (d // gs) * n_groups, body, 0)

    for lo in range(step):
        for hi in range(step):
            src = slice((lo * step + hi) * lp, (lo * step + hi + 1) * lp)
            tmp_ref[pl.ds(lo * lq + hi, lp, stride=step), :] = acc_ref[src, :] / l_ref[src, :]
    for lo in range(step):
        o_ref[pl.ds(lo, lq, stride=step), :] = tmp_ref[lo * lq:(lo + 1) * lq, :]


def _attention(qkv, batch, seq):
    dilations = tuple(d for _, d in DILATED_BRANCHES)
    for window, d in DILATED_BRANCHES:
        assert window // d == ATTN_BLOCK and seq % (d * ATTN_BLOCK) == 0
        assert ATTN_PERM % d == 0 and ATTN_BLOCK % (ATTN_PERM // d) == 0
    qkv3 = qkv.reshape(batch, seq, 3 * D_MODEL)
    bias = jnp.asarray(_attn_band_bias())
    spec = lambda part: pl.BlockSpec((None, seq, HEAD_DIM),
                                     lambda b, h, part=part: (b, 0, part * N_HEADS + h))
    kern = functools.partial(_attn_kernel, seq=seq, dilations=dilations)
    out = pl.pallas_call(
        kern,
        grid=(batch, N_HEADS),
        in_specs=[spec(0), spec(1), spec(2),
                  pl.BlockSpec(bias.shape, lambda b, h: (0, 0, 0, 0))],
        out_specs=pl.BlockSpec((None, seq, HEAD_DIM), lambda b, h: (b, 0, h)),
        out_shape=jax.ShapeDtypeStruct((batch, seq, D_MODEL), F32),
        scratch_shapes=[pltpu.VMEM((seq, HEAD_DIM), F32)] * 3 + [pltpu.VMEM((seq, HEAD_DIM), BF16)] * 3
                       + [pltpu.VMEM((seq, HEAD_DIM), F32)] * 4,
        compiler_params=_compiler_params(("parallel", "parallel")),
        name="attention",
    )(qkv3, qkv3, qkv3, bias)
    return out.reshape(batch * seq, D_MODEL)


def _matmul_rms_res_kernel(a_ref, w_ref, g_ref, h_ref, o_ref):
    f = jnp.dot(a_ref[...].astype(BF16), w_ref[...], preferred_element_type=F32)
    o_ref[...] = h_ref[...] + _rmsnorm(f, g_ref[...])


def _matmul_rms_res(a, w, li, g, h):
    m, k = a.shape
    n = w.shape[2]
    tm = RES_TM
    assert m % tm == 0
    return pl.pallas_call(
        _matmul_rms_res_kernel,
        grid=(m // tm,),
        in_specs=[pl.BlockSpec((tm, k), lambda i: (i, 0)),
                  pl.BlockSpec((None, k, n), lambda i: (li, 0, 0)),
                  pl.BlockSpec((1, n), lambda i: (0, 0)),
                  pl.BlockSpec((tm, n), lambda i: (i, 0))],
        out_specs=pl.BlockSpec((tm, n), lambda i: (i, 0)),
        out_shape=jax.ShapeDtypeStruct((m, n), F32),
        compiler_params=_compiler_params(("parallel",)),
        name="matmul_rms_res",
    )(a, w, g, h)


def _mlp_kernel(x_ref, g_pre_ref, w_up_ref, w_down_ref, g_post_ref, o_ref, hn_ref):
    k = pl.program_id(1)

    @pl.when(k == 0)
    def _():
        hn_ref[...] = _rmsnorm(x_ref[...], g_pre_ref[...]).astype(BF16)
        o_ref[...] = jnp.zeros_like(o_ref)

    u = jnp.dot(hn_ref[...], w_up_ref[...], preferred_element_type=F32)
    u = jnp.square(jnp.maximum(u, 0.0)).astype(BF16)
    o_ref[...] += jnp.dot(u, w_down_ref[...], preferred_element_type=F32)

    @pl.when(k == pl.num_programs(1) - 1)
    def _():
        o_ref[...] = x_ref[...] + _rmsnorm(o_ref[...], g_post_ref[...])


def _mlp(h, g_pre, w_up, w_down, li, g_post):
    m, d = h.shape
    f = w_up.shape[2]
    tm, tf = MLP_TM, MLP_TF
    assert m % tm == 0 and f % tf == 0
    return pl.pallas_call(
        _mlp_kernel,
        grid=(m // tm, f // tf),
        in_specs=[pl.BlockSpec((tm, d), lambda i, k: (i, 0), pipeline_mode=pl.Buffered(1)),
                  pl.BlockSpec((1, d), lambda i, k: (0, 0)),
                  pl.BlockSpec((None, d, tf), lambda i, k: (li, 0, k)),
                  pl.BlockSpec((None, tf, d), lambda i, k: (li, k, 0)),
                  pl.BlockSpec((1, d), lambda i, k: (0, 0))],
        out_specs=pl.BlockSpec((tm, d), lambda i, k: (i, 0)),
        out_shape=jax.ShapeDtypeStruct((m, d), F32),
        scratch_shapes=[pltpu.VMEM((tm, d), BF16)],
        compiler_params=_compiler_params(("parallel", "arbitrary")),
        name="mlp",
    )(h, g_pre, w_up, w_down, g_post)


def _rope_tables(seq):
    half = HEAD_DIM // 2
    inv_freq = ROPE_THETA ** (-jnp.arange(half, dtype=F32) * 2.0 / HEAD_DIM)
    ang = jnp.arange(seq, dtype=jnp.int32).astype(F32)[:, None] * inv_freq[None, :]
    cos = jnp.cos(ang)
    sin = jnp.sin(ang)
    return jnp.concatenate([cos, cos], axis=-1), jnp.concatenate([-sin, sin], axis=-1)


def kernel(x, norm_mix_pre, norm_mix_post, norm_mlp_pre, norm_mlp_post, w_in_ab, w_spatial,
           b_spatial, conv_w, w_out_ab, w_qkv, w_o, w_up, w_down):
    batch, seq, d = x.shape
    assert d == D_MODEL
    depth = norm_mix_pre.shape[0]
    m = batch * seq
    h = x.reshape(m, d)
    cos_tab, sin_tab = _rope_tables(seq)
    tiles_per_seq = seq // RMS_MATMUL_TM
    rope_specs = [pl.BlockSpec((RMS_MATMUL_TM, HEAD_DIM), lambda i, j: (i % tiles_per_seq, 0))] * 2

    w_in_bf, w_out_bf, w_qkv_bf, w_o_bf, w_up_bf, w_down_bf = (
        w.astype(BF16) for w in (w_in_ab, w_out_ab, w_qkv, w_o, w_up, w_down))

    for layer in range(depth):
        g_pre = norm_mix_pre[layer][None, :]
        g_post = norm_mix_post[layer][None, :]
        if layer % 2 == 0:
            e = layer // 2
            proj = _rms_matmul(h, g_pre, w_in_bf, e, _gelu_epilogue, 2 * A_WIDTH)
            bias_full = jnp.repeat(b_spatial[e].T, A_GROUP_DIM, axis=1)
            mixed = _mixer(proj, w_spatial[e], bias_full, conv_w[e], seq)
            h = _matmul_rms_res(mixed, w_out_bf, e, g_post, h)
        else:
            o = layer // 2
            qkv = _rms_matmul(h, g_pre, w_qkv_bf, o, _rope_epilogue, 2 * D_MODEL,
                              extra=(cos_tab, sin_tab), extra_specs=rope_specs)
            att = _attention(qkv, batch, seq)
            h = _matmul_rms_res(att, w_o_bf, o, g_post, h)
        h = _mlp(h, norm_mlp_pre[layer][None, :], w_up_bf, w_down_bf, layer,
                 norm_mlp_post[layer][None, :])
    return h.reshape(batch, seq, d)
```

```python
import functools
import math

import jax
import jax.numpy as jnp
import numpy as np
from jax import lax
from jax.experimental import pallas as pl
from jax.experimental.pallas import tpu as pltpu

F32 = jnp.float32
BF16 = jnp.bfloat16

D_MODEL = 2048
A_WIDTH = D_MODEL // 2
B_WIDTH = D_MODEL - A_WIDTH
A_GROUPS = 8
A_GROUP_DIM = A_WIDTH // A_GROUPS
CHUNK = 128
CONV_WIDTH = 3
HEAD_DIM = 128
N_HEADS = D_MODEL // HEAD_DIM
DILATED_BRANCHES = ((128, 1), (512, 4), (2048, 16))
ATTN_BLOCK = 128
ATTN_PERM = 16
ATTN_PERM_STEP = 4
ATTN_GROUP = 32
ROPE_THETA = 10000.0
FFN_DIM = 4 * D_MODEL
RMS_EPS = 1e-6
LN_EPS = 1e-5

V7X_LANES = 128
BF16_SUBLANES = 16
V7X_VMEM_LIMIT_BYTES = 56 * 1024 * 1024

RMS_MATMUL_TM = 1024
RMS_MATMUL_TN = 1024
RMS_MATMUL_SUB_TN = 512
MIXER_TM = 256
RES_TM = 512
MLP_TM = 1024
MLP_TF = 512
CONV_HALO = 8


def _compiler_params(semantics):
    return pltpu.CompilerParams(dimension_semantics=semantics,
                                vmem_limit_bytes=V7X_VMEM_LIMIT_BYTES)


def _rmsnorm(x, g):
    ms = jnp.mean(x * x, axis=-1, keepdims=True)
    return (x * lax.rsqrt(ms + RMS_EPS)) * g


def _rms_matmul_kernel(x_ref, g_ref, w_ref, *rest, epilogue, n_special, n_extra):
    extra_refs = rest[:n_extra]
    o_ref = rest[n_extra]
    hn_ref = rest[n_extra + 1]
    j = pl.program_id(1)

    @pl.when(j == 0)
    def _():
        hn_ref[...] = _rmsnorm(x_ref[...], g_ref[...]).astype(BF16)

    sub = RMS_MATMUL_SUB_TN
    cols = [pl.ds(c * sub, sub) for c in range(w_ref.shape[1] // sub)]

    def product(col):
        return jnp.dot(hn_ref[...], w_ref[:, col], preferred_element_type=F32)

    @pl.when(j < n_special)
    def _():
        for col in cols:
            epilogue(product(col), extra_refs, o_ref.at[:, col])

    @pl.when(j >= n_special)
    def _():
        for col in cols:
            o_ref[:, col] = product(col)


def _rms_matmul(x, g, w, li, epilogue, special_cols, extra=(), extra_specs=(), out_dtype=F32):
    m, k = x.shape
    n = w.shape[2]
    tm, tn = RMS_MATMUL_TM, RMS_MATMUL_TN
    assert m % tm == 0 and n % tn == 0 and special_cols % tn == 0
    kern = functools.partial(_rms_matmul_kernel, epilogue=epilogue,
                             n_special=special_cols // tn, n_extra=len(extra))
    return pl.pallas_call(
        kern,
        grid=(m // tm, n // tn),
        in_specs=[pl.BlockSpec((tm, k), lambda i, j: (i, 0)),
                  pl.BlockSpec((1, k), lambda i, j: (0, 0)),
                  pl.BlockSpec((None, k, tn), lambda i, j: (li, 0, j)),
                  *extra_specs],
        out_specs=pl.BlockSpec((tm, tn), lambda i, j: (i, j)),
        out_shape=jax.ShapeDtypeStruct((m, n), out_dtype),
        scratch_shapes=[pltpu.VMEM((tm, k), BF16)],
        compiler_params=_compiler_params(("parallel", "arbitrary")),
        name="rms_matmul",
    )(x, g, w, *extra)


def _gelu_epilogue(p, extra_refs, o_ref):
    del extra_refs
    o_ref[...] = jax.nn.gelu(p)


def _rope_epilogue(p, extra_refs, o_ref):
    cos_ref, sin_ref = extra_refs
    cos = cos_ref[...]
    sin = sin_ref[...]
    for hh in range(p.shape[1] // HEAD_DIM):
        t = p[:, hh * HEAD_DIM:(hh + 1) * HEAD_DIM]
        o_ref[:, hh * HEAD_DIM:(hh + 1) * HEAD_DIM] = (
            t * cos + pltpu.roll(t, HEAD_DIM // 2, 1) * sin)


def _mixer_kernel(au_ref, av_ref, gb_ref, gc_ref, bx_ref, gch_ref, bxh_ref,
                  ws_ref, bias_ref, cw_ref, w_out_ref, g_post_ref, h_ref, o_ref, z_ref,
                  *, tiles_per_seq):
    tm = au_ref.shape[0]
    i = pl.program_id(0)

    av = av_ref[...]
    mu = jnp.mean(av, axis=-1, keepdims=True)
    cen = av - mu
    var = jnp.mean(cen * cen, axis=-1, keepdims=True)
    vn = (cen * lax.rsqrt(var + LN_EPS)).astype(BF16)
    row = lax.broadcasted_iota(jnp.int32, (CHUNK, CHUNK), 0)
    col = lax.broadcasted_iota(jnp.int32, (CHUNK, CHUNK), 1)
    causal = col <= row
    a_cols = []
    for g in range(A_GROUPS):
        cs = slice(g * A_GROUP_DIM, (g + 1) * A_GROUP_DIM)
        w_causal = jnp.where(causal, ws_ref[g], 0.0).astype(BF16)
        chunks = []
        for c in range(tm // CHUNK):
            rs = slice(c * CHUNK, (c + 1) * CHUNK)
            mixed = jnp.dot(w_causal, vn[rs, cs], preferred_element_type=F32) + bias_ref[:, cs]
            chunks.append((au_ref[rs, cs] * mixed).astype(BF16))
        a_cols.append(jnp.concatenate(chunks, axis=0))

    z = gc_ref[...] * bx_ref[...]
    at_seq_start = (i % tiles_per_seq) == 0
    z_halo = jnp.where(at_seq_start, 0.0, gch_ref[...] * bxh_ref[...])
    z_ref[0:CONV_HALO, :] = z_halo
    z_ref[CONV_HALO:CONV_HALO + tm, :] = z
    y = (cw_ref[2:3, :] * z
         + cw_ref[1:2, :] * z_ref[CONV_HALO - 1:CONV_HALO - 1 + tm, :]
         + cw_ref[0:1, :] * z_ref[CONV_HALO - 2:CONV_HALO - 2 + tm, :])
    b_out = (gb_ref[...] * y).astype(BF16)

    mixed_ab = jnp.concatenate(a_cols + [b_out], axis=1)
    f = jnp.dot(mixed_ab, w_out_ref[...], preferred_element_type=F32)
    o_ref[...] = h_ref[...] + _rmsnorm(f, g_post_ref[...])


def _mixer(proj, w_spatial, bias_full, conv_w, w_out, li, g_post, h, seq):
    m = proj.shape[0]
    tm = MIXER_TM
    assert seq % tm == 0 and tm % CHUNK == 0 and A_WIDTH == B_WIDTH
    tiles_per_seq = seq // tm
    halo_blocks = tm // CONV_HALO
    seg = lambda s: pl.BlockSpec((tm, A_WIDTH), lambda i, s=s: (i, s))
    halo = lambda s: pl.BlockSpec(
        (CONV_HALO, A_WIDTH), lambda i, s=s: (jnp.maximum(i * halo_blocks - 1, 0), s))
    kern = functools.partial(_mixer_kernel, tiles_per_seq=tiles_per_seq)
    return pl.pallas_call(
        kern,
        grid=(m // tm,),
        in_specs=[seg(0), seg(1), seg(2), seg(3), seg(4), halo(3), halo(4),
                  pl.BlockSpec((A_GROUPS, CHUNK, CHUNK), lambda i: (0, 0, 0)),
                  pl.BlockSpec((CHUNK, A_WIDTH), lambda i: (0, 0)),
                  pl.BlockSpec((CONV_WIDTH, B_WIDTH), lambda i: (0, 0)),
                  pl.BlockSpec((None, D_MODEL, D_MODEL), lambda i: (li, 0, 0)),
                  pl.BlockSpec((1, D_MODEL), lambda i: (0, 0)),
                  pl.BlockSpec((tm, D_MODEL), lambda i: (i, 0))],
        out_specs=pl.BlockSpec((tm, D_MODEL), lambda i: (i, 0)),
        out_shape=jax.ShapeDtypeStruct((m, D_MODEL), F32),
        scratch_shapes=[pltpu.VMEM((CONV_HALO + tm, B_WIDTH), F32)],
        compiler_params=_compiler_params(("parallel",)),
        name="mixer",
    )(proj, proj, proj, proj, proj, proj, proj, w_spatial, bias_full, conv_w, w_out, g_post, h)


def _attn_band_bias():
    blk = ATTN_BLOCK
    tables = []
    for _, d in DILATED_BRANCHES:
        pieces = ATTN_PERM // d
        rows = blk // pieces
        i = np.arange(blk)
        run = i // rows
        if pieces == ATTN_PERM:
            run = ATTN_PERM_STEP * (run % ATTN_PERM_STEP) + run // ATTN_PERM_STEP
        uq = pieces * (i % rows) + run
        uk = np.concatenate([uq, blk + uq])
        step = uq[:, None] + blk - uk[None, :]
        valid = (step >= 0) & (step <= blk)
        first = valid & (uk[None, :] >= blk)
        tables.append(np.stack([np.where(valid, 0.0, -np.inf), np.where(first, 0.0, -np.inf)]))
    return np.stack(tables).astype(np.float32)


def _attn_block(qb, k_prev, k_cur, v_prev, v_cur, bias, old):
    kw = jnp.concatenate([k_prev, k_cur], axis=0)
    vw = jnp.concatenate([v_prev, v_cur], axis=0)
    s = lax.dot_general(qb, kw, (((1,), (1,)), ((), ())), preferred_element_type=F32) + bias
    m_blk = jnp.max(s, axis=-1, keepdims=True)
    m_new = jnp.broadcast_to(m_blk, (s.shape[0], HEAD_DIM))
    if old is not None:
        m_new = jnp.maximum(old[0], m_new)
    p = jnp.exp2(s - jnp.concatenate([m_new, m_new], axis=1)).astype(BF16)
    v_ones = jnp.concatenate([vw, jnp.ones_like(vw)], axis=1)
    pv = jnp.dot(p, v_ones, preferred_element_type=F32)
    a_new, l_new = pv[:, :HEAD_DIM], pv[:, HEAD_DIM:]
    if old is not None:
        corr = jnp.exp2(old[0] - m_new)
        l_new = old[1] * corr + l_new
        a_new = old[2] * corr + a_new
    return m_new, l_new, a_new


def _attn_kernel(q_ref, k_ref, v_ref, bias_ref, o_ref, qp_ref, kp_ref, vp_ref,
                 qh_ref, kh_ref, vh_ref, acc_ref, m_ref, l_ref, tmp_ref, *, seq, dilations):
    blk = ATTN_BLOCK
    lp = seq // ATTN_PERM
    step = ATTN_PERM_STEP
    lq = seq // step
    q_scale = HEAD_DIM ** -0.5 * math.log2(math.e)
    for src_ref, dst_ref, half_ref, scale in ((q_ref, qp_ref, qh_ref, q_scale),
                                              (k_ref, kp_ref, kh_ref, None),
                                              (v_ref, vp_ref, vh_ref, None)):
        for lo in range(step):
            tmp_ref[lo * lq:(lo + 1) * lq, :] = src_ref[pl.ds(lo, lq, stride=step), :]
        for lo in range(step):
            for hi in range(step):
                run = lo * step + hi
                rows = tmp_ref[pl.ds(lo * lq + hi, lp, stride=step), :]
                if scale is not None:
                    rows = rows * scale
                dst_ref[run * lp:(run + 1) * lp, :] = rows
                half_ref[run * lp:(run + 1) * lp, :] = rows.astype(BF16)

    for bi, d in enumerate(dilations):
        pieces = ATTN_PERM // d
        rows = blk // pieces
        nb = lp // rows
        gn = min(nb, ATTN_GROUP)
        gs = min(ATTN_GROUP // gn, d)
        n_groups = nb // gn
        assert nb % gn == 0 and d % gs == 0
        packed = rows % BF16_SUBLANES == 0
        q_src, k_src, v_src = (qh_ref, kh_ref, vh_ref) if packed else (qp_ref, kp_ref, vp_ref)

        def starts(r_sub, n, d=d, pieces=pieces, rows=rows):
            return [pl.multiple_of((r_sub * pieces + a) * lp + n * rows, rows) for a in range(pieces)]

        def gather(ref, st, rows=rows):
            parts = [ref[pl.ds(s0, rows), :] for s0 in st]
            return parts[0] if len(parts) == 1 else jnp.concatenate(parts, axis=0)

        def gather_half(ref, st):
            return gather(ref, st).astype(BF16)

        def scatter(ref, st, val, rows=rows):
            for a, s0 in enumerate(st):
                ref[pl.ds(s0, rows), :] = val[a * rows:(a + 1) * rows]

        def body(it, carry, bi=bi, gn=gn, gs=gs, n_groups=n_groups,
                 q_src=q_src, k_src=k_src, v_src=v_src):
            sg = it // n_groups
            n0 = (it % n_groups) * gn
            work = []
            for si in range(gs):
                r_sub = sg * gs + si
                st_prev = starts(r_sub, jnp.maximum(n0 - 1, 0))
                kb = [gather_half(k_src, st_prev)]
                vb = [gather_half(v_src, st_prev)]
                for j in range(gn):
                    st = starts(r_sub, n0 + j)
                    kb.append(gather_half(k_src, st))
                    vb.append(gather_half(v_src, st))
                    if j > 0:
                        bias = bias_ref[bi, 0]
                    elif n_groups == 1:
                        bias = bias_ref[bi, 1]
                    else:
                        bias = bias_ref[bi, jnp.where(n0 == 0, 1, 0)]
                    old = None if bi == 0 else (gather(m_ref, st), gather(l_ref, st),
                                                gather(acc_ref, st))
                    work.append((st, gather_half(q_src, st), kb[j], kb[j + 1],
                                 vb[j], vb[j + 1], bias, old))
            results = [(w[0],) + _attn_block(*w[1:]) for w in work]
            for st, m_new, l_new, a_new in results:
                scatter(m_ref, st, m_new)
                scatter(l_ref, st, l_new)
                scatter(acc_ref, st, a_new)
            return carry

        lax.fori_loop(0, (d // gs) * n_groups, body, 0)

    for lo in range(step):
        for hi in range(step):
            src = slice((lo * step + hi) * lp, (lo * step + hi + 1) * lp)
            tmp_ref[pl.ds(lo * lq + hi, lp, stride=step), :] = acc_ref[src, :] / l_ref[src, :]
    for lo in range(step):
        o_ref[pl.ds(lo, lq, stride=step), :] = tmp_ref[lo * lq:(lo + 1) * lq, :]


def _attention(qkv, batch, seq):
    dilations = tuple(d for _, d in DILATED_BRANCHES)
    for window, d in DILATED_BRANCHES:
        assert window // d == ATTN_BLOCK and seq % (d * ATTN_BLOCK) == 0
        assert ATTN_PERM % d == 0 and ATTN_BLOCK % (ATTN_PERM // d) == 0
    qkv3 = qkv.reshape(batch, seq, 3 * D_MODEL)
    bias = jnp.asarray(_attn_band_bias())
    spec = lambda part: pl.BlockSpec((None, seq, HEAD_DIM),
                                     lambda b, h, part=part: (b, 0, part * N_HEADS + h))
    kern = functools.partial(_attn_kernel, seq=seq, dilations=dilations)
    out = pl.pallas_call(
        kern,
        grid=(batch, N_HEADS),
        in_specs=[spec(0), spec(1), spec(2),
                  pl.BlockSpec(bias.shape, lambda b, h: (0, 0, 0, 0))],
        out_specs=pl.BlockSpec((None, seq, HEAD_DIM), lambda b, h: (b, 0, h)),
        out_shape=jax.ShapeDtypeStruct((batch, seq, D_MODEL), F32),
        scratch_shapes=[pltpu.VMEM((seq, HEAD_DIM), F32)] * 3 + [pltpu.VMEM((seq, HEAD_DIM), BF16)] * 3
                       + [pltpu.VMEM((seq, HEAD_DIM), F32)] * 4,
        compiler_params=_compiler_params(("parallel", "parallel")),
        name="attention",
    )(qkv3, qkv3, qkv3, bias)
    return out.reshape(batch * seq, D_MODEL)


def _matmul_rms_res_kernel(a_ref, w_ref, g_ref, h_ref, o_ref):
    f = jnp.dot(a_ref[...].astype(BF16), w_ref[...], preferred_element_type=F32)
    o_ref[...] = h_ref[...] + _rmsnorm(f, g_ref[...])


def _matmul_rms_res(a, w, li, g, h):
    m, k = a.shape
    n = w.shape[2]
    tm = RES_TM
    assert m % tm == 0
    return pl.pallas_call(
        _matmul_rms_res_kernel,
        grid=(m // tm,),
        in_specs=[pl.BlockSpec((tm, k), lambda i: (i, 0)),
                  pl.BlockSpec((None, k, n), lambda i: (li, 0, 0)),
                  pl.BlockSpec((1, n), lambda i: (0, 0)),
                  pl.BlockSpec((tm, n), lambda i: (i, 0))],
        out_specs=pl.BlockSpec((tm, n), lambda i: (i, 0)),
        out_shape=jax.ShapeDtypeStruct((m, n), F32),
        compiler_params=_compiler_params(("parallel",)),
        name="matmul_rms_res",
    )(a, w, g, h)


def _mlp_kernel(x_ref, g_pre_ref, w_up_ref, w_down_ref, g_post_ref, o_ref, hn_ref):
    k = pl.program_id(1)

    @pl.when(k == 0)
    def _():
        hn_ref[...] = _rmsnorm(x_ref[...], g_pre_ref[...]).astype(BF16)
        o_ref[...] = jnp.zeros_like(o_ref)

    u = jnp.dot(hn_ref[...], w_up_ref[...], preferred_element_type=F32)
    u = jnp.square(jnp.maximum(u, 0.0)).astype(BF16)
    o_ref[...] += jnp.dot(u, w_down_ref[...], preferred_element_type=F32)

    @pl.when(k == pl.num_programs(1) - 1)
    def _():
        o_ref[...] = x_ref[...] + _rmsnorm(o_ref[...], g_post_ref[...])


def _mlp(h, g_pre, w_up, w_down, li, g_post):
    m, d = h.shape
    f = w_up.shape[2]
    tm, tf = MLP_TM, MLP_TF
    assert m % tm == 0 and f % tf == 0
    return pl.pallas_call(
        _mlp_kernel,
        grid=(m // tm, f // tf),
        in_specs=[pl.BlockSpec((tm, d), lambda i, k: (i, 0), pipeline_mode=pl.Buffered(1)),
                  pl.BlockSpec((1, d), lambda i, k: (0, 0)),
                  pl.BlockSpec((None, d, tf), lambda i, k: (li, 0, k)),
                  pl.BlockSpec((None, tf, d), lambda i, k: (li, k, 0)),
                  pl.BlockSpec((1, d), lambda i, k: (0, 0))],
        out_specs=pl.BlockSpec((tm, d), lambda i, k: (i, 0)),
        out_shape=jax.ShapeDtypeStruct((m, d), F32),
        scratch_shapes=[pltpu.VMEM((tm, d), BF16)],
        compiler_params=_compiler_params(("parallel", "arbitrary")),
        name="mlp",
    )(h, g_pre, w_up, w_down, g_post)


def _rope_tables(seq):
    half = HEAD_DIM // 2
    inv_freq = ROPE_THETA ** (-jnp.arange(half, dtype=F32) * 2.0 / HEAD_DIM)
    ang = jnp.arange(seq, dtype=jnp.int32).astype(F32)[:, None] * inv_freq[None, :]
    cos = jnp.cos(ang)
    sin = jnp.sin(ang)
    return jnp.concatenate([cos, cos], axis=-1), jnp.concatenate([-sin, sin], axis=-1)


def kernel(x, norm_mix_pre, norm_mix_post, norm_mlp_pre, norm_mlp_post, w_in_ab, w_spatial,
           b_spatial, conv_w, w_out_ab, w_qkv, w_o, w_up, w_down):
    batch, seq, d = x.shape
    assert d == D_MODEL
    depth = norm_mix_pre.shape[0]
    m = batch * seq
    h = x.reshape(m, d)
    cos_tab, sin_tab = _rope_tables(seq)
    tiles_per_seq = seq // RMS_MATMUL_TM
    rope_specs = [pl.BlockSpec((RMS_MATMUL_TM, HEAD_DIM), lambda i, j: (i % tiles_per_seq, 0))] * 2

    w_in_bf, w_out_bf, w_qkv_bf, w_o_bf, w_up_bf, w_down_bf = (
        w.astype(BF16) for w in (w_in_ab, w_out_ab, w_qkv, w_o, w_up, w_down))

    for layer in range(depth):
        g_pre = norm_mix_pre[layer][None, :]
        g_post = norm_mix_post[layer][None, :]
        if layer % 2 == 0:
            e = layer // 2
            proj = _rms_matmul(h, g_pre, w_in_bf, e, _gelu_epilogue, 2 * A_WIDTH)
            bias_full = jnp.repeat(b_spatial[e].T, A_GROUP_DIM, axis=1)
            h = _mixer(proj, w_spatial[e], bias_full, conv_w[e], w_out_bf, e, g_post, h, seq)
        else:
            o = layer // 2
            qkv = _rms_matmul(h, g_pre, w_qkv_bf, o, _rope_epilogue, 2 * D_MODEL,
                              extra=(cos_tab, sin_tab), extra_specs=rope_specs)
            att = _attention(qkv, batch, seq)
            h = _matmul_rms_res(att, w_o_bf, o, g_post, h)
        h = _mlp(h, norm_mlp_pre[layer][None, :], w_up_bf, w_down_bf, layer,
                 norm_mlp_post[layer][None, :])
    return h.reshape(batch, seq, d)
```

```python
import functools
import math

import jax
import jax.numpy as jnp
import numpy as np
from jax import lax
from jax.experimental import pallas as pl
from jax.experimental.pallas import tpu as pltpu

F32 = jnp.float32
BF16 = jnp.bfloat16

D_MODEL = 2048
A_WIDTH = D_MODEL // 2
B_WIDTH = D_MODEL - A_WIDTH
A_GROUPS = 8
A_GROUP_DIM = A_WIDTH // A_GROUPS
CHUNK = 128
CONV_WIDTH = 3
HEAD_DIM = 128
N_HEADS = D_MODEL // HEAD_DIM
DILATED_BRANCHES = ((128, 1), (512, 4), (2048, 16))
ATTN_BLOCK = 128
ATTN_PERM = 16
ATTN_PERM_STEP = 4
ATTN_GROUP = 32
ROPE_THETA = 10000.0
FFN_DIM = 4 * D_MODEL
RMS_EPS = 1e-6
LN_EPS = 1e-5

V7X_LANES = 128
BF16_SUBLANES = 16
V7X_VMEM_LIMIT_BYTES = 60 * 1024 * 1024

RMS_MATMUL_TM = 1024
RMS_MATMUL_TN = 1024
RMS_MATMUL_SUB_TN = 512
MIXER_TM = 256
RES_TM = 512
MLP_TM = 1024
MLP_TF = 1024
MLP_SUB = 512
CONV_HALO = 8


def _compiler_params(semantics):
    return pltpu.CompilerParams(dimension_semantics=semantics,
                                vmem_limit_bytes=V7X_VMEM_LIMIT_BYTES)


def _rmsnorm(x, g):
    ms = jnp.mean(x * x, axis=-1, keepdims=True)
    return (x * lax.rsqrt(ms + RMS_EPS)) * g


def _rms_matmul_kernel(x_ref, g_ref, w_ref, *rest, epilogue, n_special, n_extra):
    extra_refs = rest[:n_extra]
    o_ref = rest[n_extra]
    hn_ref = rest[n_extra + 1]
    j = pl.program_id(1)

    @pl.when(j == 0)
    def _():
        hn_ref[...] = _rmsnorm(x_ref[...], g_ref[...]).astype(BF16)

    sub = RMS_MATMUL_SUB_TN
    cols = [pl.ds(c * sub, sub) for c in range(w_ref.shape[1] // sub)]

    def product(col):
        return jnp.dot(hn_ref[...], w_ref[:, col], preferred_element_type=F32)

    @pl.when(j < n_special)
    def _():
        for col in cols:
            epilogue(product(col), extra_refs, o_ref.at[:, col])

    @pl.when(j >= n_special)
    def _():
        for col in cols:
            o_ref[:, col] = product(col)


def _rms_matmul(x, g, w, li, epilogue, special_cols, extra=(), extra_specs=(), out_dtype=F32):
    m, k = x.shape
    n = w.shape[2]
    tm, tn = RMS_MATMUL_TM, RMS_MATMUL_TN
    assert m % tm == 0 and n % tn == 0 and special_cols % tn == 0
    kern = functools.partial(_rms_matmul_kernel, epilogue=epilogue,
                             n_special=special_cols // tn, n_extra=len(extra))
    return pl.pallas_call(
        kern,
        grid=(m // tm, n // tn),
        in_specs=[pl.BlockSpec((tm, k), lambda i, j: (i, 0)),
                  pl.BlockSpec((1, k), lambda i, j: (0, 0)),
                  pl.BlockSpec((None, k, tn), lambda i, j: (li, 0, j)),
                  *extra_specs],
        out_specs=pl.BlockSpec((tm, tn), lambda i, j: (i, j)),
        out_shape=jax.ShapeDtypeStruct((m, n), out_dtype),
        scratch_shapes=[pltpu.VMEM((tm, k), BF16)],
        compiler_params=_compiler_params(("parallel", "arbitrary")),
        name="rms_matmul",
    )(x, g, w, *extra)


def _gelu_epilogue(p, extra_refs, o_ref):
    del extra_refs
    o_ref[...] = jax.nn.gelu(p)


def _rope_epilogue(p, extra_refs, o_ref):
    cos_ref, sin_ref = extra_refs
    cos = cos_ref[...]
    sin = sin_ref[...]
    for hh in range(p.shape[1] // HEAD_DIM):
        t = p[:, hh * HEAD_DIM:(hh + 1) * HEAD_DIM]
        o_ref[:, hh * HEAD_DIM:(hh + 1) * HEAD_DIM] = (
            t * cos + pltpu.roll(t, HEAD_DIM // 2, 1) * sin)


def _mixer_kernel(au_ref, av_ref, gb_ref, gc_ref, bx_ref, gch_ref, bxh_ref,
                  ws_ref, bias_ref, cw_ref, w_out_ref, g_post_ref, h_ref, o_ref, z_ref,
                  *, tiles_per_seq):
    tm = au_ref.shape[0]
    i = pl.program_id(0)

    av = av_ref[...]
    mu = jnp.mean(av, axis=-1, keepdims=True)
    cen = av - mu
    var = jnp.mean(cen * cen, axis=-1, keepdims=True)
    vn = (cen * lax.rsqrt(var + LN_EPS)).astype(BF16)
    row = lax.broadcasted_iota(jnp.int32, (CHUNK, CHUNK), 0)
    col = lax.broadcasted_iota(jnp.int32, (CHUNK, CHUNK), 1)
    causal = col <= row
    a_cols = []
    for g in range(A_GROUPS):
        cs = slice(g * A_GROUP_DIM, (g + 1) * A_GROUP_DIM)
        w_causal = jnp.where(causal, ws_ref[g], 0.0).astype(BF16)
        chunks = []
        for c in range(tm // CHUNK):
            rs = slice(c * CHUNK, (c + 1) * CHUNK)
            mixed = jnp.dot(w_causal, vn[rs, cs], preferred_element_type=F32) + bias_ref[:, cs]
            chunks.append((au_ref[rs, cs] * mixed).astype(BF16))
        a_cols.append(jnp.concatenate(chunks, axis=0))

    z = gc_ref[...] * bx_ref[...]
    at_seq_start = (i % tiles_per_seq) == 0
    z_halo = jnp.where(at_seq_start, 0.0, gch_ref[...] * bxh_ref[...])
    z_ref[0:CONV_HALO, :] = z_halo
    z_ref[CONV_HALO:CONV_HALO + tm, :] = z
    y = (cw_ref[2:3, :] * z
         + cw_ref[1:2, :] * z_ref[CONV_HALO - 1:CONV_HALO - 1 + tm, :]
         + cw_ref[0:1, :] * z_ref[CONV_HALO - 2:CONV_HALO - 2 + tm, :])
    b_out = (gb_ref[...] * y).astype(BF16)

    mixed_ab = jnp.concatenate(a_cols + [b_out], axis=1)
    f = jnp.dot(mixed_ab, w_out_ref[...], preferred_element_type=F32)
    o_ref[...] = h_ref[...] + _rmsnorm(f, g_post_ref[...])


def _mixer(proj, w_spatial, bias_full, conv_w, w_out, li, g_post, h, seq):
    m = proj.shape[0]
    tm = MIXER_TM
    assert seq % tm == 0 and tm % CHUNK == 0 and A_WIDTH == B_WIDTH
    tiles_per_seq = seq // tm
    halo_blocks = tm // CONV_HALO
    seg = lambda s: pl.BlockSpec((tm, A_WIDTH), lambda i, s=s: (i, s))
    halo = lambda s: pl.BlockSpec(
        (CONV_HALO, A_WIDTH), lambda i, s=s: (jnp.maximum(i * halo_blocks - 1, 0), s))
    kern = functools.partial(_mixer_kernel, tiles_per_seq=tiles_per_seq)
    return pl.pallas_call(
        kern,
        grid=(m // tm,),
        in_specs=[seg(0), seg(1), seg(2), seg(3), seg(4), halo(3), halo(4),
                  pl.BlockSpec((A_GROUPS, CHUNK, CHUNK), lambda i: (0, 0, 0)),
                  pl.BlockSpec((CHUNK, A_WIDTH), lambda i: (0, 0)),
                  pl.BlockSpec((CONV_WIDTH, B_WIDTH), lambda i: (0, 0)),
                  pl.BlockSpec((None, D_MODEL, D_MODEL), lambda i: (li, 0, 0)),
                  pl.BlockSpec((1, D_MODEL), lambda i: (0, 0)),
                  pl.BlockSpec((tm, D_MODEL), lambda i: (i, 0))],
        out_specs=pl.BlockSpec((tm, D_MODEL), lambda i: (i, 0)),
        out_shape=jax.ShapeDtypeStruct((m, D_MODEL), F32),
        scratch_shapes=[pltpu.VMEM((CONV_HALO + tm, B_WIDTH), F32)],
        compiler_params=_compiler_params(("parallel",)),
        name="mixer",
    )(proj, proj, proj, proj, proj, proj, proj, w_spatial, bias_full, conv_w, w_out, g_post, h)


def _attn_band_bias():
    blk = ATTN_BLOCK
    tables = []
    for _, d in DILATED_BRANCHES:
        pieces = ATTN_PERM // d
        rows = blk // pieces
        i = np.arange(blk)
        run = i // rows
        if pieces == ATTN_PERM:
            run = ATTN_PERM_STEP * (run % ATTN_PERM_STEP) + run // ATTN_PERM_STEP
        uq = pieces * (i % rows) + run
        uk = np.concatenate([uq, blk + uq])
        step = uq[:, None] + blk - uk[None, :]
        valid = (step >= 0) & (step <= blk)
        first = valid & (uk[None, :] >= blk)
        tables.append(np.stack([np.where(valid, 0.0, -np.inf), np.where(first, 0.0, -np.inf)]))
    return np.stack(tables).astype(np.float32)


def _attn_block(qb, k_prev, k_cur, v_prev, v_cur, bias, old):
    kw = jnp.concatenate([k_prev, k_cur], axis=0)
    vw = jnp.concatenate([v_prev, v_cur], axis=0)
    s = lax.dot_general(qb, kw, (((1,), (1,)), ((), ())), preferred_element_type=F32) + bias
    m_blk = jnp.max(s, axis=-1, keepdims=True)
    m_new = jnp.broadcast_to(m_blk, (s.shape[0], HEAD_DIM))
    if old is not None:
        m_new = jnp.maximum(old[0], m_new)
    p = jnp.exp2(s - jnp.concatenate([m_new, m_new], axis=1)).astype(BF16)
    v_ones = jnp.concatenate([vw, jnp.ones_like(vw)], axis=1)
    pv = jnp.dot(p, v_ones, preferred_element_type=F32)
    a_new, l_new = pv[:, :HEAD_DIM], pv[:, HEAD_DIM:]
    if old is not None:
        corr = jnp.exp2(old[0] - m_new)
        l_new = old[1] * corr + l_new
        a_new = old[2] * corr + a_new
    return m_new, l_new, a_new


def _attn_kernel(q_ref, k_ref, v_ref, bias_ref, o_ref, qp_ref, kp_ref, vp_ref,
                 qh_ref, kh_ref, vh_ref, acc_ref, m_ref, l_ref, tmp_ref, *, seq, dilations):
    blk = ATTN_BLOCK
    lp = seq // ATTN_PERM
    step = ATTN_PERM_STEP
    lq = seq // step
    q_scale = HEAD_DIM ** -0.5 * math.log2(math.e)
    for src_ref, dst_ref, half_ref, scale in ((q_ref, qp_ref, qh_ref, q_scale),
                                              (k_ref, kp_ref, kh_ref, None),
                                              (v_ref, vp_ref, vh_ref, None)):
        for lo in range(step):
            tmp_ref[lo * lq:(lo + 1) * lq, :] = src_ref[pl.ds(lo, lq, stride=step), :]
        for lo in range(step):
            for hi in range(step):
                run = lo * step + hi
                rows = tmp_ref[pl.ds(lo * lq + hi, lp, stride=step), :]
                if scale is not None:
                    rows = rows * scale
                dst_ref[run * lp:(run + 1) * lp, :] = rows
                half_ref[run * lp:(run + 1) * lp, :] = rows.astype(BF16)

    for bi, d in enumerate(dilations):
        pieces = ATTN_PERM // d
        rows = blk // pieces
        nb = lp // rows
        gn = min(nb, ATTN_GROUP)
        gs = min(ATTN_GROUP // gn, d)
        n_groups = nb // gn
        assert nb % gn == 0 and d % gs == 0
        packed = rows % BF16_SUBLANES == 0
        q_src, k_src, v_src = (qh_ref, kh_ref, vh_ref) if packed else (qp_ref, kp_ref, vp_ref)

        def starts(r_sub, n, d=d, pieces=pieces, rows=rows):
            return [pl.multiple_of((r_sub * pieces + a) * lp + n * rows, rows) for a in range(pieces)]

        def gather(ref, st, rows=rows):
            parts = [ref[pl.ds(s0, rows), :] for s0 in st]
            return parts[0] if len(parts) == 1 else jnp.concatenate(parts, axis=0)

        def gather_half(ref, st):
            return gather(ref, st).astype(BF16)

        def scatter(ref, st, val, rows=rows):
            for a, s0 in enumerate(st):
                ref[pl.ds(s0, rows), :] = val[a * rows:(a + 1) * rows]

        def body(it, carry, bi=bi, gn=gn, gs=gs, n_groups=n_groups,
                 q_src=q_src, k_src=k_src, v_src=v_src):
            sg = it // n_groups
            n0 = (it % n_groups) * gn
            work = []
            for si in range(gs):
                r_sub = sg * gs + si
                st_prev = starts(r_sub, jnp.maximum(n0 - 1, 0))
                kb = [gather_half(k_src, st_prev)]
                vb = [gather_half(v_src, st_prev)]
                for j in range(gn):
                    st = starts(r_sub, n0 + j)
                    kb.append(gather_half(k_src, st))
                    vb.append(gather_half(v_src, st))
                    if j > 0:
                        bias = bias_ref[bi, 0]
                    elif n_groups == 1:
                        bias = bias_ref[bi, 1]
                    else:
                        bias = bias_ref[bi, jnp.where(n0 == 0, 1, 0)]
                    old = None if bi == 0 else (gather(m_ref, st), gather(l_ref, st),
                                                gather(acc_ref, st))
                    work.append((st, gather_half(q_src, st), kb[j], kb[j + 1],
                                 vb[j], vb[j + 1], bias, old))
            results = [(w[0],) + _attn_block(*w[1:]) for w in work]
            for st, m_new, l_new, a_new in results:
                scatter(m_ref, st, m_new)
                scatter(l_ref, st, l_new)
                scatter(acc_ref, st, a_new)
            return carry

        lax.fori_loop(0, (d // gs) * n_groups, body, 0)

    for lo in range(step):
        for hi in range(step):
            src = slice((lo * step + hi) * lp, (lo * step + hi + 1) * lp)
            tmp_ref[pl.ds(lo * lq + hi, lp, stride=step), :] = acc_ref[src, :] / l_ref[src, :]
    for lo in range(step):
        o_ref[pl.ds(lo, lq, stride=step), :] = tmp_ref[lo * lq:(lo + 1) * lq, :]


def _attention(qkv, batch, seq):
    dilations = tuple(d for _, d in DILATED_BRANCHES)
    for window, d in DILATED_BRANCHES:
        assert window // d == ATTN_BLOCK and seq % (d * ATTN_BLOCK) == 0
        assert ATTN_PERM % d == 0 and ATTN_BLOCK % (ATTN_PERM // d) == 0
    qkv3 = qkv.reshape(batch, seq, 3 * D_MODEL)
    bias = jnp.asarray(_attn_band_bias())
    spec = lambda part: pl.BlockSpec((None, seq, HEAD_DIM),
                                     lambda b, h, part=part: (b, 0, part * N_HEADS + h))
    kern = functools.partial(_attn_kernel, seq=seq, dilations=dilations)
    out = pl.pallas_call(
        kern,
        grid=(batch, N_HEADS),
        in_specs=[spec(0), spec(1), spec(2),
                  pl.BlockSpec(bias.shape, lambda b, h: (0, 0, 0, 0))],
        out_specs=pl.BlockSpec((None, seq, HEAD_DIM), lambda b, h: (b, 0, h)),
        out_shape=jax.ShapeDtypeStruct((batch, seq, D_MODEL), F32),
        scratch_shapes=[pltpu.VMEM((seq, HEAD_DIM), F32)] * 3 + [pltpu.VMEM((seq, HEAD_DIM), BF16)] * 3
                       + [pltpu.VMEM((seq, HEAD_DIM), F32)] * 4,
        compiler_params=_compiler_params(("parallel", "parallel")),
        name="attention",
    )(qkv3, qkv3, qkv3, bias)
    return out.reshape(batch * seq, D_MODEL)


def _matmul_rms_res_kernel(a_ref, w_ref, g_ref, h_ref, o_ref):
    f = jnp.dot(a_ref[...].astype(BF16), w_ref[...], preferred_element_type=F32)
    o_ref[...] = h_ref[...] + _rmsnorm(f, g_ref[...])


def _matmul_rms_res(a, w, li, g, h):
    m, k = a.shape
    n = w.shape[2]
    tm = RES_TM
    assert m % tm == 0
    return pl.pallas_call(
        _matmul_rms_res_kernel,
        grid=(m // tm,),
        in_specs=[pl.BlockSpec((tm, k), lambda i: (i, 0)),
                  pl.BlockSpec((None, k, n), lambda i: (li, 0, 0)),
                  pl.BlockSpec((1, n), lambda i: (0, 0)),
                  pl.BlockSpec((tm, n), lambda i: (i, 0))],
        out_specs=pl.BlockSpec((tm, n), lambda i: (i, 0)),
        out_shape=jax.ShapeDtypeStruct((m, n), F32),
        compiler_params=_compiler_params(("parallel",)),
        name="matmul_rms_res",
    )(a, w, g, h)


def _mlp_kernel(x_ref, g_pre_ref, w_up_ref, w_down_ref, g_post_ref, o_ref, hn_ref):
    k = pl.program_id(1)

    @pl.when(k == 0)
    def _():
        hn_ref[...] = _rmsnorm(x_ref[...], g_pre_ref[...]).astype(BF16)
        o_ref[...] = jnp.zeros_like(o_ref)

    sub = MLP_SUB
    u = [jnp.dot(hn_ref[...], w_up_ref[:, pl.ds(c * sub, sub)], preferred_element_type=F32)
         for c in range(w_up_ref.shape[1] // sub)]
    u = jnp.concatenate([jnp.square(jnp.maximum(t, 0.0)).astype(BF16) for t in u], axis=1)
    for n in range(o_ref.shape[1] // sub):
        cols = pl.ds(n * sub, sub)
        o_ref[:, cols] += jnp.dot(u, w_down_ref[:, cols], preferred_element_type=F32)

    @pl.when(k == pl.num_programs(1) - 1)
    def _():
        o_ref[...] = x_ref[...] + _rmsnorm(o_ref[...], g_post_ref[...])


def _mlp(h, g_pre, w_up, w_down, li, g_post):
    m, d = h.shape
    f = w_up.shape[2]
    tm, tf = MLP_TM, MLP_TF
    assert m % tm == 0 and f % tf == 0
    return pl.pallas_call(
        _mlp_kernel,
        grid=(m // tm, f // tf),
        in_specs=[pl.BlockSpec((tm, d), lambda i, k: (i, 0), pipeline_mode=pl.Buffered(1)),
                  pl.BlockSpec((1, d), lambda i, k: (0, 0)),
                  pl.BlockSpec((None, d, tf), lambda i, k: (li, 0, k)),
                  pl.BlockSpec((None, tf, d), lambda i, k: (li, k, 0)),
                  pl.BlockSpec((1, d), lambda i, k: (0, 0))],
        out_specs=pl.BlockSpec((tm, d), lambda i, k: (i, 0)),
        out_shape=jax.ShapeDtypeStruct((m, d), F32),
        scratch_shapes=[pltpu.VMEM((tm, d), BF16)],
        compiler_params=_compiler_params(("parallel", "arbitrary")),
        name="mlp",
    )(h, g_pre, w_up, w_down, g_post)


def _rope_tables(seq):
    half = HEAD_DIM // 2
    inv_freq = ROPE_THETA ** (-jnp.arange(half, dtype=F32) * 2.0 / HEAD_DIM)
    ang = jnp.arange(seq, dtype=jnp.int32).astype(F32)[:, None] * inv_freq[None, :]
    cos = jnp.cos(ang)
    sin = jnp.sin(ang)
    return jnp.concatenate([cos, cos], axis=-1), jnp.concatenate([-sin, sin], axis=-1)


def kernel(x, norm_mix_pre, norm_mix_post, norm_mlp_pre, norm_mlp_post, w_in_ab, w_spatial,
           b_spatial, conv_w, w_out_ab, w_qkv, w_o, w_up, w_down):
    batch, seq, d = x.shape
    assert d == D_MODEL
    depth = norm_mix_pre.shape[0]
    m = batch * seq
    h = x.reshape(m, d)
    cos_tab, sin_tab = _rope_tables(seq)
    tiles_per_seq = seq // RMS_MATMUL_TM
    rope_specs = [pl.BlockSpec((RMS_MATMUL_TM, HEAD_DIM), lambda i, j: (i % tiles_per_seq, 0))] * 2

    w_in_bf, w_out_bf, w_qkv_bf, w_o_bf, w_up_bf, w_down_bf = (
        w.astype(BF16) for w in (w_in_ab, w_out_ab, w_qkv, w_o, w_up, w_down))

    for layer in range(depth):
        g_pre = norm_mix_pre[layer][None, :]
        g_post = norm_mix_post[layer][None, :]
        if layer % 2 == 0:
            e = layer // 2
            proj = _rms_matmul(h, g_pre, w_in_bf, e, _gelu_epilogue, 2 * A_WIDTH)
            bias_full = jnp.repeat(b_spatial[e].T, A_GROUP_DIM, axis=1)
            h = _mixer(proj, w_spatial[e], bias_full, conv_w[e], w_out_bf, e, g_post, h, seq)
        else:
            o = layer // 2
            qkv = _rms_matmul(h, g_pre, w_qkv_bf, o, _rope_epilogue, 2 * D_MODEL,
                              extra=(cos_tab, sin_tab), extra_specs=rope_specs)
            att = _attention(qkv, batch, seq)
            h = _matmul_rms_res(att, w_o_bf, o, g_post, h)
        h = _mlp(h, norm_mlp_pre[layer][None, :], w_up_bf, w_down_bf, layer,
                 norm_mlp_post[layer][None, :])
    return h.reshape(batch, seq, d)
```

```python
import functools
import math

import jax
import jax.numpy as jnp
import numpy as np
from jax import lax
from jax.experimental import pallas as pl
from jax.experimental.pallas import tpu as pltpu

F32 = jnp.float32
BF16 = jnp.bfloat16

D_MODEL = 2048
A_WIDTH = D_MODEL // 2
B_WIDTH = D_MODEL - A_WIDTH
A_GROUPS = 8
A_GROUP_DIM = A_WIDTH // A_GROUPS
CHUNK = 128
CONV_WIDTH = 3
HEAD_DIM = 128
N_HEADS = D_MODEL // HEAD_DIM
DILATED_BRANCHES = ((128, 1), (512, 4), (2048, 16))
ATTN_BLOCK = 128
ATTN_PERM = 16
ATTN_PERM_STEP = 4
ATTN_GROUP = 32
ROPE_THETA = 10000.0
FFN_DIM = 4 * D_MODEL
RMS_EPS = 1e-6
LN_EPS = 1e-5

V7X_LANES = 128
BF16_SUBLANES = 16
V7X_VMEM_LIMIT_BYTES = 60 * 1024 * 1024

RMS_MATMUL_TM = 1024
RMS_MATMUL_TN = 1024
RMS_MATMUL_SUB_TN = 512
MIXER_TM = 256
RES_TM = 512
MLP_TM = 1024
MLP_TF = 1024
MLP_SUB = 512
CONV_HALO = 8
NORM_ROW_CHUNK = 16
WEIGHT_CAST_STEPS = 64


def _compiler_params(semantics):
    return pltpu.CompilerParams(dimension_semantics=semantics,
                                vmem_limit_bytes=V7X_VMEM_LIMIT_BYTES)


def _rmsnorm(x, g):
    ms = jnp.mean(x * x, axis=-1, keepdims=True)
    return (x * lax.rsqrt(ms + RMS_EPS)) * g


def _for_row_chunks(n_rows, fn):
    for c in range(n_rows // NORM_ROW_CHUNK):
        fn(pl.ds(c * NORM_ROW_CHUNK, NORM_ROW_CHUNK))


def _rms_matmul_kernel(x_ref, g_ref, w_ref, *rest, epilogue, n_special, n_extra, n_cast):
    extra_refs = rest[:n_extra]
    cast_in = rest[n_extra:n_extra + n_cast]
    o_ref = rest[n_extra + n_cast]
    cast_out = rest[n_extra + n_cast + 1:n_extra + 2 * n_cast + 1]
    hn_ref = rest[n_extra + 2 * n_cast + 1]
    j = pl.program_id(1)

    @pl.when(j == 0)
    def _():
        hn_ref[...] = _rmsnorm(x_ref[...], g_ref[...]).astype(BF16)

    sub = RMS_MATMUL_SUB_TN
    cols = [pl.ds(c * sub, sub) for c in range(w_ref.shape[1] // sub)]

    def product(col):
        return jnp.dot(hn_ref[...], w_ref[:, col], preferred_element_type=F32)

    def cast_weight_chunks():
        for src, dst in zip(cast_in, cast_out):
            dst[...] = src[...].astype(BF16)

    @pl.when(j < n_special)
    def _():
        cast_weight_chunks()
        for col in cols:
            epilogue(product(col), extra_refs, o_ref.at[:, col])

    @pl.when(j >= n_special)
    def _():
        cast_weight_chunks()
        for col in cols:
            o_ref[:, col] = product(col)


def _rms_matmul(x, g, w, li, epilogue, special_cols, extra=(), extra_specs=(), cast=(), cast_li=0):
    m, k = x.shape
    n = w.shape[2]
    tm, tn = RMS_MATMUL_TM, RMS_MATMUL_TN
    assert m % tm == 0 and n % tn == 0 and special_cols % tn == 0
    n_col_steps = n // tn
    cast_steps = min(WEIGHT_CAST_STEPS, 2 ** int(math.log2((m // tm) * n_col_steps)))
    kern = functools.partial(_rms_matmul_kernel, epilogue=epilogue, n_special=special_cols // tn,
                             n_extra=len(extra), n_cast=len(cast))
    chunk = lambda i, j: jnp.minimum(i * n_col_steps + j, cast_steps - 1)
    cast_in_specs, cast_out_specs, cast_out_shapes = [], [], []
    for c in cast:
        _, rows, width = c.shape
        assert rows % (cast_steps * BF16_SUBLANES) == 0
        rows_per_step = rows // cast_steps
        cast_in_specs.append(pl.BlockSpec((None, rows_per_step, width),
                                          lambda i, j: (cast_li, chunk(i, j), 0)))
        cast_out_specs.append(pl.BlockSpec((rows_per_step, width), lambda i, j: (chunk(i, j), 0)))
        cast_out_shapes.append(jax.ShapeDtypeStruct((rows, width), BF16))
    outs = pl.pallas_call(
        kern,
        grid=(m // tm, n_col_steps),
        in_specs=[pl.BlockSpec((tm, k), lambda i, j: (i, 0)),
                  pl.BlockSpec((1, k), lambda i, j: (0, 0)),
                  pl.BlockSpec((None, k, tn), lambda i, j: (li, 0, j)),
                  *extra_specs, *cast_in_specs],
        out_specs=[pl.BlockSpec((tm, tn), lambda i, j: (i, j)), *cast_out_specs],
        out_shape=[jax.ShapeDtypeStruct((m, n), F32), *cast_out_shapes],
        scratch_shapes=[pltpu.VMEM((tm, k), BF16)],
        compiler_params=_compiler_params(("arbitrary", "arbitrary")),
        name="rms_matmul",
    )(x, g, w, *extra, *cast)
    return outs[0], outs[1:]


def _gelu_epilogue(p, extra_refs, o_ref):
    del extra_refs
    o_ref[...] = jax.nn.gelu(p)


def _rope_epilogue(p, extra_refs, o_ref):
    cos_ref, sin_ref = extra_refs
    cos = cos_ref[...]
    sin = sin_ref[...]
    for hh in range(p.shape[1] // HEAD_DIM):
        t = p[:, hh * HEAD_DIM:(hh + 1) * HEAD_DIM]
        o_ref[:, hh * HEAD_DIM:(hh + 1) * HEAD_DIM] = (
            t * cos + pltpu.roll(t, HEAD_DIM // 2, 1) * sin)


def _mixer_kernel(au_ref, av_ref, gb_ref, gc_ref, bx_ref, gch_ref, bxh_ref,
                  ws_ref, bias_ref, cw_ref, w_out_ref, g_post_ref, h_ref, o_ref, z_ref,
                  *, tiles_per_seq):
    tm = au_ref.shape[0]
    i = pl.program_id(0)

    av = av_ref[...]
    mu = jnp.mean(av, axis=-1, keepdims=True)
    cen = av - mu
    var = jnp.mean(cen * cen, axis=-1, keepdims=True)
    vn = (cen * lax.rsqrt(var + LN_EPS)).astype(BF16)
    row = lax.broadcasted_iota(jnp.int32, (CHUNK, CHUNK), 0)
    col = lax.broadcasted_iota(jnp.int32, (CHUNK, CHUNK), 1)
    causal = col <= row
    a_cols = []
    for g in range(A_GROUPS):
        cs = slice(g * A_GROUP_DIM, (g + 1) * A_GROUP_DIM)
        w_causal = jnp.where(causal, ws_ref[g], 0.0).astype(BF16)
        chunks = []
        for c in range(tm // CHUNK):
            rs = slice(c * CHUNK, (c + 1) * CHUNK)
            mixed = jnp.dot(w_causal, vn[rs, cs], preferred_element_type=F32) + bias_ref[:, cs]
            chunks.append((au_ref[rs, cs] * mixed).astype(BF16))
        a_cols.append(jnp.concatenate(chunks, axis=0))

    z = gc_ref[...] * bx_ref[...]
    at_seq_start = (i % tiles_per_seq) == 0
    z_halo = jnp.where(at_seq_start, 0.0, gch_ref[...] * bxh_ref[...])
    z_ref[0:CONV_HALO, :] = z_halo
    z_ref[CONV_HALO:CONV_HALO + tm, :] = z
    y = (cw_ref[2:3, :] * z
         + cw_ref[1:2, :] * z_ref[CONV_HALO - 1:CONV_HALO - 1 + tm, :]
         + cw_ref[0:1, :] * z_ref[CONV_HALO - 2:CONV_HALO - 2 + tm, :])
    b_out = (gb_ref[...] * y).astype(BF16)

    mixed_ab = jnp.concatenate(a_cols + [b_out], axis=1)
    f = jnp.dot(mixed_ab, w_out_ref[...], preferred_element_type=F32)
    o_ref[...] = h_ref[...] + _rmsnorm(f, g_post_ref[...])


def _mixer(proj, w_spatial, bias_full, conv_w, w_out, li, g_post, h, seq):
    m = proj.shape[0]
    tm = MIXER_TM
    assert seq % tm == 0 and tm % CHUNK == 0 and A_WIDTH == B_WIDTH
    tiles_per_seq = seq // tm
    halo_blocks = tm // CONV_HALO
    seg = lambda s: pl.BlockSpec((tm, A_WIDTH), lambda i, s=s: (i, s))
    halo = lambda s: pl.BlockSpec(
        (CONV_HALO, A_WIDTH), lambda i, s=s: (jnp.maximum(i * halo_blocks - 1, 0), s))
    kern = functools.partial(_mixer_kernel, tiles_per_seq=tiles_per_seq)
    return pl.pallas_call(
        kern,
        grid=(m // tm,),
        in_specs=[seg(0), seg(1), seg(2), seg(3), seg(4), halo(3), halo(4),
                  pl.BlockSpec((A_GROUPS, CHUNK, CHUNK), lambda i: (0, 0, 0)),
                  pl.BlockSpec((CHUNK, A_WIDTH), lambda i: (0, 0)),
                  pl.BlockSpec((CONV_WIDTH, B_WIDTH), lambda i: (0, 0)),
                  pl.BlockSpec((None, D_MODEL, D_MODEL), lambda i: (li, 0, 0)),
                  pl.BlockSpec((1, D_MODEL), lambda i: (0, 0)),
                  pl.BlockSpec((tm, D_MODEL), lambda i: (i, 0))],
        out_specs=pl.BlockSpec((tm, D_MODEL), lambda i: (i, 0)),
        out_shape=jax.ShapeDtypeStruct((m, D_MODEL), F32),
        scratch_shapes=[pltpu.VMEM((CONV_HALO + tm, B_WIDTH), F32)],
        compiler_params=_compiler_params(("parallel",)),
        name="mixer",
    )(proj, proj, proj, proj, proj, proj, proj, w_spatial, bias_full, conv_w, w_out, g_post, h)


def _attn_band_bias():
    blk = ATTN_BLOCK
    tables = []
    for _, d in DILATED_BRANCHES:
        pieces = ATTN_PERM // d
        rows = blk // pieces
        i = np.arange(blk)
        run = i // rows
        if pieces == ATTN_PERM:
            run = ATTN_PERM_STEP * (run % ATTN_PERM_STEP) + run // ATTN_PERM_STEP
        uq = pieces * (i % rows) + run
        uk = np.concatenate([uq, blk + uq])
        step = uq[:, None] + blk - uk[None, :]
        valid = (step >= 0) & (step <= blk)
        first = valid & (uk[None, :] >= blk)
        tables.append(np.stack([np.where(valid, 0.0, -np.inf), np.where(first, 0.0, -np.inf)]))
    return np.stack(tables).astype(np.float32)


def _attn_block(qb, k_prev, k_cur, v_prev, v_cur, bias, old):
    kw = jnp.concatenate([k_prev, k_cur], axis=0)
    vw = jnp.concatenate([v_prev, v_cur], axis=0)
    s = lax.dot_general(qb, kw, (((1,), (1,)), ((), ())), preferred_element_type=F32) + bias
    m_blk = jnp.max(s, axis=-1, keepdims=True)
    m_new = jnp.broadcast_to(m_blk, (s.shape[0], HEAD_DIM))
    if old is not None:
        m_new = jnp.maximum(old[0], m_new)
    p = jnp.exp2(s - jnp.concatenate([m_new, m_new], axis=1)).astype(BF16)
    v_ones = jnp.concatenate([vw, jnp.ones_like(vw)], axis=1)
    pv = jnp.dot(p, v_ones, preferred_element_type=F32)
    a_new, l_new = pv[:, :HEAD_DIM], pv[:, HEAD_DIM:]
    if old is not None:
        corr = jnp.exp2(old[0] - m_new)
        l_new = old[1] * corr + l_new
        a_new = old[2] * corr + a_new
    return m_new, l_new, a_new


def _attn_kernel(q_ref, k_ref, v_ref, bias_ref, o_ref, qp_ref, kp_ref, vp_ref,
                 qh_ref, kh_ref, vh_ref, acc_ref, m_ref, l_ref, tmp_ref, *, seq, dilations):
    blk = ATTN_BLOCK
    lp = seq // ATTN_PERM
    step = ATTN_PERM_STEP
    lq = seq // step
    q_scale = HEAD_DIM ** -0.5 * math.log2(math.e)
    for src_ref, dst_ref, half_ref, scale in ((q_ref, qp_ref, qh_ref, q_scale),
                                              (k_ref, kp_ref, kh_ref, None),
                                              (v_ref, vp_ref, vh_ref, None)):
        for lo in range(step):
            tmp_ref[lo * lq:(lo + 1) * lq, :] = src_ref[pl.ds(lo, lq, stride=step), :]
        for lo in range(step):
            for hi in range(step):
                run = lo * step + hi
                rows = tmp_ref[pl.ds(lo * lq + hi, lp, stride=step), :]
                if scale is not None:
                    rows = rows * scale
                dst_ref[run * lp:(run + 1) * lp, :] = rows
                half_ref[run * lp:(run + 1) * lp, :] = rows.astype(BF16)

    for bi, d in enumerate(dilations):
        pieces = ATTN_PERM // d
        rows = blk // pieces
        nb = lp // rows
        gn = min(nb, ATTN_GROUP)
        gs = min(ATTN_GROUP // gn, d)
        n_groups = nb // gn
        assert nb % gn == 0 and d % gs == 0
        packed = rows % BF16_SUBLANES == 0
        q_src, k_src, v_src = (qh_ref, kh_ref, vh_ref) if packed else (qp_ref, kp_ref, vp_ref)

        def starts(r_sub, n, d=d, pieces=pieces, rows=rows):
            return [pl.multiple_of((r_sub * pieces + a) * lp + n * rows, rows) for a in range(pieces)]

        def gather(ref, st, rows=rows):
            parts = [ref[pl.ds(s0, rows), :] for s0 in st]
            return parts[0] if len(parts) == 1 else jnp.concatenate(parts, axis=0)

        def gather_half(ref, st):
            return gather(ref, st).astype(BF16)

        def scatter(ref, st, val, rows=rows):
            for a, s0 in enumerate(st):
                ref[pl.ds(s0, rows), :] = val[a * rows:(a + 1) * rows]

        def body(it, carry, bi=bi, gn=gn, gs=gs, n_groups=n_groups,
                 q_src=q_src, k_src=k_src, v_src=v_src):
            sg = it // n_groups
            n0 = (it % n_groups) * gn
            work = []
            for si in range(gs):
                r_sub = sg * gs + si
                st_prev = starts(r_sub, jnp.maximum(n0 - 1, 0))
                kb = [gather_half(k_src, st_prev)]
                vb = [gather_half(v_src, st_prev)]
                for j in range(gn):
                    st = starts(r_sub, n0 + j)
                    kb.append(gather_half(k_src, st))
                    vb.append(gather_half(v_src, st))
                    if j > 0:
                        bias = bias_ref[bi, 0]
                    elif n_groups == 1:
                        bias = bias_ref[bi, 1]
                    else:
                        bias = bias_ref[bi, jnp.where(n0 == 0, 1, 0)]
                    old = None if bi == 0 else (gather(m_ref, st), gather(l_ref, st),
                                                gather(acc_ref, st))
                    work.append((st, gather_half(q_src, st), kb[j], kb[j + 1],
                                 vb[j], vb[j + 1], bias, old))
            results = [(w[0],) + _attn_block(*w[1:]) for w in work]
            for st, m_new, l_new, a_new in results:
                scatter(m_ref, st, m_new)
                scatter(l_ref, st, l_new)
                scatter(acc_ref, st, a_new)
            return carry

        lax.fori_loop(0, (d // gs) * n_groups, body, 0)

    for lo in range(step):
        for hi in range(step):
            src = slice((lo * step + hi) * lp, (lo * step + hi + 1) * lp)
            tmp_ref[pl.ds(lo * lq + hi, lp, stride=step), :] = acc_ref[src, :] / l_ref[src, :]
    for lo in range(step):
        o_ref[pl.ds(lo, lq, stride=step), :] = tmp_ref[lo * lq:(lo + 1) * lq, :]


def _attention(qkv, batch, seq):
    dilations = tuple(d for _, d in DILATED_BRANCHES)
    for window, d in DILATED_BRANCHES:
        assert window // d == ATTN_BLOCK and seq % (d * ATTN_BLOCK) == 0
        assert ATTN_PERM % d == 0 and ATTN_BLOCK % (ATTN_PERM // d) == 0
    qkv3 = qkv.reshape(batch, seq, 3 * D_MODEL)
    bias = jnp.asarray(_attn_band_bias())
    spec = lambda part: pl.BlockSpec((None, seq, HEAD_DIM),
                                     lambda b, h, part=part: (b, 0, part * N_HEADS + h))
    kern = functools.partial(_attn_kernel, seq=seq, dilations=dilations)
    out = pl.pallas_call(
        kern,
        grid=(batch, N_HEADS),
        in_specs=[spec(0), spec(1), spec(2),
                  pl.BlockSpec(bias.shape, lambda b, h: (0, 0, 0, 0))],
        out_specs=pl.BlockSpec((None, seq, HEAD_DIM), lambda b, h: (b, 0, h)),
        out_shape=jax.ShapeDtypeStruct((batch, seq, D_MODEL), F32),
        scratch_shapes=[pltpu.VMEM((seq, HEAD_DIM), F32)] * 3 + [pltpu.VMEM((seq, HEAD_DIM), BF16)] * 3
                       + [pltpu.VMEM((seq, HEAD_DIM), F32)] * 4,
        compiler_params=_compiler_params(("parallel", "parallel")),
        name="attention",
    )(qkv3, qkv3, qkv3, bias)
    return out.reshape(batch * seq, D_MODEL)


def _matmul_rms_res_kernel(a_ref, w_ref, g_ref, h_ref, o_ref):
    f = jnp.dot(a_ref[...].astype(BF16), w_ref[...], preferred_element_type=F32)
    o_ref[...] = h_ref[...] + _rmsnorm(f, g_ref[...])


def _matmul_rms_res(a, w, li, g, h):
    m, k = a.shape
    n = w.shape[2]
    tm = RES_TM
    assert m % tm == 0
    return pl.pallas_call(
        _matmul_rms_res_kernel,
        grid=(m // tm,),
        in_specs=[pl.BlockSpec((tm, k), lambda i: (i, 0)),
                  pl.BlockSpec((None, k, n), lambda i: (li, 0, 0)),
                  pl.BlockSpec((1, n), lambda i: (0, 0)),
                  pl.BlockSpec((tm, n), lambda i: (i, 0))],
        out_specs=pl.BlockSpec((tm, n), lambda i: (i, 0)),
        out_shape=jax.ShapeDtypeStruct((m, n), F32),
        compiler_params=_compiler_params(("parallel",)),
        name="matmul_rms_res",
    )(a, w, g, h)


def _mlp_kernel(x_ref, g_pre_ref, w_up_ref, w_down_ref, g_post_ref, o_ref, hn_ref):
    k = pl.program_id(1)

    @pl.when(k == 0)
    def _():
        hn_ref[...] = _rmsnorm(x_ref[...], g_pre_ref[...]).astype(BF16)
        o_ref[...] = jnp.zeros_like(o_ref)

    sub = MLP_SUB
    u = [jnp.dot(hn_ref[...], w_up_ref[:, pl.ds(c * sub, sub)], preferred_element_type=F32)
         for c in range(w_up_ref.shape[1] // sub)]
    u = jnp.concatenate([jnp.square(jnp.maximum(t, 0.0)).astype(BF16) for t in u], axis=1)
    for n in range(o_ref.shape[1] // sub):
        cols = pl.ds(n * sub, sub)
        o_ref[:, cols] += jnp.dot(u, w_down_ref[:, cols], preferred_element_type=F32)

    @pl.when(k == pl.num_programs(1) - 1)
    def _():
        def finish_rows(rows):
            o_ref[rows, :] = x_ref[rows, :] + _rmsnorm(o_ref[rows, :], g_post_ref[...])
        _for_row_chunks(o_ref.shape[0], finish_rows)


def _mlp(h, g_pre, w_up, w_down, g_post):
    m, d = h.shape
    f = w_up.shape[1]
    tm, tf = MLP_TM, MLP_TF
    assert m % tm == 0 and f % tf == 0
    return pl.pallas_call(
        _mlp_kernel,
        grid=(m // tm, f // tf),
        in_specs=[pl.BlockSpec((tm, d), lambda i, k: (i, 0), pipeline_mode=pl.Buffered(1)),
                  pl.BlockSpec((1, d), lambda i, k: (0, 0)),
                  pl.BlockSpec((d, tf), lambda i, k: (0, k)),
                  pl.BlockSpec((tf, d), lambda i, k: (k, 0)),
                  pl.BlockSpec((1, d), lambda i, k: (0, 0))],
        out_specs=pl.BlockSpec((tm, d), lambda i, k: (i, 0)),
        out_shape=jax.ShapeDtypeStruct((m, d), F32),
        scratch_shapes=[pltpu.VMEM((tm, d), BF16)],
        compiler_params=_compiler_params(("parallel", "arbitrary")),
        name="mlp",
    )(h, g_pre, w_up, w_down, g_post)


def _rope_tables(seq):
    half = HEAD_DIM // 2
    inv_freq = ROPE_THETA ** (-jnp.arange(half, dtype=F32) * 2.0 / HEAD_DIM)
    ang = jnp.arange(seq, dtype=jnp.int32).astype(F32)[:, None] * inv_freq[None, :]
    cos = jnp.cos(ang)
    sin = jnp.sin(ang)
    return jnp.concatenate([cos, cos], axis=-1), jnp.concatenate([-sin, sin], axis=-1)


def kernel(x, norm_mix_pre, norm_mix_post, norm_mlp_pre, norm_mlp_post, w_in_ab, w_spatial,
           b_spatial, conv_w, w_out_ab, w_qkv, w_o, w_up, w_down):
    batch, seq, d = x.shape
    assert d == D_MODEL
    depth = norm_mix_pre.shape[0]
    m = batch * seq
    h = x.reshape(m, d)
    cos_tab, sin_tab = _rope_tables(seq)
    tiles_per_seq = seq // RMS_MATMUL_TM
    rope_specs = [pl.BlockSpec((RMS_MATMUL_TM, HEAD_DIM), lambda i, j: (i % tiles_per_seq, 0))] * 2

    w_in_bf, w_out_bf, w_qkv_bf, w_o_bf = (w.astype(BF16) for w in (w_in_ab, w_out_ab, w_qkv, w_o))

    for layer in range(depth):
        g_pre = norm_mix_pre[layer][None, :]
        g_post = norm_mix_post[layer][None, :]
        if layer % 2 == 0:
            e = layer // 2
            proj, (w_up_bf, w_down_bf) = _rms_matmul(h, g_pre, w_in_bf, e, _gelu_epilogue, 2 * A_WIDTH,
                                                      cast=(w_up, w_down), cast_li=layer)
            bias_full = jnp.repeat(b_spatial[e].T, A_GROUP_DIM, axis=1)
            h = _mixer(proj, w_spatial[e], bias_full, conv_w[e], w_out_bf, e, g_post, h, seq)
        else:
            o = layer // 2
            qkv, (w_up_bf, w_down_bf) = _rms_matmul(h, g_pre, w_qkv_bf, o, _rope_epilogue, 2 * D_MODEL,
                                                     extra=(cos_tab, sin_tab), extra_specs=rope_specs,
                                                     cast=(w_up, w_down), cast_li=layer)
            att = _attention(qkv, batch, seq)
            h = _matmul_rms_res(att, w_o_bf, o, g_post, h)
        h = _mlp(h, norm_mlp_pre[layer][None, :], w_up_bf, w_down_bf, norm_mlp_post[layer][None, :])
    return h.reshape(batch, seq, d)
```

```python
import functools
import math

import jax
import jax.numpy as jnp
import numpy as np
from jax import lax
from jax.experimental import pallas as pl
from jax.experimental.pallas import tpu as pltpu

F32 = jnp.float32
BF16 = jnp.bfloat16

D_MODEL = 2048
A_WIDTH = D_MODEL // 2
B_WIDTH = D_MODEL - A_WIDTH
A_GROUPS = 8
A_GROUP_DIM = A_WIDTH // A_GROUPS
CHUNK = 128
CONV_WIDTH = 3
HEAD_DIM = 128
N_HEADS = D_MODEL // HEAD_DIM
DILATED_BRANCHES = ((128, 1), (512, 4), (2048, 16))
ATTN_BLOCK = 128
ATTN_PERM = 16
ATTN_PERM_STEP = 4
ATTN_GROUP = 32
ROPE_THETA = 10000.0
FFN_DIM = 4 * D_MODEL
RMS_EPS = 1e-6
LN_EPS = 1e-5

V7X_LANES = 128
BF16_SUBLANES = 16
V7X_VMEM_LIMIT_BYTES = 60 * 1024 * 1024

RMS_MATMUL_TM = 1024
RMS_MATMUL_TN = 1024
RMS_MATMUL_SUB_TN = 512
MIXER_TM = 256
RES_TM = 512
MLP_TM = 1024
MLP_TF = 1024
MLP_SUB = 512
CONV_HALO = 8
NORM_ROW_CHUNK = 16
WEIGHT_CAST_STEPS = 64


def _compiler_params(semantics):
    return pltpu.CompilerParams(dimension_semantics=semantics,
                                vmem_limit_bytes=V7X_VMEM_LIMIT_BYTES)


def _rmsnorm(x, g):
    ms = jnp.mean(x * x, axis=-1, keepdims=True)
    return (x * lax.rsqrt(ms + RMS_EPS)) * g


def _for_row_chunks(n_rows, fn):
    for c in range(n_rows // NORM_ROW_CHUNK):
        fn(pl.ds(c * NORM_ROW_CHUNK, NORM_ROW_CHUNK))


def _rms_matmul_kernel(x_ref, g_ref, w_ref, *rest, epilogue, n_special, n_extra, n_cast):
    extra_refs = rest[:n_extra]
    cast_in = rest[n_extra:n_extra + n_cast]
    o_ref = rest[n_extra + n_cast]
    cast_out = rest[n_extra + n_cast + 1:n_extra + 2 * n_cast + 1]
    hn_ref = rest[n_extra + 2 * n_cast + 1]
    j = pl.program_id(1)

    @pl.when(j == 0)
    def _():
        hn_ref[...] = _rmsnorm(x_ref[...], g_ref[...]).astype(BF16)

    sub = RMS_MATMUL_SUB_TN
    cols = [pl.ds(c * sub, sub) for c in range(w_ref.shape[1] // sub)]

    def product(col):
        return jnp.dot(hn_ref[...], w_ref[:, col], preferred_element_type=F32)

    def cast_weight_chunks():
        for src, dst in zip(cast_in, cast_out):
            dst[...] = src[...].astype(BF16)

    @pl.when(j < n_special)
    def _():
        cast_weight_chunks()
        for col in cols:
            epilogue(product(col), extra_refs, o_ref.at[:, col])

    @pl.when(j >= n_special)
    def _():
        cast_weight_chunks()
        for col in cols:
            o_ref[:, col] = product(col)


def _rms_matmul(x, g, w, li, epilogue, special_cols, extra=(), extra_specs=(), cast=(), cast_li=0):
    m, k = x.shape
    n = w.shape[2]
    tm, tn = RMS_MATMUL_TM, RMS_MATMUL_TN
    assert m % tm == 0 and n % tn == 0 and special_cols % tn == 0
    n_col_steps = n // tn
    cast_steps = min(WEIGHT_CAST_STEPS, 2 ** int(math.log2((m // tm) * n_col_steps)))
    kern = functools.partial(_rms_matmul_kernel, epilogue=epilogue, n_special=special_cols // tn,
                             n_extra=len(extra), n_cast=len(cast))
    chunk = lambda i, j: jnp.minimum(i * n_col_steps + j, cast_steps - 1)
    cast_in_specs, cast_out_specs, cast_out_shapes = [], [], []
    for c in cast:
        _, rows, width = c.shape
        assert rows % (cast_steps * BF16_SUBLANES) == 0
        rows_per_step = rows // cast_steps
        cast_in_specs.append(pl.BlockSpec((None, rows_per_step, width),
                                          lambda i, j: (cast_li, chunk(i, j), 0)))
        cast_out_specs.append(pl.BlockSpec((rows_per_step, width), lambda i, j: (chunk(i, j), 0)))
        cast_out_shapes.append(jax.ShapeDtypeStruct((rows, width), BF16))
    outs = pl.pallas_call(
        kern,
        grid=(m // tm, n_col_steps),
        in_specs=[pl.BlockSpec((tm, k), lambda i, j: (i, 0)),
                  pl.BlockSpec((1, k), lambda i, j: (0, 0)),
                  pl.BlockSpec((None, k, tn), lambda i, j: (li, 0, j)),
                  *extra_specs, *cast_in_specs],
        out_specs=[pl.BlockSpec((tm, tn), lambda i, j: (i, j)), *cast_out_specs],
        out_shape=[jax.ShapeDtypeStruct((m, n), F32), *cast_out_shapes],
        scratch_shapes=[pltpu.VMEM((tm, k), BF16)],
        compiler_params=_compiler_params(("arbitrary", "arbitrary")),
        name="rms_matmul",
    )(x, g, w, *extra, *cast)
    return outs[0], outs[1:]


def _gelu_epilogue(p, extra_refs, o_ref):
    del extra_refs
    o_ref[...] = jax.nn.gelu(p)


def _rope_epilogue(p, extra_refs, o_ref):
    cos_ref, sin_ref = extra_refs
    cos = cos_ref[...]
    sin = sin_ref[...]
    for hh in range(p.shape[1] // HEAD_DIM):
        t = p[:, hh * HEAD_DIM:(hh + 1) * HEAD_DIM]
        o_ref[:, hh * HEAD_DIM:(hh + 1) * HEAD_DIM] = (
            t * cos + pltpu.roll(t, HEAD_DIM // 2, 1) * sin)


def _mixer_kernel(au_ref, av_ref, gb_ref, gc_ref, bx_ref, gch_ref, bxh_ref,
                  ws_ref, bias_ref, cw_ref, w_out_ref, g_post_ref, h_ref, o_ref, z_ref,
                  *, tiles_per_seq):
    tm = au_ref.shape[0]
    i = pl.program_id(0)

    av = av_ref[...]
    mu = jnp.mean(av, axis=-1, keepdims=True)
    cen = av - mu
    var = jnp.mean(cen * cen, axis=-1, keepdims=True)
    vn = (cen * lax.rsqrt(var + LN_EPS)).astype(BF16)
    row = lax.broadcasted_iota(jnp.int32, (CHUNK, CHUNK), 0)
    col = lax.broadcasted_iota(jnp.int32, (CHUNK, CHUNK), 1)
    causal = col <= row
    a_cols = []
    for g in range(A_GROUPS):
        cs = slice(g * A_GROUP_DIM, (g + 1) * A_GROUP_DIM)
        w_causal = jnp.where(causal, ws_ref[g], 0.0).astype(BF16)
        chunks = []
        for c in range(tm // CHUNK):
            rs = slice(c * CHUNK, (c + 1) * CHUNK)
            mixed = jnp.dot(w_causal, vn[rs, cs], preferred_element_type=F32) + bias_ref[:, cs]
            chunks.append((au_ref[rs, cs] * mixed).astype(BF16))
        a_cols.append(jnp.concatenate(chunks, axis=0))

    z = gc_ref[...] * bx_ref[...]
    at_seq_start = (i % tiles_per_seq) == 0
    z_halo = jnp.where(at_seq_start, 0.0, gch_ref[...] * bxh_ref[...])
    z_ref[0:CONV_HALO, :] = z_halo
    z_ref[CONV_HALO:CONV_HALO + tm, :] = z
    y = (cw_ref[2:3, :] * z
         + cw_ref[1:2, :] * z_ref[CONV_HALO - 1:CONV_HALO - 1 + tm, :]
         + cw_ref[0:1, :] * z_ref[CONV_HALO - 2:CONV_HALO - 2 + tm, :])
    b_out = (gb_ref[...] * y).astype(BF16)

    mixed_ab = jnp.concatenate(a_cols + [b_out], axis=1)
    f = jnp.dot(mixed_ab, w_out_ref[...], preferred_element_type=F32)
    o_ref[...] = h_ref[...] + _rmsnorm(f, g_post_ref[...])


def _mixer(proj, w_spatial, bias_full, conv_w, w_out, li, g_post, h, seq):
    m = proj.shape[0]
    tm = MIXER_TM
    assert seq % tm == 0 and tm % CHUNK == 0 and A_WIDTH == B_WIDTH
    tiles_per_seq = seq // tm
    halo_blocks = tm // CONV_HALO
    seg = lambda s: pl.BlockSpec((tm, A_WIDTH), lambda i, s=s: (i, s))
    halo = lambda s: pl.BlockSpec(
        (CONV_HALO, A_WIDTH), lambda i, s=s: (jnp.maximum(i * halo_blocks - 1, 0), s))
    kern = functools.partial(_mixer_kernel, tiles_per_seq=tiles_per_seq)
    return pl.pallas_call(
        kern,
        grid=(m // tm,),
        in_specs=[seg(0), seg(1), seg(2), seg(3), seg(4), halo(3), halo(4),
                  pl.BlockSpec((A_GROUPS, CHUNK, CHUNK), lambda i: (0, 0, 0)),
                  pl.BlockSpec((CHUNK, A_WIDTH), lambda i: (0, 0)),
                  pl.BlockSpec((CONV_WIDTH, B_WIDTH), lambda i: (0, 0)),
                  pl.BlockSpec((None, D_MODEL, D_MODEL), lambda i: (li, 0, 0)),
                  pl.BlockSpec((1, D_MODEL), lambda i: (0, 0)),
                  pl.BlockSpec((tm, D_MODEL), lambda i: (i, 0))],
        out_specs=pl.BlockSpec((tm, D_MODEL), lambda i: (i, 0)),
        out_shape=jax.ShapeDtypeStruct((m, D_MODEL), F32),
        scratch_shapes=[pltpu.VMEM((CONV_HALO + tm, B_WIDTH), F32)],
        compiler_params=_compiler_params(("parallel",)),
        name="mixer",
    )(proj, proj, proj, proj, proj, proj, proj, w_spatial, bias_full, conv_w, w_out, g_post, h)


def _attn_band_bias():
    blk = ATTN_BLOCK
    tables = []
    for _, d in DILATED_BRANCHES:
        pieces = ATTN_PERM // d
        rows = blk // pieces
        i = np.arange(blk)
        run = i // rows
        if pieces == ATTN_PERM:
            run = ATTN_PERM_STEP * (run % ATTN_PERM_STEP) + run // ATTN_PERM_STEP
        uq = pieces * (i % rows) + run
        uk = np.concatenate([uq, blk + uq])
        step = uq[:, None] + blk - uk[None, :]
        valid = (step >= 0) & (step <= blk)
        first = valid & (uk[None, :] >= blk)
        tables.append(np.stack([np.where(valid, 0.0, -np.inf), np.where(first, 0.0, -np.inf)]))
    return np.stack(tables).astype(np.float32)


def _attn_block(qb, k_prev, k_cur, v_prev, v_cur, bias, old):
    kw = jnp.concatenate([k_prev, k_cur], axis=0)
    vw = jnp.concatenate([v_prev, v_cur], axis=0)
    s = lax.dot_general(qb, kw, (((1,), (1,)), ((), ())), preferred_element_type=F32) + bias
    m_blk = jnp.max(s, axis=-1, keepdims=True)
    m_new = jnp.broadcast_to(m_blk, (s.shape[0], HEAD_DIM))
    if old is not None:
        m_new = jnp.maximum(old[0], m_new)
    p = jnp.exp2(s - jnp.concatenate([m_new, m_new], axis=1)).astype(BF16)
    v_ones = jnp.concatenate([vw, jnp.ones_like(vw)], axis=1)
    pv = jnp.dot(p, v_ones, preferred_element_type=F32)
    a_new, l_new = pv[:, :HEAD_DIM], pv[:, HEAD_DIM:]
    if old is not None:
        corr = jnp.exp2(old[0] - m_new)
        l_new = old[1] * corr + l_new
        a_new = old[2] * corr + a_new
    return m_new, l_new, a_new


def _attn_kernel(q_ref, k_ref, v_ref, bias_ref, o_ref, qp_ref, kp_ref, vp_ref,
                 qh_ref, kh_ref, vh_ref, acc_ref, m_ref, l_ref, tmp_ref, *, seq, dilations):
    blk = ATTN_BLOCK
    lp = seq // ATTN_PERM
    step = ATTN_PERM_STEP
    lq = seq // step
    q_scale = HEAD_DIM ** -0.5 * math.log2(math.e)
    for src_ref, dst_ref, half_ref, scale in ((q_ref, qp_ref, qh_ref, q_scale),
                                              (k_ref, kp_ref, kh_ref, None),
                                              (v_ref, vp_ref, vh_ref, None)):
        for lo in range(step):
            tmp_ref[lo * lq:(lo + 1) * lq, :] = src_ref[pl.ds(lo, lq, stride=step), :]
        for lo in range(step):
            for hi in range(step):
                run = lo * step + hi
                rows = tmp_ref[pl.ds(lo * lq + hi, lp, stride=step), :]
                if scale is not None:
                    rows = rows * scale
                dst_ref[run * lp:(run + 1) * lp, :] = rows
                half_ref[run * lp:(run + 1) * lp, :] = rows.astype(BF16)

    for bi, d in enumerate(dilations):
        pieces = ATTN_PERM // d
        rows = blk // pieces
        nb = lp // rows
        gn = min(nb, ATTN_GROUP)
        gs = min(ATTN_GROUP // gn, d)
        n_groups = nb // gn
        assert nb % gn == 0 and d % gs == 0
        packed = rows % BF16_SUBLANES == 0
        q_src, k_src, v_src = (qh_ref, kh_ref, vh_ref) if packed else (qp_ref, kp_ref, vp_ref)

        def starts(r_sub, n, d=d, pieces=pieces, rows=rows):
            return [pl.multiple_of((r_sub * pieces + a) * lp + n * rows, rows) for a in range(pieces)]

        def gather(ref, st, rows=rows):
            parts = [ref[pl.ds(s0, rows), :] for s0 in st]
            return parts[0] if len(parts) == 1 else jnp.concatenate(parts, axis=0)

        def gather_half(ref, st):
            return gather(ref, st).astype(BF16)

        def scatter(ref, st, val, rows=rows):
            for a, s0 in enumerate(st):
                ref[pl.ds(s0, rows), :] = val[a * rows:(a + 1) * rows]

        def body(it, carry, bi=bi, gn=gn, gs=gs, n_groups=n_groups,
                 q_src=q_src, k_src=k_src, v_src=v_src):
            sg = it // n_groups
            n0 = (it % n_groups) * gn
            work = []
            for si in range(gs):
                r_sub = sg * gs + si
                st_prev = starts(r_sub, jnp.maximum(n0 - 1, 0))
                kb = [gather_half(k_src, st_prev)]
                vb = [gather_half(v_src, st_prev)]
                for j in range(gn):
                    st = starts(r_sub, n0 + j)
                    kb.append(gather_half(k_src, st))
                    vb.append(gather_half(v_src, st))
                    if j > 0:
                        bias = bias_ref[bi, 0]
                    elif n_groups == 1:
                        bias = bias_ref[bi, 1]
                    else:
                        bias = bias_ref[bi, jnp.where(n0 == 0, 1, 0)]
                    old = None if bi == 0 else (gather(m_ref, st), gather(l_ref, st),
                                                gather(acc_ref, st))
                    work.append((st, gather_half(q_src, st), kb[j], kb[j + 1],
                                 vb[j], vb[j + 1], bias, old))
            results = [(w[0],) + _attn_block(*w[1:]) for w in work]
            for st, m_new, l_new, a_new in results:
                scatter(m_ref, st, m_new)
                scatter(l_ref, st, l_new)
                scatter(acc_ref, st, a_new)
            return carry

        lax.fori_loop(0, (d // gs) * n_groups, body, 0)

    for lo in range(step):
        for hi in range(step):
            src = slice((lo * step + hi) * lp, (lo * step + hi + 1) * lp)
            tmp_ref[pl.ds(lo * lq + hi, lp, stride=step), :] = acc_ref[src, :] / l_ref[src, :]
    for lo in range(step):
        o_ref[pl.ds(lo, lq, stride=step), :] = tmp_ref[lo * lq:(lo + 1) * lq, :]


def _attention(qkv, batch, seq):
    dilations = tuple(d for _, d in DILATED_BRANCHES)
    for window, d in DILATED_BRANCHES:
        assert window // d == ATTN_BLOCK and seq % (d * ATTN_BLOCK) == 0
        assert ATTN_PERM % d == 0 and ATTN_BLOCK % (ATTN_PERM // d) == 0
    qkv3 = qkv.reshape(batch, seq, 3 * D_MODEL)
    bias = jnp.asarray(_attn_band_bias())
    spec = lambda part: pl.BlockSpec((None, seq, HEAD_DIM),
                                     lambda b, h, part=part: (b, 0, part * N_HEADS + h))
    kern = functools.partial(_attn_kernel, seq=seq, dilations=dilations)
    out = pl.pallas_call(
        kern,
        grid=(batch, N_HEADS),
        in_specs=[spec(0), spec(1), spec(2),
                  pl.BlockSpec(bias.shape, lambda b, h: (0, 0, 0, 0))],
        out_specs=pl.BlockSpec((None, seq, HEAD_DIM), lambda b, h: (b, 0, h)),
        out_shape=jax.ShapeDtypeStruct((batch, seq, D_MODEL), F32),
        scratch_shapes=[pltpu.VMEM((seq, HEAD_DIM), F32)] * 3 + [pltpu.VMEM((seq, HEAD_DIM), BF16)] * 3
                       + [pltpu.VMEM((seq, HEAD_DIM), F32)] * 4,
        compiler_params=_compiler_params(("parallel", "parallel")),
        name="attention",
    )(qkv3, qkv3, qkv3, bias)
    return out.reshape(batch * seq, D_MODEL)


def _matmul_rms_res_kernel(a_ref, w_ref, g_ref, h_ref, o_ref):
    f = jnp.dot(a_ref[...].astype(BF16), w_ref[...], preferred_element_type=F32)
    o_ref[...] = h_ref[...] + _rmsnorm(f, g_ref[...])


def _matmul_rms_res(a, w, li, g, h):
    m, k = a.shape
    n = w.shape[2]
    tm = RES_TM
    assert m % tm == 0
    return pl.pallas_call(
        _matmul_rms_res_kernel,
        grid=(m // tm,),
        in_specs=[pl.BlockSpec((tm, k), lambda i: (i, 0)),
                  pl.BlockSpec((None, k, n), lambda i: (li, 0, 0)),
                  pl.BlockSpec((1, n), lambda i: (0, 0)),
                  pl.BlockSpec((tm, n), lambda i: (i, 0))],
        out_specs=pl.BlockSpec((tm, n), lambda i: (i, 0)),
        out_shape=jax.ShapeDtypeStruct((m, n), F32),
        compiler_params=_compiler_params(("parallel",)),
        name="matmul_rms_res",
    )(a, w, g, h)


def _mlp_kernel(x_ref, g_pre_ref, w_up_ref, w_down_ref, g_post_ref, o_ref, hn_ref):
    k = pl.program_id(1)

    @pl.when(k == 0)
    def _():
        hn_ref[...] = _rmsnorm(x_ref[...], g_pre_ref[...]).astype(BF16)
        o_ref[...] = jnp.zeros_like(o_ref)

    sub = MLP_SUB
    for c in range(w_up_ref.shape[1] // sub):
        ffn = pl.ds(c * sub, sub)
        u = jnp.dot(hn_ref[...], w_up_ref[:, ffn], preferred_element_type=F32)
        u = jnp.square(jnp.maximum(u, 0.0)).astype(BF16)
        for n in range(o_ref.shape[1] // sub):
            cols = pl.ds(n * sub, sub)
            o_ref[:, cols] += jnp.dot(u, w_down_ref[ffn, cols], preferred_element_type=F32)

    @pl.when(k == pl.num_programs(1) - 1)
    def _():
        def finish_rows(rows):
            o_ref[rows, :] = x_ref[rows, :] + _rmsnorm(o_ref[rows, :], g_post_ref[...])
        _for_row_chunks(o_ref.shape[0], finish_rows)


def _mlp(h, g_pre, w_up, w_down, g_post):
    m, d = h.shape
    f = w_up.shape[1]
    tm, tf = MLP_TM, MLP_TF
    assert m % tm == 0 and f % tf == 0
    return pl.pallas_call(
        _mlp_kernel,
        grid=(m // tm, f // tf),
        in_specs=[pl.BlockSpec((tm, d), lambda i, k: (i, 0)),
                  pl.BlockSpec((1, d), lambda i, k: (0, 0)),
                  pl.BlockSpec((d, tf), lambda i, k: (0, k)),
                  pl.BlockSpec((tf, d), lambda i, k: (k, 0)),
                  pl.BlockSpec((1, d), lambda i, k: (0, 0))],
        out_specs=pl.BlockSpec((tm, d), lambda i, k: (i, 0)),
        out_shape=jax.ShapeDtypeStruct((m, d), F32),
        scratch_shapes=[pltpu.VMEM((tm, d), BF16)],
        compiler_params=_compiler_params(("parallel", "arbitrary")),
        name="mlp",
    )(h, g_pre, w_up, w_down, g_post)


def _rope_tables(seq):
    half = HEAD_DIM // 2
    inv_freq = ROPE_THETA ** (-jnp.arange(half, dtype=F32) * 2.0 / HEAD_DIM)
    ang = jnp.arange(seq, dtype=jnp.int32).astype(F32)[:, None] * inv_freq[None, :]
    cos = jnp.cos(ang)
    sin = jnp.sin(ang)
    return jnp.concatenate([cos, cos], axis=-1), jnp.concatenate([-sin, sin], axis=-1)


def kernel(x, norm_mix_pre, norm_mix_post, norm_mlp_pre, norm_mlp_post, w_in_ab, w_spatial,
           b_spatial, conv_w, w_out_ab, w_qkv, w_o, w_up, w_down):
    batch, seq, d = x.shape
    assert d == D_MODEL
    depth = norm_mix_pre.shape[0]
    m = batch * seq
    h = x.reshape(m, d)
    cos_tab, sin_tab = _rope_tables(seq)
    tiles_per_seq = seq // RMS_MATMUL_TM
    rope_specs = [pl.BlockSpec((RMS_MATMUL_TM, HEAD_DIM), lambda i, j: (i % tiles_per_seq, 0))] * 2

    w_in_bf, w_out_bf, w_qkv_bf, w_o_bf = (w.astype(BF16) for w in (w_in_ab, w_out_ab, w_qkv, w_o))

    for layer in range(depth):
        g_pre = norm_mix_pre[layer][None, :]
        g_post = norm_mix_post[layer][None, :]
        if layer % 2 == 0:
            e = layer // 2
            proj, (w_up_bf, w_down_bf) = _rms_matmul(h, g_pre, w_in_bf, e, _gelu_epilogue, 2 * A_WIDTH,
                                                      cast=(w_up, w_down), cast_li=layer)
            bias_full = jnp.repeat(b_spatial[e].T, A_GROUP_DIM, axis=1)
            h = _mixer(proj, w_spatial[e], bias_full, conv_w[e], w_out_bf, e, g_post, h, seq)
        else:
            o = layer // 2
            qkv, (w_up_bf, w_down_bf) = _rms_matmul(h, g_pre, w_qkv_bf, o, _rope_epilogue, 2 * D_MODEL,
                                                     extra=(cos_tab, sin_tab), extra_specs=rope_specs,
                                                     cast=(w_up, w_down), cast_li=layer)
            att = _attention(qkv, batch, seq)
            h = _matmul_rms_res(att, w_o_bf, o, g_post, h)
        h = _mlp(h, norm_mlp_pre[layer][None, :], w_up_bf, w_down_bf, norm_mlp_post[layer][None, :])
    return h.reshape(batch, seq, d)
```

```python
import functools
import math

import jax
import jax.numpy as jnp
import numpy as np
from jax import lax
from jax.experimental import pallas as pl
from jax.experimental.pallas import tpu as pltpu

F32 = jnp.float32
BF16 = jnp.bfloat16

D_MODEL = 2048
A_WIDTH = D_MODEL // 2
B_WIDTH = D_MODEL - A_WIDTH
A_GROUPS = 8
A_GROUP_DIM = A_WIDTH // A_GROUPS
CHUNK = 128
CONV_WIDTH = 3
HEAD_DIM = 128
N_HEADS = D_MODEL // HEAD_DIM
DILATED_BRANCHES = ((128, 1), (512, 4), (2048, 16))
ATTN_BLOCK = 128
ATTN_PERM = 16
ATTN_PERM_STEP = 4
ATTN_GROUP = 32
ROPE_THETA = 10000.0
FFN_DIM = 4 * D_MODEL
RMS_EPS = 1e-6
LN_EPS = 1e-5

V7X_LANES = 128
BF16_SUBLANES = 16
V7X_VMEM_LIMIT_BYTES = 60 * 1024 * 1024

RMS_MATMUL_TM = 1024
RMS_MATMUL_TN = 1024
RMS_MATMUL_SUB_TN = 512
MIXER_TM = 256
RES_TM = 512
MLP_TM = 1024
MLP_TF = 1024
MLP_SUB = 512
CONV_HALO = 8
NORM_ROW_CHUNK = 16
RMS_MATMUL_CAST_STEPS = 64
MLP_CAST_STEPS = 128


def _compiler_params(semantics):
    return pltpu.CompilerParams(dimension_semantics=semantics,
                                vmem_limit_bytes=V7X_VMEM_LIMIT_BYTES)


def _rmsnorm(x, g):
    ms = jnp.mean(x * x, axis=-1, keepdims=True)
    return (x * lax.rsqrt(ms + RMS_EPS)) * g


def _for_row_chunks(n_rows, fn):
    for c in range(n_rows // NORM_ROW_CHUNK):
        fn(pl.ds(c * NORM_ROW_CHUNK, NORM_ROW_CHUNK))


def _weight_cast_plan(cast, cast_li, n_outer, n_inner, max_steps):
    steps = min(max_steps, 2 ** int(math.log2(n_outer * n_inner)))
    chunk = lambda i, j: jnp.minimum(i * n_inner + j, steps - 1)
    in_specs, out_specs, out_shapes = [], [], []
    for c in cast:
        _, rows, width = c.shape
        assert rows % (steps * BF16_SUBLANES) == 0
        rows_per_step = rows // steps
        in_specs.append(pl.BlockSpec((None, rows_per_step, width),
                                     lambda i, j: (cast_li, chunk(i, j), 0)))
        out_specs.append(pl.BlockSpec((rows_per_step, width), lambda i, j: (chunk(i, j), 0)))
        out_shapes.append(jax.ShapeDtypeStruct((rows, width), BF16))
    return in_specs, out_specs, out_shapes


def _cast_chunks(cast_in, cast_out):
    for src, dst in zip(cast_in, cast_out):
        dst[...] = src[...].astype(BF16)


def _rms_matmul_kernel(x_ref, g_ref, w_ref, *rest, epilogue, n_special, n_extra, n_cast):
    extra_refs = rest[:n_extra]
    cast_in = rest[n_extra:n_extra + n_cast]
    o_ref = rest[n_extra + n_cast]
    cast_out = rest[n_extra + n_cast + 1:n_extra + 2 * n_cast + 1]
    hn_ref = rest[n_extra + 2 * n_cast + 1]
    j = pl.program_id(1)

    @pl.when(j == 0)
    def _():
        hn_ref[...] = _rmsnorm(x_ref[...], g_ref[...]).astype(BF16)

    sub = RMS_MATMUL_SUB_TN
    cols = [pl.ds(c * sub, sub) for c in range(w_ref.shape[1] // sub)]

    def product(col):
        return jnp.dot(hn_ref[...], w_ref[:, col], preferred_element_type=F32)

    def cast_weight_chunks():
        _cast_chunks(cast_in, cast_out)

    @pl.when(j < n_special)
    def _():
        cast_weight_chunks()
        for col in cols:
            epilogue(product(col), extra_refs, o_ref.at[:, col])

    @pl.when(j >= n_special)
    def _():
        cast_weight_chunks()
        for col in cols:
            o_ref[:, col] = product(col)


def _rms_matmul(x, g, w, li, epilogue, special_cols, extra=(), extra_specs=(), cast=(), cast_li=0):
    m, k = x.shape
    n = w.shape[2]
    tm, tn = RMS_MATMUL_TM, RMS_MATMUL_TN
    assert m % tm == 0 and n % tn == 0 and special_cols % tn == 0
    n_col_steps = n // tn
    kern = functools.partial(_rms_matmul_kernel, epilogue=epilogue, n_special=special_cols // tn,
                             n_extra=len(extra), n_cast=len(cast))
    cast_in_specs, cast_out_specs, cast_out_shapes = _weight_cast_plan(
        cast, cast_li, m // tm, n_col_steps, RMS_MATMUL_CAST_STEPS)
    outs = pl.pallas_call(
        kern,
        grid=(m // tm, n_col_steps),
        in_specs=[pl.BlockSpec((tm, k), lambda i, j: (i, 0)),
                  pl.BlockSpec((1, k), lambda i, j: (0, 0)),
                  pl.BlockSpec((None, k, tn), lambda i, j: (li, 0, j)),
                  *extra_specs, *cast_in_specs],
        out_specs=[pl.BlockSpec((tm, tn), lambda i, j: (i, j)), *cast_out_specs],
        out_shape=[jax.ShapeDtypeStruct((m, n), F32), *cast_out_shapes],
        scratch_shapes=[pltpu.VMEM((tm, k), BF16)],
        compiler_params=_compiler_params(("arbitrary", "arbitrary")),
        name="rms_matmul",
    )(x, g, w, *extra, *cast)
    return outs[0], outs[1:]


def _gelu_epilogue(p, extra_refs, o_ref):
    del extra_refs
    o_ref[...] = jax.nn.gelu(p)


def _rope_epilogue(p, extra_refs, o_ref):
    cos_ref, sin_ref = extra_refs
    cos = cos_ref[...]
    sin = sin_ref[...]
    for hh in range(p.shape[1] // HEAD_DIM):
        t = p[:, hh * HEAD_DIM:(hh + 1) * HEAD_DIM]
        o_ref[:, hh * HEAD_DIM:(hh + 1) * HEAD_DIM] = (
            t * cos + pltpu.roll(t, HEAD_DIM // 2, 1) * sin)


def _mixer_kernel(au_ref, av_ref, gb_ref, gc_ref, bx_ref, gch_ref, bxh_ref,
                  ws_ref, bias_ref, cw_ref, w_out_ref, g_post_ref, h_ref, o_ref, z_ref,
                  *, tiles_per_seq):
    tm = au_ref.shape[0]
    i = pl.program_id(0)

    av = av_ref[...]
    mu = jnp.mean(av, axis=-1, keepdims=True)
    cen = av - mu
    var = jnp.mean(cen * cen, axis=-1, keepdims=True)
    vn = (cen * lax.rsqrt(var + LN_EPS)).astype(BF16)
    row = lax.broadcasted_iota(jnp.int32, (CHUNK, CHUNK), 0)
    col = lax.broadcasted_iota(jnp.int32, (CHUNK, CHUNK), 1)
    causal = col <= row
    a_cols = []
    for g in range(A_GROUPS):
        cs = slice(g * A_GROUP_DIM, (g + 1) * A_GROUP_DIM)
        w_causal = jnp.where(causal, ws_ref[g], 0.0).astype(BF16)
        chunks = []
        for c in range(tm // CHUNK):
            rs = slice(c * CHUNK, (c + 1) * CHUNK)
            mixed = jnp.dot(w_causal, vn[rs, cs], preferred_element_type=F32) + bias_ref[:, cs]
            chunks.append((au_ref[rs, cs] * mixed).astype(BF16))
        a_cols.append(jnp.concatenate(chunks, axis=0))

    z = gc_ref[...] * bx_ref[...]
    at_seq_start = (i % tiles_per_seq) == 0
    z_halo = jnp.where(at_seq_start, 0.0, gch_ref[...] * bxh_ref[...])
    z_ref[0:CONV_HALO, :] = z_halo
    z_ref[CONV_HALO:CONV_HALO + tm, :] = z
    y = (cw_ref[2:3, :] * z
         + cw_ref[1:2, :] * z_ref[CONV_HALO - 1:CONV_HALO - 1 + tm, :]
         + cw_ref[0:1, :] * z_ref[CONV_HALO - 2:CONV_HALO - 2 + tm, :])
    b_out = (gb_ref[...] * y).astype(BF16)

    mixed_ab = jnp.concatenate(a_cols + [b_out], axis=1)
    f = jnp.dot(mixed_ab, w_out_ref[...], preferred_element_type=F32)
    o_ref[...] = h_ref[...] + _rmsnorm(f, g_post_ref[...])


def _mixer(proj, w_spatial, bias_full, conv_w, w_out, li, g_post, h, seq):
    m = proj.shape[0]
    tm = MIXER_TM
    assert seq % tm == 0 and tm % CHUNK == 0 and A_WIDTH == B_WIDTH
    tiles_per_seq = seq // tm
    halo_blocks = tm // CONV_HALO
    seg = lambda s: pl.BlockSpec((tm, A_WIDTH), lambda i, s=s: (i, s))
    halo = lambda s: pl.BlockSpec(
        (CONV_HALO, A_WIDTH), lambda i, s=s: (jnp.maximum(i * halo_blocks - 1, 0), s))
    kern = functools.partial(_mixer_kernel, tiles_per_seq=tiles_per_seq)
    return pl.pallas_call(
        kern,
        grid=(m // tm,),
        in_specs=[seg(0), seg(1), seg(2), seg(3), seg(4), halo(3), halo(4),
                  pl.BlockSpec((A_GROUPS, CHUNK, CHUNK), lambda i: (0, 0, 0)),
                  pl.BlockSpec((CHUNK, A_WIDTH), lambda i: (0, 0)),
                  pl.BlockSpec((CONV_WIDTH, B_WIDTH), lambda i: (0, 0)),
                  pl.BlockSpec((None, D_MODEL, D_MODEL), lambda i: (li, 0, 0)),
                  pl.BlockSpec((1, D_MODEL), lambda i: (0, 0)),
                  pl.BlockSpec((tm, D_MODEL), lambda i: (i, 0))],
        out_specs=pl.BlockSpec((tm, D_MODEL), lambda i: (i, 0)),
        out_shape=jax.ShapeDtypeStruct((m, D_MODEL), F32),
        scratch_shapes=[pltpu.VMEM((CONV_HALO + tm, B_WIDTH), F32)],
        compiler_params=_compiler_params(("parallel",)),
        name="mixer",
    )(proj, proj, proj, proj, proj, proj, proj, w_spatial, bias_full, conv_w, w_out, g_post, h)


def _attn_band_bias():
    blk = ATTN_BLOCK
    tables = []
    for _, d in DILATED_BRANCHES:
        pieces = ATTN_PERM // d
        rows = blk // pieces
        i = np.arange(blk)
        run = i // rows
        if pieces == ATTN_PERM:
            run = ATTN_PERM_STEP * (run % ATTN_PERM_STEP) + run // ATTN_PERM_STEP
        uq = pieces * (i % rows) + run
        uk = np.concatenate([uq, blk + uq])
        step = uq[:, None] + blk - uk[None, :]
        valid = (step >= 0) & (step <= blk)
        first = valid & (uk[None, :] >= blk)
        tables.append(np.stack([np.where(valid, 0.0, -np.inf), np.where(first, 0.0, -np.inf)]))
    return np.stack(tables).astype(np.float32)


def _attn_block(qb, k_prev, k_cur, v_prev, v_cur, bias, old):
    kw = jnp.concatenate([k_prev, k_cur], axis=0)
    vw = jnp.concatenate([v_prev, v_cur], axis=0)
    s = lax.dot_general(qb, kw, (((1,), (1,)), ((), ())), preferred_element_type=F32) + bias
    m_blk = jnp.max(s, axis=-1, keepdims=True)
    m_new = jnp.broadcast_to(m_blk, (s.shape[0], HEAD_DIM))
    if old is not None:
        m_new = jnp.maximum(old[0], m_new)
    p = jnp.exp2(s - jnp.concatenate([m_new, m_new], axis=1)).astype(BF16)
    v_ones = jnp.concatenate([vw, jnp.ones_like(vw)], axis=1)
    pv = jnp.dot(p, v_ones, preferred_element_type=F32)
    a_new, l_new = pv[:, :HEAD_DIM], pv[:, HEAD_DIM:]
    if old is not None:
        corr = jnp.exp2(old[0] - m_new)
        l_new = old[1] * corr + l_new
        a_new = old[2] * corr + a_new
    return m_new, l_new, a_new


def _attn_kernel(q_ref, k_ref, v_ref, bias_ref, o_ref, qp_ref, kp_ref, vp_ref,
                 qh_ref, kh_ref, vh_ref, acc_ref, m_ref, l_ref, tmp_ref, *, seq, dilations):
    blk = ATTN_BLOCK
    lp = seq // ATTN_PERM
    step = ATTN_PERM_STEP
    lq = seq // step
    q_scale = HEAD_DIM ** -0.5 * math.log2(math.e)
    for src_ref, dst_ref, half_ref, scale in ((q_ref, qp_ref, qh_ref, q_scale),
                                              (k_ref, kp_ref, kh_ref, None),
                                              (v_ref, vp_ref, vh_ref, None)):
        for lo in range(step):
            tmp_ref[lo * lq:(lo + 1) * lq, :] = src_ref[pl.ds(lo, lq, stride=step), :]
        for lo in range(step):
            for hi in range(step):
                run = lo * step + hi
                rows = tmp_ref[pl.ds(lo * lq + hi, lp, stride=step), :]
                if scale is not None:
                    rows = rows * scale
                dst_ref[run * lp:(run + 1) * lp, :] = rows
                half_ref[run * lp:(run + 1) * lp, :] = rows.astype(BF16)

    for bi, d in enumerate(dilations):
        pieces = ATTN_PERM // d
        rows = blk // pieces
        nb = lp // rows
        gn = min(nb, ATTN_GROUP)
        gs = min(ATTN_GROUP // gn, d)
        n_groups = nb // gn
        assert nb % gn == 0 and d % gs == 0
        packed = rows % BF16_SUBLANES == 0
        q_src, k_src, v_src = (qh_ref, kh_ref, vh_ref) if packed else (qp_ref, kp_ref, vp_ref)

        def starts(r_sub, n, d=d, pieces=pieces, rows=rows):
            return [pl.multiple_of((r_sub * pieces + a) * lp + n * rows, rows) for a in range(pieces)]

        def gather(ref, st, rows=rows):
            parts = [ref[pl.ds(s0, rows), :] for s0 in st]
            return parts[0] if len(parts) == 1 else jnp.concatenate(parts, axis=0)

        def gather_half(ref, st):
            return gather(ref, st).astype(BF16)

        def scatter(ref, st, val, rows=rows):
            for a, s0 in enumerate(st):
                ref[pl.ds(s0, rows), :] = val[a * rows:(a + 1) * rows]

        def body(it, carry, bi=bi, gn=gn, gs=gs, n_groups=n_groups,
                 q_src=q_src, k_src=k_src, v_src=v_src):
            sg = it // n_groups
            n0 = (it % n_groups) * gn
            work = []
            for si in range(gs):
                r_sub = sg * gs + si
                st_prev = starts(r_sub, jnp.maximum(n0 - 1, 0))
                kb = [gather_half(k_src, st_prev)]
                vb = [gather_half(v_src, st_prev)]
                for j in range(gn):
                    st = starts(r_sub, n0 + j)
                    kb.append(gather_half(k_src, st))
                    vb.append(gather_half(v_src, st))
                    if j > 0:
                        bias = bias_ref[bi, 0]
                    elif n_groups == 1:
                        bias = bias_ref[bi, 1]
                    else:
                        bias = bias_ref[bi, jnp.where(n0 == 0, 1, 0)]
                    old = None if bi == 0 else (gather(m_ref, st), gather(l_ref, st),
                                                gather(acc_ref, st))
                    work.append((st, gather_half(q_src, st), kb[j], kb[j + 1],
                                 vb[j], vb[j + 1], bias, old))
            results = [(w[0],) + _attn_block(*w[1:]) for w in work]
            for st, m_new, l_new, a_new in results:
                scatter(m_ref, st, m_new)
                scatter(l_ref, st, l_new)
                scatter(acc_ref, st, a_new)
            return carry

        lax.fori_loop(0, (d // gs) * n_groups, body, 0)

    for lo in range(step):
        for hi in range(step):
            src = slice((lo * step + hi) * lp, (lo * step + hi + 1) * lp)
            tmp_ref[pl.ds(lo * lq + hi, lp, stride=step), :] = acc_ref[src, :] / l_ref[src, :]
    for lo in range(step):
        o_ref[pl.ds(lo, lq, stride=step), :] = tmp_ref[lo * lq:(lo + 1) * lq, :]


def _attention(qkv, batch, seq):
    dilations = tuple(d for _, d in DILATED_BRANCHES)
    for window, d in DILATED_BRANCHES:
        assert window // d == ATTN_BLOCK and seq % (d * ATTN_BLOCK) == 0
        assert ATTN_PERM % d == 0 and ATTN_BLOCK % (ATTN_PERM // d) == 0
    qkv3 = qkv.reshape(batch, seq, 3 * D_MODEL)
    bias = jnp.asarray(_attn_band_bias())
    spec = lambda part: pl.BlockSpec((None, seq, HEAD_DIM),
                                     lambda b, h, part=part: (b, 0, part * N_HEADS + h))
    kern = functools.partial(_attn_kernel, seq=seq, dilations=dilations)
    out = pl.pallas_call(
        kern,
        grid=(batch, N_HEADS),
        in_specs=[spec(0), spec(1), spec(2),
                  pl.BlockSpec(bias.shape, lambda b, h: (0, 0, 0, 0))],
        out_specs=pl.BlockSpec((None, seq, HEAD_DIM), lambda b, h: (b, 0, h)),
        out_shape=jax.ShapeDtypeStruct((batch, seq, D_MODEL), F32),
        scratch_shapes=[pltpu.VMEM((seq, HEAD_DIM), F32)] * 3 + [pltpu.VMEM((seq, HEAD_DIM), BF16)] * 3
                       + [pltpu.VMEM((seq, HEAD_DIM), F32)] * 4,
        compiler_params=_compiler_params(("parallel", "parallel")),
        name="attention",
    )(qkv3, qkv3, qkv3, bias)
    return out.reshape(batch * seq, D_MODEL)


def _matmul_rms_res_kernel(a_ref, w_ref, g_ref, h_ref, o_ref):
    f = jnp.dot(a_ref[...].astype(BF16), w_ref[...], preferred_element_type=F32)
    o_ref[...] = h_ref[...] + _rmsnorm(f, g_ref[...])


def _matmul_rms_res(a, w, li, g, h):
    m, k = a.shape
    n = w.shape[2]
    tm = RES_TM
    assert m % tm == 0
    return pl.pallas_call(
        _matmul_rms_res_kernel,
        grid=(m // tm,),
        in_specs=[pl.BlockSpec((tm, k), lambda i: (i, 0)),
                  pl.BlockSpec((None, k, n), lambda i: (li, 0, 0)),
                  pl.BlockSpec((1, n), lambda i: (0, 0)),
                  pl.BlockSpec((tm, n), lambda i: (i, 0))],
        out_specs=pl.BlockSpec((tm, n), lambda i: (i, 0)),
        out_shape=jax.ShapeDtypeStruct((m, n), F32),
        compiler_params=_compiler_params(("parallel",)),
        name="matmul_rms_res",
    )(a, w, g, h)


def _mlp_kernel(x_ref, g_pre_ref, w_up_ref, w_down_ref, g_post_ref, *rest, n_cast):
    cast_in = rest[:n_cast]
    o_ref = rest[n_cast]
    cast_out = rest[n_cast + 1:2 * n_cast + 1]
    hn_ref = rest[2 * n_cast + 1]
    k = pl.program_id(1)

    @pl.when(k == 0)
    def _():
        hn_ref[...] = _rmsnorm(x_ref[...], g_pre_ref[...]).astype(BF16)
        o_ref[...] = jnp.zeros_like(o_ref)

    sub = MLP_SUB
    _cast_chunks(cast_in, cast_out)
    for c in range(w_up_ref.shape[1] // sub):
        ffn = pl.ds(c * sub, sub)
        u = jnp.dot(hn_ref[...], w_up_ref[:, ffn], preferred_element_type=F32)
        u = jnp.square(jnp.maximum(u, 0.0)).astype(BF16)
        for n in range(o_ref.shape[1] // sub):
            cols = pl.ds(n * sub, sub)
            o_ref[:, cols] += jnp.dot(u, w_down_ref[ffn, cols], preferred_element_type=F32)

    @pl.when(k == pl.num_programs(1) - 1)
    def _():
        def finish_rows(rows):
            o_ref[rows, :] = x_ref[rows, :] + _rmsnorm(o_ref[rows, :], g_post_ref[...])
        _for_row_chunks(o_ref.shape[0], finish_rows)


def _mlp(h, g_pre, w_up, w_down, g_post, cast=(), cast_li=0):
    m, d = h.shape
    f = w_up.shape[1]
    tm, tf = MLP_TM, MLP_TF
    assert m % tm == 0 and f % tf == 0
    cast_in_specs, cast_out_specs, cast_out_shapes = _weight_cast_plan(
        cast, cast_li, m // tm, f // tf, MLP_CAST_STEPS)
    outs = pl.pallas_call(
        functools.partial(_mlp_kernel, n_cast=len(cast)),
        grid=(m // tm, f // tf),
        in_specs=[pl.BlockSpec((tm, d), lambda i, k: (i, 0)),
                  pl.BlockSpec((1, d), lambda i, k: (0, 0)),
                  pl.BlockSpec((d, tf), lambda i, k: (0, k)),
                  pl.BlockSpec((tf, d), lambda i, k: (k, 0)),
                  pl.BlockSpec((1, d), lambda i, k: (0, 0)),
                  *cast_in_specs],
        out_specs=[pl.BlockSpec((tm, d), lambda i, k: (i, 0)), *cast_out_specs],
        out_shape=[jax.ShapeDtypeStruct((m, d), F32), *cast_out_shapes],
        scratch_shapes=[pltpu.VMEM((tm, d), BF16)],
        compiler_params=_compiler_params(("arbitrary", "arbitrary")),
        name="mlp",
    )(h, g_pre, w_up, w_down, g_post, *cast)
    return outs[0], outs[1:]


def _rope_tables(seq):
    half = HEAD_DIM // 2
    inv_freq = ROPE_THETA ** (-jnp.arange(half, dtype=F32) * 2.0 / HEAD_DIM)
    ang = jnp.arange(seq, dtype=jnp.int32).astype(F32)[:, None] * inv_freq[None, :]
    cos = jnp.cos(ang)
    sin = jnp.sin(ang)
    return jnp.concatenate([cos, cos], axis=-1), jnp.concatenate([-sin, sin], axis=-1)


def kernel(x, norm_mix_pre, norm_mix_post, norm_mlp_pre, norm_mlp_post, w_in_ab, w_spatial,
           b_spatial, conv_w, w_out_ab, w_qkv, w_o, w_up, w_down):
    batch, seq, d = x.shape
    assert d == D_MODEL
    depth = norm_mix_pre.shape[0]
    m = batch * seq
    h = x.reshape(m, d)
    cos_tab, sin_tab = _rope_tables(seq)
    tiles_per_seq = seq // RMS_MATMUL_TM
    rope_specs = [pl.BlockSpec((RMS_MATMUL_TM, HEAD_DIM), lambda i, j: (i % tiles_per_seq, 0))] * 2

    w_in_bf, w_out_bf, w_qkv_bf, w_o_bf = (w.astype(BF16) for w in (w_in_ab, w_out_ab, w_qkv, w_o))

    for layer in range(depth):
        g_pre = norm_mix_pre[layer][None, :]
        g_post = norm_mix_post[layer][None, :]
        if layer % 2 == 0:
            e = layer // 2
            first_cast = (w_up, w_down) if layer == 0 else ()
            proj, first_bf = _rms_matmul(h, g_pre, w_in_bf, e, _gelu_epilogue, 2 * A_WIDTH,
                                         cast=first_cast, cast_li=layer)
            if layer == 0:
                w_up_bf, w_down_bf = first_bf
            bias_full = jnp.repeat(b_spatial[e].T, A_GROUP_DIM, axis=1)
            h = _mixer(proj, w_spatial[e], bias_full, conv_w[e], w_out_bf, e, g_post, h, seq)
        else:
            o = layer // 2
            qkv, _ = _rms_matmul(h, g_pre, w_qkv_bf, o, _rope_epilogue, 2 * D_MODEL,
                                 extra=(cos_tab, sin_tab), extra_specs=rope_specs)
            att = _attention(qkv, batch, seq)
            h = _matmul_rms_res(att, w_o_bf, o, g_post, h)
        next_cast = (w_up, w_down) if layer + 1 < depth else ()
        h, next_bf = _mlp(h, norm_mlp_pre[layer][None, :], w_up_bf, w_down_bf,
                          norm_mlp_post[layer][None, :], cast=next_cast, cast_li=layer + 1)
        if next_bf:
            w_up_bf, w_down_bf = next_bf
    return h.reshape(batch, seq, d)
```

```python
import functools
import math

import jax
import jax.numpy as jnp
import numpy as np
from jax import lax
from jax.experimental import pallas as pl
from jax.experimental.pallas import tpu as pltpu

F32 = jnp.float32
BF16 = jnp.bfloat16

D_MODEL = 2048
A_WIDTH = D_MODEL // 2
B_WIDTH = D_MODEL - A_WIDTH
A_GROUPS = 8
A_GROUP_DIM = A_WIDTH // A_GROUPS
CHUNK = 128
CONV_WIDTH = 3
HEAD_DIM = 128
N_HEADS = D_MODEL // HEAD_DIM
DILATED_BRANCHES = ((128, 1), (512, 4), (2048, 16))
ATTN_BLOCK = 128
ATTN_PERM = 16
ATTN_PERM_STEP = 4
ATTN_GROUP = 32
ROPE_THETA = 10000.0
FFN_DIM = 4 * D_MODEL
RMS_EPS = 1e-6
LN_EPS = 1e-5

V7X_LANES = 128
BF16_SUBLANES = 16
V7X_VMEM_LIMIT_BYTES = 60 * 1024 * 1024

RMS_MATMUL_TM = 1024
W_IN_TN = 1024
W_QKV_TN = 2048
RMS_MATMUL_SUB_TN = 512
MIXER_TM = 256
RES_TM = 512
MLP_TM = 1024
MLP_TF = 1024
MLP_SUB = 512
CONV_HALO = 8
NORM_ROW_CHUNK = 16
RMS_MATMUL_CAST_STEPS = 64
MLP_CAST_STEPS = 128


def _compiler_params(semantics):
    return pltpu.CompilerParams(dimension_semantics=semantics,
                                vmem_limit_bytes=V7X_VMEM_LIMIT_BYTES)


def _rmsnorm(x, g):
    ms = jnp.mean(x * x, axis=-1, keepdims=True)
    return (x * lax.rsqrt(ms + RMS_EPS)) * g


def _for_row_chunks(n_rows, fn):
    for c in range(n_rows // NORM_ROW_CHUNK):
        fn(pl.ds(c * NORM_ROW_CHUNK, NORM_ROW_CHUNK))


def _weight_cast_plan(cast, cast_li, n_outer, n_inner, max_steps):
    steps = min(max_steps, 2 ** int(math.log2(n_outer * n_inner)))
    chunk = lambda i, j: jnp.minimum(i * n_inner + j, steps - 1)
    in_specs, out_specs, out_shapes = [], [], []
    for c in cast:
        _, rows, width = c.shape
        assert rows % (steps * BF16_SUBLANES) == 0
        rows_per_step = rows // steps
        in_specs.append(pl.BlockSpec((None, rows_per_step, width),
                                     lambda i, j: (cast_li, chunk(i, j), 0)))
        out_specs.append(pl.BlockSpec((rows_per_step, width), lambda i, j: (chunk(i, j), 0)))
        out_shapes.append(jax.ShapeDtypeStruct((rows, width), BF16))
    return in_specs, out_specs, out_shapes


def _cast_chunks(cast_in, cast_out):
    for src, dst in zip(cast_in, cast_out):
        dst[...] = src[...].astype(BF16)


def _rms_matmul_kernel(x_ref, g_ref, w_ref, *rest, epilogue, n_special, n_extra, n_cast):
    extra_refs = rest[:n_extra]
    cast_in = rest[n_extra:n_extra + n_cast]
    o_ref = rest[n_extra + n_cast]
    cast_out = rest[n_extra + n_cast + 1:n_extra + 2 * n_cast + 1]
    hn_ref = rest[n_extra + 2 * n_cast + 1]
    j = pl.program_id(1)

    @pl.when(j == 0)
    def _():
        hn_ref[...] = _rmsnorm(x_ref[...], g_ref[...]).astype(BF16)

    sub = RMS_MATMUL_SUB_TN
    cols = [pl.ds(c * sub, sub) for c in range(w_ref.shape[1] // sub)]

    def product(col):
        return jnp.dot(hn_ref[...], w_ref[:, col], preferred_element_type=F32)

    def cast_weight_chunks():
        _cast_chunks(cast_in, cast_out)

    @pl.when(j < n_special)
    def _():
        cast_weight_chunks()
        for col in cols:
            epilogue(product(col), extra_refs, o_ref.at[:, col])

    @pl.when(j >= n_special)
    def _():
        cast_weight_chunks()
        for col in cols:
            o_ref[:, col] = product(col)


def _rms_matmul(x, g, w, li, epilogue, special_cols, tn, extra=(), extra_specs=(), cast=(),
                cast_li=0):
    m, k = x.shape
    n = w.shape[2]
    tm = RMS_MATMUL_TM
    assert m % tm == 0 and n % tn == 0 and special_cols % tn == 0
    n_col_steps = n // tn
    kern = functools.partial(_rms_matmul_kernel, epilogue=epilogue, n_special=special_cols // tn,
                             n_extra=len(extra), n_cast=len(cast))
    cast_in_specs, cast_out_specs, cast_out_shapes = _weight_cast_plan(
        cast, cast_li, m // tm, n_col_steps, RMS_MATMUL_CAST_STEPS)
    outs = pl.pallas_call(
        kern,
        grid=(m // tm, n_col_steps),
        in_specs=[pl.BlockSpec((tm, k), lambda i, j: (i, 0)),
                  pl.BlockSpec((1, k), lambda i, j: (0, 0)),
                  pl.BlockSpec((None, k, tn), lambda i, j: (li, 0, j)),
                  *extra_specs, *cast_in_specs],
        out_specs=[pl.BlockSpec((tm, tn), lambda i, j: (i, j)), *cast_out_specs],
        out_shape=[jax.ShapeDtypeStruct((m, n), F32), *cast_out_shapes],
        scratch_shapes=[pltpu.VMEM((tm, k), BF16)],
        compiler_params=_compiler_params(("arbitrary", "arbitrary")),
        name="rms_matmul",
    )(x, g, w, *extra, *cast)
    return outs[0], outs[1:]


def _gelu_epilogue(p, extra_refs, o_ref):
    del extra_refs
    o_ref[...] = jax.nn.gelu(p)


def _rope_epilogue(p, extra_refs, o_ref):
    cos_ref, sin_ref = extra_refs
    cos = cos_ref[...]
    sin = sin_ref[...]
    for hh in range(p.shape[1] // HEAD_DIM):
        t = p[:, hh * HEAD_DIM:(hh + 1) * HEAD_DIM]
        o_ref[:, hh * HEAD_DIM:(hh + 1) * HEAD_DIM] = (
            t * cos + pltpu.roll(t, HEAD_DIM // 2, 1) * sin)


def _mixer_kernel(au_ref, av_ref, gb_ref, gc_ref, bx_ref, gch_ref, bxh_ref,
                  ws_ref, bias_ref, cw_ref, w_out_ref, g_post_ref, h_ref, o_ref, z_ref,
                  *, tiles_per_seq):
    tm = au_ref.shape[0]
    i = pl.program_id(0)

    av = av_ref[...]
    mu = jnp.mean(av, axis=-1, keepdims=True)
    cen = av - mu
    var = jnp.mean(cen * cen, axis=-1, keepdims=True)
    vn = (cen * lax.rsqrt(var + LN_EPS)).astype(BF16)
    row = lax.broadcasted_iota(jnp.int32, (CHUNK, CHUNK), 0)
    col = lax.broadcasted_iota(jnp.int32, (CHUNK, CHUNK), 1)
    causal = col <= row
    a_cols = []
    for g in range(A_GROUPS):
        cs = slice(g * A_GROUP_DIM, (g + 1) * A_GROUP_DIM)
        w_causal = jnp.where(causal, ws_ref[g], 0.0).astype(BF16)
        chunks = []
        for c in range(tm // CHUNK):
            rs = slice(c * CHUNK, (c + 1) * CHUNK)
            mixed = jnp.dot(w_causal, vn[rs, cs], preferred_element_type=F32) + bias_ref[:, cs]
            chunks.append((au_ref[rs, cs] * mixed).astype(BF16))
        a_cols.append(jnp.concatenate(chunks, axis=0))

    z = gc_ref[...] * bx_ref[...]
    at_seq_start = (i % tiles_per_seq) == 0
    z_halo = jnp.where(at_seq_start, 0.0, gch_ref[...] * bxh_ref[...])
    z_ref[0:CONV_HALO, :] = z_halo
    z_ref[CONV_HALO:CONV_HALO + tm, :] = z
    y = (cw_ref[2:3, :] * z
         + cw_ref[1:2, :] * z_ref[CONV_HALO - 1:CONV_HALO - 1 + tm, :]
         + cw_ref[0:1, :] * z_ref[CONV_HALO - 2:CONV_HALO - 2 + tm, :])
    b_out = (gb_ref[...] * y).astype(BF16)

    mixed_ab = jnp.concatenate(a_cols + [b_out], axis=1)
    f = jnp.dot(mixed_ab, w_out_ref[...], preferred_element_type=F32)
    o_ref[...] = h_ref[...] + _rmsnorm(f, g_post_ref[...])


def _mixer(proj, w_spatial, bias_full, conv_w, w_out, li, g_post, h, seq):
    m = proj.shape[0]
    tm = MIXER_TM
    assert seq % tm == 0 and tm % CHUNK == 0 and A_WIDTH == B_WIDTH
    tiles_per_seq = seq // tm
    halo_blocks = tm // CONV_HALO
    seg = lambda s: pl.BlockSpec((tm, A_WIDTH), lambda i, s=s: (i, s))
    halo = lambda s: pl.BlockSpec(
        (CONV_HALO, A_WIDTH), lambda i, s=s: (jnp.maximum(i * halo_blocks - 1, 0), s))
    kern = functools.partial(_mixer_kernel, tiles_per_seq=tiles_per_seq)
    return pl.pallas_call(
        kern,
        grid=(m // tm,),
        in_specs=[seg(0), seg(1), seg(2), seg(3), seg(4), halo(3), halo(4),
                  pl.BlockSpec((A_GROUPS, CHUNK, CHUNK), lambda i: (0, 0, 0)),
                  pl.BlockSpec((CHUNK, A_WIDTH), lambda i: (0, 0)),
                  pl.BlockSpec((CONV_WIDTH, B_WIDTH), lambda i: (0, 0)),
                  pl.BlockSpec((None, D_MODEL, D_MODEL), lambda i: (li, 0, 0)),
                  pl.BlockSpec((1, D_MODEL), lambda i: (0, 0)),
                  pl.BlockSpec((tm, D_MODEL), lambda i: (i, 0))],
        out_specs=pl.BlockSpec((tm, D_MODEL), lambda i: (i, 0)),
        out_shape=jax.ShapeDtypeStruct((m, D_MODEL), F32),
        scratch_shapes=[pltpu.VMEM((CONV_HALO + tm, B_WIDTH), F32)],
        compiler_params=_compiler_params(("parallel",)),
        name="mixer",
    )(proj, proj, proj, proj, proj, proj, proj, w_spatial, bias_full, conv_w, w_out, g_post, h)


def _attn_band_bias():
    blk = ATTN_BLOCK
    tables = []
    for _, d in DILATED_BRANCHES:
        pieces = ATTN_PERM // d
        rows = blk // pieces
        i = np.arange(blk)
        run = i // rows
        if pieces == ATTN_PERM:
            run = ATTN_PERM_STEP * (run % ATTN_PERM_STEP) + run // ATTN_PERM_STEP
        uq = pieces * (i % rows) + run
        uk = np.concatenate([uq, blk + uq])
        step = uq[:, None] + blk - uk[None, :]
        valid = (step >= 0) & (step <= blk)
        first = valid & (uk[None, :] >= blk)
        tables.append(np.stack([np.where(valid, 0.0, -np.inf), np.where(first, 0.0, -np.inf)]))
    return np.stack(tables).astype(np.float32)


def _attn_block(qb, k_prev, k_cur, v_prev, v_cur, bias, old):
    kw = jnp.concatenate([k_prev, k_cur], axis=0)
    vw = jnp.concatenate([v_prev, v_cur], axis=0)
    s = lax.dot_general(qb, kw, (((1,), (1,)), ((), ())), preferred_element_type=F32) + bias
    m_blk = jnp.max(s, axis=-1, keepdims=True)
    m_new = jnp.broadcast_to(m_blk, (s.shape[0], HEAD_DIM))
    if old is not None:
        m_new = jnp.maximum(old[0], m_new)
    p = jnp.exp2(s - jnp.concatenate([m_new, m_new], axis=1)).astype(BF16)
    v_ones = jnp.concatenate([vw, jnp.ones_like(vw)], axis=1)
    pv = jnp.dot(p, v_ones, preferred_element_type=F32)
    a_new, l_new = pv[:, :HEAD_DIM], pv[:, HEAD_DIM:]
    if old is not None:
        corr = jnp.exp2(old[0] - m_new)
        l_new = old[1] * corr + l_new
        a_new = old[2] * corr + a_new
    return m_new, l_new, a_new


def _attn_kernel(q_ref, k_ref, v_ref, bias_ref, o_ref, qp_ref, kp_ref, vp_ref,
                 qh_ref, kh_ref, vh_ref, acc_ref, m_ref, l_ref, tmp_ref, *, seq, dilations):
    blk = ATTN_BLOCK
    lp = seq // ATTN_PERM
    step = ATTN_PERM_STEP
    lq = seq // step
    q_scale = HEAD_DIM ** -0.5 * math.log2(math.e)
    for src_ref, dst_ref, half_ref, scale in ((q_ref, qp_ref, qh_ref, q_scale),
                                              (k_ref, kp_ref, kh_ref, None),
                                              (v_ref, vp_ref, vh_ref, None)):
        for lo in range(step):
            tmp_ref[lo * lq:(lo + 1) * lq, :] = src_ref[pl.ds(lo, lq, stride=step), :]
        for lo in range(step):
            for hi in range(step):
                run = lo * step + hi
                rows = tmp_ref[pl.ds(lo * lq + hi, lp, stride=step), :]
                if scale is not None:
                    rows = rows * scale
                dst_ref[run * lp:(run + 1) * lp, :] = rows
                half_ref[run * lp:(run + 1) * lp, :] = rows.astype(BF16)

    for bi, d in enumerate(dilations):
        pieces = ATTN_PERM // d
        rows = blk // pieces
        nb = lp // rows
        gn = min(nb, ATTN_GROUP)
        gs = min(ATTN_GROUP // gn, d)
        n_groups = nb // gn
        assert nb % gn == 0 and d % gs == 0
        packed = rows % BF16_SUBLANES == 0
        q_src, k_src, v_src = (qh_ref, kh_ref, vh_ref) if packed else (qp_ref, kp_ref, vp_ref)

        def starts(r_sub, n, d=d, pieces=pieces, rows=rows):
            return [pl.multiple_of((r_sub * pieces + a) * lp + n * rows, rows) for a in range(pieces)]

        def gather(ref, st, rows=rows):
            parts = [ref[pl.ds(s0, rows), :] for s0 in st]
            return parts[0] if len(parts) == 1 else jnp.concatenate(parts, axis=0)

        def gather_half(ref, st):
            return gather(ref, st).astype(BF16)

        def scatter(ref, st, val, rows=rows):
            for a, s0 in enumerate(st):
                ref[pl.ds(s0, rows), :] = val[a * rows:(a + 1) * rows]

        def body(it, carry, bi=bi, gn=gn, gs=gs, n_groups=n_groups,
                 q_src=q_src, k_src=k_src, v_src=v_src):
            sg = it // n_groups
            n0 = (it % n_groups) * gn
            work = []
            for si in range(gs):
                r_sub = sg * gs + si
                st_prev = starts(r_sub, jnp.maximum(n0 - 1, 0))
                kb = [gather_half(k_src, st_prev)]
                vb = [gather_half(v_src, st_prev)]
                for j in range(gn):
                    st = starts(r_sub, n0 + j)
                    kb.append(gather_half(k_src, st))
                    vb.append(gather_half(v_src, st))
                    if j > 0:
                        bias = bias_ref[bi, 0]
                    elif n_groups == 1:
                        bias = bias_ref[bi, 1]
                    else:
                        bias = bias_ref[bi, jnp.where(n0 == 0, 1, 0)]
                    old = None if bi == 0 else (gather(m_ref, st), gather(l_ref, st),
                                                gather(acc_ref, st))
                    work.append((st, gather_half(q_src, st), kb[j], kb[j + 1],
                                 vb[j], vb[j + 1], bias, old))
            results = [(w[0],) + _attn_block(*w[1:]) for w in work]
            for st, m_new, l_new, a_new in results:
                scatter(m_ref, st, m_new)
                scatter(l_ref, st, l_new)
                scatter(acc_ref, st, a_new)
            return carry

        lax.fori_loop(0, (d // gs) * n_groups, body, 0)

    for lo in range(step):
        for hi in range(step):
            src = slice((lo * step + hi) * lp, (lo * step + hi + 1) * lp)
            tmp_ref[pl.ds(lo * lq + hi, lp, stride=step), :] = acc_ref[src, :] / l_ref[src, :]
    for lo in range(step):
        o_ref[pl.ds(lo, lq, stride=step), :] = tmp_ref[lo * lq:(lo + 1) * lq, :]


def _attention(qkv, batch, seq):
    dilations = tuple(d for _, d in DILATED_BRANCHES)
    for window, d in DILATED_BRANCHES:
        assert window // d == ATTN_BLOCK and seq % (d * ATTN_BLOCK) == 0
        assert ATTN_PERM % d == 0 and ATTN_BLOCK % (ATTN_PERM // d) == 0
    qkv3 = qkv.reshape(batch, seq, 3 * D_MODEL)
    bias = jnp.asarray(_attn_band_bias())
    spec = lambda part: pl.BlockSpec((None, seq, HEAD_DIM),
                                     lambda b, h, part=part: (b, 0, part * N_HEADS + h))
    kern = functools.partial(_attn_kernel, seq=seq, dilations=dilations)
    out = pl.pallas_call(
        kern,
        grid=(batch, N_HEADS),
        in_specs=[spec(0), spec(1), spec(2),
                  pl.BlockSpec(bias.shape, lambda b, h: (0, 0, 0, 0))],
        out_specs=pl.BlockSpec((None, seq, HEAD_DIM), lambda b, h: (b, 0, h)),
        out_shape=jax.ShapeDtypeStruct((batch, seq, D_MODEL), F32),
        scratch_shapes=[pltpu.VMEM((seq, HEAD_DIM), F32)] * 3 + [pltpu.VMEM((seq, HEAD_DIM), BF16)] * 3
                       + [pltpu.VMEM((seq, HEAD_DIM), F32)] * 4,
        compiler_params=_compiler_params(("parallel", "parallel")),
        name="attention",
    )(qkv3, qkv3, qkv3, bias)
    return out.reshape(batch * seq, D_MODEL)


def _matmul_rms_res_kernel(a_ref, w_ref, g_ref, h_ref, o_ref):
    f = jnp.dot(a_ref[...].astype(BF16), w_ref[...], preferred_element_type=F32)
    o_ref[...] = h_ref[...] + _rmsnorm(f, g_ref[...])


def _matmul_rms_res(a, w, li, g, h):
    m, k = a.shape
    n = w.shape[2]
    tm = RES_TM
    assert m % tm == 0
    return pl.pallas_call(
        _matmul_rms_res_kernel,
        grid=(m // tm,),
        in_specs=[pl.BlockSpec((tm, k), lambda i: (i, 0)),
                  pl.BlockSpec((None, k, n), lambda i: (li, 0, 0)),
                  pl.BlockSpec((1, n), lambda i: (0, 0)),
                  pl.BlockSpec((tm, n), lambda i: (i, 0))],
        out_specs=pl.BlockSpec((tm, n), lambda i: (i, 0)),
        out_shape=jax.ShapeDtypeStruct((m, n), F32),
        compiler_params=_compiler_params(("parallel",)),
        name="matmul_rms_res",
    )(a, w, g, h)


def _mlp_kernel(x_ref, g_pre_ref, w_up_ref, w_down_ref, g_post_ref, *rest, n_cast):
    cast_in = rest[:n_cast]
    o_ref = rest[n_cast]
    cast_out = rest[n_cast + 1:2 * n_cast + 1]
    hn_ref = rest[2 * n_cast + 1]
    k = pl.program_id(1)

    @pl.when(k == 0)
    def _():
        hn_ref[...] = _rmsnorm(x_ref[...], g_pre_ref[...]).astype(BF16)
        o_ref[...] = jnp.zeros_like(o_ref)

    sub = MLP_SUB
    _cast_chunks(cast_in, cast_out)
    for c in range(w_up_ref.shape[1] // sub):
        ffn = pl.ds(c * sub, sub)
        u = jnp.dot(hn_ref[...], w_up_ref[:, ffn], preferred_element_type=F32)
        u = jnp.square(jnp.maximum(u, 0.0)).astype(BF16)
        for n in range(o_ref.shape[1] // sub):
            cols = pl.ds(n * sub, sub)
            o_ref[:, cols] += jnp.dot(u, w_down_ref[ffn, cols], preferred_element_type=F32)

    @pl.when(k == pl.num_programs(1) - 1)
    def _():
        def finish_rows(rows):
            o_ref[rows, :] = x_ref[rows, :] + _rmsnorm(o_ref[rows, :], g_post_ref[...])
        _for_row_chunks(o_ref.shape[0], finish_rows)


def _mlp(h, g_pre, w_up, w_down, g_post, cast=(), cast_li=0):
    m, d = h.shape
    f = w_up.shape[1]
    tm, tf = MLP_TM, MLP_TF
    assert m % tm == 0 and f % tf == 0
    cast_in_specs, cast_out_specs, cast_out_shapes = _weight_cast_plan(
        cast, cast_li, m // tm, f // tf, MLP_CAST_STEPS)
    outs = pl.pallas_call(
        functools.partial(_mlp_kernel, n_cast=len(cast)),
        grid=(m // tm, f // tf),
        in_specs=[pl.BlockSpec((tm, d), lambda i, k: (i, 0)),
                  pl.BlockSpec((1, d), lambda i, k: (0, 0)),
                  pl.BlockSpec((d, tf), lambda i, k: (0, k)),
                  pl.BlockSpec((tf, d), lambda i, k: (k, 0)),
                  pl.BlockSpec((1, d), lambda i, k: (0, 0)),
                  *cast_in_specs],
        out_specs=[pl.BlockSpec((tm, d), lambda i, k: (i, 0)), *cast_out_specs],
        out_shape=[jax.ShapeDtypeStruct((m, d), F32), *cast_out_shapes],
        scratch_shapes=[pltpu.VMEM((tm, d), BF16)],
        compiler_params=_compiler_params(("arbitrary", "arbitrary")),
        name="mlp",
    )(h, g_pre, w_up, w_down, g_post, *cast)
    return outs[0], outs[1:]


def _rope_tables(seq):
    half = HEAD_DIM // 2
    inv_freq = ROPE_THETA ** (-jnp.arange(half, dtype=F32) * 2.0 / HEAD_DIM)
    ang = jnp.arange(seq, dtype=jnp.int32).astype(F32)[:, None] * inv_freq[None, :]
    cos = jnp.cos(ang)
    sin = jnp.sin(ang)
    return jnp.concatenate([cos, cos], axis=-1), jnp.concatenate([-sin, sin], axis=-1)


def kernel(x, norm_mix_pre, norm_mix_post, norm_mlp_pre, norm_mlp_post, w_in_ab, w_spatial,
           b_spatial, conv_w, w_out_ab, w_qkv, w_o, w_up, w_down):
    batch, seq, d = x.shape
    assert d == D_MODEL
    depth = norm_mix_pre.shape[0]
    m = batch * seq
    h = x.reshape(m, d)
    cos_tab, sin_tab = _rope_tables(seq)
    tiles_per_seq = seq // RMS_MATMUL_TM
    rope_specs = [pl.BlockSpec((RMS_MATMUL_TM, HEAD_DIM), lambda i, j: (i % tiles_per_seq, 0))] * 2

    w_in_bf, w_out_bf, w_qkv_bf, w_o_bf = (w.astype(BF16) for w in (w_in_ab, w_out_ab, w_qkv, w_o))

    for layer in range(depth):
        g_pre = norm_mix_pre[layer][None, :]
        g_post = norm_mix_post[layer][None, :]
        if layer % 2 == 0:
            e = layer // 2
            first_cast = (w_up, w_down) if layer == 0 else ()
            proj, first_bf = _rms_matmul(h, g_pre, w_in_bf, e, _gelu_epilogue, 2 * A_WIDTH,
                                         W_IN_TN, cast=first_cast, cast_li=layer)
            if layer == 0:
                w_up_bf, w_down_bf = first_bf
            bias_full = jnp.repeat(b_spatial[e].T, A_GROUP_DIM, axis=1)
            h = _mixer(proj, w_spatial[e], bias_full, conv_w[e], w_out_bf, e, g_post, h, seq)
        else:
            o = layer // 2
            qkv, _ = _rms_matmul(h, g_pre, w_qkv_bf, o, _rope_epilogue, 2 * D_MODEL, W_QKV_TN,
                                 extra=(cos_tab, sin_tab), extra_specs=rope_specs)
            att = _attention(qkv, batch, seq)
            h = _matmul_rms_res(att, w_o_bf, o, g_post, h)
        next_cast = (w_up, w_down) if layer + 1 < depth else ()
        h, next_bf = _mlp(h, norm_mlp_pre[layer][None, :], w_up_bf, w_down_bf,
                          norm_mlp_post[layer][None, :], cast=next_cast, cast_li=layer + 1)
        if next_bf:
            w_up_bf, w_down_bf = next_bf
    return h.reshape(batch, seq, d)
```

```python
import functools
import math

import jax
import jax.numpy as jnp
import numpy as np
from jax import lax
from jax.experimental import pallas as pl
from jax.experimental.pallas import tpu as pltpu

F32 = jnp.float32
BF16 = jnp.bfloat16

D_MODEL = 2048
A_WIDTH = D_MODEL // 2
B_WIDTH = D_MODEL - A_WIDTH
A_GROUPS = 8
A_GROUP_DIM = A_WIDTH // A_GROUPS
CHUNK = 128
CONV_WIDTH = 3
HEAD_DIM = 128
N_HEADS = D_MODEL // HEAD_DIM
DILATED_BRANCHES = ((128, 1), (512, 4), (2048, 16))
ATTN_BLOCK = 128
ATTN_PERM = 16
ATTN_PERM_STEP = 4
ATTN_GROUP = 32
ROPE_THETA = 10000.0
FFN_DIM = 4 * D_MODEL
RMS_EPS = 1e-6
LN_EPS = 1e-5

V7X_LANES = 128
BF16_SUBLANES = 16
V7X_VMEM_LIMIT_BYTES = 60 * 1024 * 1024

RMS_MATMUL_TM = 1024
W_IN_TN = 1024
W_QKV_TN = 2048
RMS_MATMUL_SUB_TN = 512
MIXER_TM = 512
RES_TM = 512
MLP_TM = 1024
MLP_TF = 1024
MLP_SUB = 512
CONV_HALO = 8
NORM_ROW_CHUNK = 16
RMS_MATMUL_CAST_STEPS = 64
MLP_CAST_STEPS = 128


def _compiler_params(semantics):
    return pltpu.CompilerParams(dimension_semantics=semantics,
                                vmem_limit_bytes=V7X_VMEM_LIMIT_BYTES)


def _rmsnorm(x, g):
    ms = jnp.mean(x * x, axis=-1, keepdims=True)
    return (x * lax.rsqrt(ms + RMS_EPS)) * g


def _for_row_chunks(n_rows, fn):
    for c in range(n_rows // NORM_ROW_CHUNK):
        fn(pl.ds(c * NORM_ROW_CHUNK, NORM_ROW_CHUNK))


def _weight_cast_plan(cast, cast_li, n_outer, n_inner, max_steps):
    steps = min(max_steps, 2 ** int(math.log2(n_outer * n_inner)))
    chunk = lambda i, j: jnp.minimum(i * n_inner + j, steps - 1)
    in_specs, out_specs, out_shapes = [], [], []
    for c in cast:
        _, rows, width = c.shape
        assert rows % (steps * BF16_SUBLANES) == 0
        rows_per_step = rows // steps
        in_specs.append(pl.BlockSpec((None, rows_per_step, width),
                                     lambda i, j: (cast_li, chunk(i, j), 0)))
        out_specs.append(pl.BlockSpec((rows_per_step, width), lambda i, j: (chunk(i, j), 0)))
        out_shapes.append(jax.ShapeDtypeStruct((rows, width), BF16))
    return in_specs, out_specs, out_shapes


def _cast_chunks(cast_in, cast_out):
    for src, dst in zip(cast_in, cast_out):
        dst[...] = src[...].astype(BF16)


def _rms_matmul_kernel(x_ref, g_ref, w_ref, *rest, epilogue, n_special, n_extra, n_cast):
    extra_refs = rest[:n_extra]
    cast_in = rest[n_extra:n_extra + n_cast]
    o_ref = rest[n_extra + n_cast]
    cast_out = rest[n_extra + n_cast + 1:n_extra + 2 * n_cast + 1]
    hn_ref = rest[n_extra + 2 * n_cast + 1]
    j = pl.program_id(1)

    @pl.when(j == 0)
    def _():
        hn_ref[...] = _rmsnorm(x_ref[...], g_ref[...]).astype(BF16)

    sub = RMS_MATMUL_SUB_TN
    cols = [pl.ds(c * sub, sub) for c in range(w_ref.shape[1] // sub)]

    def product(col):
        return jnp.dot(hn_ref[...], w_ref[:, col], preferred_element_type=F32)

    def cast_weight_chunks():
        _cast_chunks(cast_in, cast_out)

    @pl.when(j < n_special)
    def _():
        cast_weight_chunks()
        for col in cols:
            epilogue(product(col), extra_refs, o_ref.at[:, col])

    @pl.when(j >= n_special)
    def _():
        cast_weight_chunks()
        for col in cols:
            o_ref[:, col] = product(col)


def _rms_matmul(x, g, w, li, epilogue, special_cols, tn, extra=(), extra_specs=(), cast=(),
                cast_li=0):
    m, k = x.shape
    n = w.shape[2]
    tm = RMS_MATMUL_TM
    assert m % tm == 0 and n % tn == 0 and special_cols % tn == 0
    n_col_steps = n // tn
    kern = functools.partial(_rms_matmul_kernel, epilogue=epilogue, n_special=special_cols // tn,
                             n_extra=len(extra), n_cast=len(cast))
    cast_in_specs, cast_out_specs, cast_out_shapes = _weight_cast_plan(
        cast, cast_li, m // tm, n_col_steps, RMS_MATMUL_CAST_STEPS)
    outs = pl.pallas_call(
        kern,
        grid=(m // tm, n_col_steps),
        in_specs=[pl.BlockSpec((tm, k), lambda i, j: (i, 0)),
                  pl.BlockSpec((1, k), lambda i, j: (0, 0)),
                  pl.BlockSpec((None, k, tn), lambda i, j: (li, 0, j)),
                  *extra_specs, *cast_in_specs],
        out_specs=[pl.BlockSpec((tm, tn), lambda i, j: (i, j)), *cast_out_specs],
        out_shape=[jax.ShapeDtypeStruct((m, n), F32), *cast_out_shapes],
        scratch_shapes=[pltpu.VMEM((tm, k), BF16)],
        compiler_params=_compiler_params(("arbitrary", "arbitrary")),
        name="rms_matmul",
    )(x, g, w, *extra, *cast)
    return outs[0], outs[1:]


def _gelu_epilogue(p, extra_refs, o_ref):
    del extra_refs
    o_ref[...] = jax.nn.gelu(p)


def _rope_epilogue(p, extra_refs, o_ref):
    cos_ref, sin_ref = extra_refs
    cos = cos_ref[...]
    sin = sin_ref[...]
    for hh in range(p.shape[1] // HEAD_DIM):
        t = p[:, hh * HEAD_DIM:(hh + 1) * HEAD_DIM]
        o_ref[:, hh * HEAD_DIM:(hh + 1) * HEAD_DIM] = (
            t * cos + pltpu.roll(t, HEAD_DIM // 2, 1) * sin)


def _mixer_kernel(au_ref, av_ref, gb_ref, gc_ref, bx_ref, gch_ref, bxh_ref,
                  ws_ref, bias_ref, cw_ref, w_out_ref, g_post_ref, h_ref, o_ref, z_ref,
                  *, tiles_per_seq):
    tm = au_ref.shape[0]
    i = pl.program_id(0)

    av = av_ref[...]
    mu = jnp.mean(av, axis=-1, keepdims=True)
    cen = av - mu
    var = jnp.mean(cen * cen, axis=-1, keepdims=True)
    vn = (cen * lax.rsqrt(var + LN_EPS)).astype(BF16)
    row = lax.broadcasted_iota(jnp.int32, (CHUNK, CHUNK), 0)
    col = lax.broadcasted_iota(jnp.int32, (CHUNK, CHUNK), 1)
    causal = col <= row
    a_cols = []
    for g in range(A_GROUPS):
        cs = slice(g * A_GROUP_DIM, (g + 1) * A_GROUP_DIM)
        w_causal = jnp.where(causal, ws_ref[g], 0.0).astype(BF16)
        chunks = []
        for c in range(tm // CHUNK):
            rs = slice(c * CHUNK, (c + 1) * CHUNK)
            mixed = jnp.dot(w_causal, vn[rs, cs], preferred_element_type=F32) + bias_ref[:, cs]
            chunks.append((au_ref[rs, cs] * mixed).astype(BF16))
        a_cols.append(jnp.concatenate(chunks, axis=0))

    z = gc_ref[...] * bx_ref[...]
    at_seq_start = (i % tiles_per_seq) == 0
    z_halo = jnp.where(at_seq_start, 0.0, gch_ref[...] * bxh_ref[...])
    z_ref[0:CONV_HALO, :] = z_halo
    z_ref[CONV_HALO:CONV_HALO + tm, :] = z
    y = (cw_ref[2:3, :] * z
         + cw_ref[1:2, :] * z_ref[CONV_HALO - 1:CONV_HALO - 1 + tm, :]
         + cw_ref[0:1, :] * z_ref[CONV_HALO - 2:CONV_HALO - 2 + tm, :])
    b_out = (gb_ref[...] * y).astype(BF16)

    mixed_ab = jnp.concatenate(a_cols + [b_out], axis=1)
    f = jnp.dot(mixed_ab, w_out_ref[...], preferred_element_type=F32)
    o_ref[...] = h_ref[...] + _rmsnorm(f, g_post_ref[...])


def _mixer(proj, w_spatial, bias_full, conv_w, w_out, li, g_post, h, seq):
    m = proj.shape[0]
    tm = MIXER_TM
    assert seq % tm == 0 and tm % CHUNK == 0 and A_WIDTH == B_WIDTH
    tiles_per_seq = seq // tm
    halo_blocks = tm // CONV_HALO
    seg = lambda s: pl.BlockSpec((tm, A_WIDTH), lambda i, s=s: (i, s))
    halo = lambda s: pl.BlockSpec(
        (CONV_HALO, A_WIDTH), lambda i, s=s: (jnp.maximum(i * halo_blocks - 1, 0), s))
    kern = functools.partial(_mixer_kernel, tiles_per_seq=tiles_per_seq)
    return pl.pallas_call(
        kern,
        grid=(m // tm,),
        in_specs=[seg(0), seg(1), seg(2), seg(3), seg(4), halo(3), halo(4),
                  pl.BlockSpec((A_GROUPS, CHUNK, CHUNK), lambda i: (0, 0, 0)),
                  pl.BlockSpec((CHUNK, A_WIDTH), lambda i: (0, 0)),
                  pl.BlockSpec((CONV_WIDTH, B_WIDTH), lambda i: (0, 0)),
                  pl.BlockSpec((None, D_MODEL, D_MODEL), lambda i: (li, 0, 0),
                               pipeline_mode=pl.Buffered(1)),
                  pl.BlockSpec((1, D_MODEL), lambda i: (0, 0)),
                  pl.BlockSpec((tm, D_MODEL), lambda i: (i, 0))],
        out_specs=pl.BlockSpec((tm, D_MODEL), lambda i: (i, 0)),
        out_shape=jax.ShapeDtypeStruct((m, D_MODEL), F32),
        scratch_shapes=[pltpu.VMEM((CONV_HALO + tm, B_WIDTH), F32)],
        compiler_params=_compiler_params(("parallel",)),
        name="mixer",
    )(proj, proj, proj, proj, proj, proj, proj, w_spatial, bias_full, conv_w, w_out, g_post, h)


def _attn_band_bias():
    blk = ATTN_BLOCK
    tables = []
    for _, d in DILATED_BRANCHES:
        pieces = ATTN_PERM // d
        rows = blk // pieces
        i = np.arange(blk)
        run = i // rows
        if pieces == ATTN_PERM:
            run = ATTN_PERM_STEP * (run % ATTN_PERM_STEP) + run // ATTN_PERM_STEP
        uq = pieces * (i % rows) + run
        uk = np.concatenate([uq, blk + uq])
        step = uq[:, None] + blk - uk[None, :]
        valid = (step >= 0) & (step <= blk)
        first = valid & (uk[None, :] >= blk)
        tables.append(np.stack([np.where(valid, 0.0, -np.inf), np.where(first, 0.0, -np.inf)]))
    return np.stack(tables).astype(np.float32)


def _attn_block(qb, k_prev, k_cur, v_prev, v_cur, bias, old):
    kw = jnp.concatenate([k_prev, k_cur], axis=0)
    vw = jnp.concatenate([v_prev, v_cur], axis=0)
    s = lax.dot_general(qb, kw, (((1,), (1,)), ((), ())), preferred_element_type=F32) + bias
    m_blk = jnp.max(s, axis=-1, keepdims=True)
    m_new = jnp.broadcast_to(m_blk, (s.shape[0], HEAD_DIM))
    if old is not None:
        m_new = jnp.maximum(old[0], m_new)
    p = jnp.exp2(s - jnp.concatenate([m_new, m_new], axis=1)).astype(BF16)
    v_ones = jnp.concatenate([vw, jnp.ones_like(vw)], axis=1)
    pv = jnp.dot(p, v_ones, preferred_element_type=F32)
    a_new, l_new = pv[:, :HEAD_DIM], pv[:, HEAD_DIM:]
    if old is not None:
        corr = jnp.exp2(old[0] - m_new)
        l_new = old[1] * corr + l_new
        a_new = old[2] * corr + a_new
    return m_new, l_new, a_new


def _attn_kernel(q_ref, k_ref, v_ref, bias_ref, o_ref, qp_ref, kp_ref, vp_ref,
                 qh_ref, kh_ref, vh_ref, acc_ref, m_ref, l_ref, tmp_ref, *, seq, dilations):
    blk = ATTN_BLOCK
    lp = seq // ATTN_PERM
    step = ATTN_PERM_STEP
    lq = seq // step
    q_scale = HEAD_DIM ** -0.5 * math.log2(math.e)
    for src_ref, dst_ref, half_ref, scale in ((q_ref, qp_ref, qh_ref, q_scale),
                                              (k_ref, kp_ref, kh_ref, None),
                                              (v_ref, vp_ref, vh_ref, None)):
        for lo in range(step):
            tmp_ref[lo * lq:(lo + 1) * lq, :] = src_ref[pl.ds(lo, lq, stride=step), :]
        for lo in range(step):
            for hi in range(step):
                run = lo * step + hi
                rows = tmp_ref[pl.ds(lo * lq + hi, lp, stride=step), :]
                if scale is not None:
                    rows = rows * scale
                dst_ref[run * lp:(run + 1) * lp, :] = rows
                half_ref[run * lp:(run + 1) * lp, :] = rows.astype(BF16)

    for bi, d in enumerate(dilations):
        pieces = ATTN_PERM // d
        rows = blk // pieces
        nb = lp // rows
        gn = min(nb, ATTN_GROUP)
        gs = min(ATTN_GROUP // gn, d)
        n_groups = nb // gn
        assert nb % gn == 0 and d % gs == 0
        packed = rows % BF16_SUBLANES == 0
        q_src, k_src, v_src = (qh_ref, kh_ref, vh_ref) if packed else (qp_ref, kp_ref, vp_ref)

        def starts(r_sub, n, d=d, pieces=pieces, rows=rows):
            return [pl.multiple_of((r_sub * pieces + a) * lp + n * rows, rows) for a in range(pieces)]

        def gather(ref, st, rows=rows):
            parts = [ref[pl.ds(s0, rows), :] for s0 in st]
            return parts[0] if len(parts) == 1 else jnp.concatenate(parts, axis=0)

        def gather_half(ref, st):
            return gather(ref, st).astype(BF16)

        def scatter(ref, st, val, rows=rows):
            for a, s0 in enumerate(st):
                ref[pl.ds(s0, rows), :] = val[a * rows:(a + 1) * rows]

        def body(it, carry, bi=bi, gn=gn, gs=gs, n_groups=n_groups,
                 q_src=q_src, k_src=k_src, v_src=v_src):
            sg = it // n_groups
            n0 = (it % n_groups) * gn
            work = []
            for si in range(gs):
                r_sub = sg * gs + si
                st_prev = starts(r_sub, jnp.maximum(n0 - 1, 0))
                kb = [gather_half(k_src, st_prev)]
                vb = [gather_half(v_src, st_prev)]
                for j in range(gn):
                    st = starts(r_sub, n0 + j)
                    kb.append(gather_half(k_src, st))
                    vb.append(gather_half(v_src, st))
                    if j > 0:
                        bias = bias_ref[bi, 0]
                    elif n_groups == 1:
                        bias = bias_ref[bi, 1]
                    else:
                        bias = bias_ref[bi, jnp.where(n0 == 0, 1, 0)]
                    old = None if bi == 0 else (gather(m_ref, st), gather(l_ref, st),
                                                gather(acc_ref, st))
                    work.append((st, gather_half(q_src, st), kb[j], kb[j + 1],
                                 vb[j], vb[j + 1], bias, old))
            results = [(w[0],) + _attn_block(*w[1:]) for w in work]
            for st, m_new, l_new, a_new in results:
                scatter(m_ref, st, m_new)
                scatter(l_ref, st, l_new)
                scatter(acc_ref, st, a_new)
            return carry

        lax.fori_loop(0, (d // gs) * n_groups, body, 0)

    for lo in range(step):
        for hi in range(step):
            src = slice((lo * step + hi) * lp, (lo * step + hi + 1) * lp)
            tmp_ref[pl.ds(lo * lq + hi, lp, stride=step), :] = acc_ref[src, :] / l_ref[src, :]
    for lo in range(step):
        o_ref[pl.ds(lo, lq, stride=step), :] = tmp_ref[lo * lq:(lo + 1) * lq, :]


def _attention(qkv, batch, seq):
    dilations = tuple(d for _, d in DILATED_BRANCHES)
    for window, d in DILATED_BRANCHES:
        assert window // d == ATTN_BLOCK and seq % (d * ATTN_BLOCK) == 0
        assert ATTN_PERM % d == 0 and ATTN_BLOCK % (ATTN_PERM // d) == 0
    qkv3 = qkv.reshape(batch, seq, 3 * D_MODEL)
    bias = jnp.asarray(_attn_band_bias())
    spec = lambda part: pl.BlockSpec((None, seq, HEAD_DIM),
                                     lambda b, h, part=part: (b, 0, part * N_HEADS + h))
    kern = functools.partial(_attn_kernel, seq=seq, dilations=dilations)
    out = pl.pallas_call(
        kern,
        grid=(batch, N_HEADS),
        in_specs=[spec(0), spec(1), spec(2),
                  pl.BlockSpec(bias.shape, lambda b, h: (0, 0, 0, 0))],
        out_specs=pl.BlockSpec((None, seq, HEAD_DIM), lambda b, h: (b, 0, h)),
        out_shape=jax.ShapeDtypeStruct((batch, seq, D_MODEL), F32),
        scratch_shapes=[pltpu.VMEM((seq, HEAD_DIM), F32)] * 3 + [pltpu.VMEM((seq, HEAD_DIM), BF16)] * 3
                       + [pltpu.VMEM((seq, HEAD_DIM), F32)] * 4,
        compiler_params=_compiler_params(("parallel", "parallel")),
        name="attention",
    )(qkv3, qkv3, qkv3, bias)
    return out.reshape(batch * seq, D_MODEL)


def _matmul_rms_res_kernel(a_ref, w_ref, g_ref, h_ref, o_ref):
    f = jnp.dot(a_ref[...].astype(BF16), w_ref[...], preferred_element_type=F32)
    o_ref[...] = h_ref[...] + _rmsnorm(f, g_ref[...])


def _matmul_rms_res(a, w, li, g, h):
    m, k = a.shape
    n = w.shape[2]
    tm = RES_TM
    assert m % tm == 0
    return pl.pallas_call(
        _matmul_rms_res_kernel,
        grid=(m // tm,),
        in_specs=[pl.BlockSpec((tm, k), lambda i: (i, 0)),
                  pl.BlockSpec((None, k, n), lambda i: (li, 0, 0)),
                  pl.BlockSpec((1, n), lambda i: (0, 0)),
                  pl.BlockSpec((tm, n), lambda i: (i, 0))],
        out_specs=pl.BlockSpec((tm, n), lambda i: (i, 0)),
        out_shape=jax.ShapeDtypeStruct((m, n), F32),
        compiler_params=_compiler_params(("parallel",)),
        name="matmul_rms_res",
    )(a, w, g, h)


def _mlp_kernel(x_ref, g_pre_ref, w_up_ref, w_down_ref, g_post_ref, *rest, n_cast):
    cast_in = rest[:n_cast]
    o_ref = rest[n_cast]
    cast_out = rest[n_cast + 1:2 * n_cast + 1]
    hn_ref = rest[2 * n_cast + 1]
    k = pl.program_id(1)

    @pl.when(k == 0)
    def _():
        hn_ref[...] = _rmsnorm(x_ref[...], g_pre_ref[...]).astype(BF16)
        o_ref[...] = jnp.zeros_like(o_ref)

    sub = MLP_SUB
    _cast_chunks(cast_in, cast_out)
    for c in range(w_up_ref.shape[1] // sub):
        ffn = pl.ds(c * sub, sub)
        u = jnp.dot(hn_ref[...], w_up_ref[:, ffn], preferred_element_type=F32)
        u = jnp.square(jnp.maximum(u, 0.0)).astype(BF16)
        for n in range(o_ref.shape[1] // sub):
            cols = pl.ds(n * sub, sub)
            o_ref[:, cols] += jnp.dot(u, w_down_ref[ffn, cols], preferred_element_type=F32)

    @pl.when(k == pl.num_programs(1) - 1)
    def _():
        def finish_rows(rows):
            o_ref[rows, :] = x_ref[rows, :] + _rmsnorm(o_ref[rows, :], g_post_ref[...])
        _for_row_chunks(o_ref.shape[0], finish_rows)


def _mlp(h, g_pre, w_up, w_down, g_post, cast=(), cast_li=0):
    m, d = h.shape
    f = w_up.shape[1]
    tm, tf = MLP_TM, MLP_TF
    assert m % tm == 0 and f % tf == 0
    cast_in_specs, cast_out_specs, cast_out_shapes = _weight_cast_plan(
        cast, cast_li, m // tm, f // tf, MLP_CAST_STEPS)
    outs = pl.pallas_call(
        functools.partial(_mlp_kernel, n_cast=len(cast)),
        grid=(m // tm, f // tf),
        in_specs=[pl.BlockSpec((tm, d), lambda i, k: (i, 0)),
                  pl.BlockSpec((1, d), lambda i, k: (0, 0)),
                  pl.BlockSpec((d, tf), lambda i, k: (0, k)),
                  pl.BlockSpec((tf, d), lambda i, k: (k, 0)),
                  pl.BlockSpec((1, d), lambda i, k: (0, 0)),
                  *cast_in_specs],
        out_specs=[pl.BlockSpec((tm, d), lambda i, k: (i, 0)), *cast_out_specs],
        out_shape=[jax.ShapeDtypeStruct((m, d), F32), *cast_out_shapes],
        scratch_shapes=[pltpu.VMEM((tm, d), BF16)],
        compiler_params=_compiler_params(("arbitrary", "arbitrary")),
        name="mlp",
    )(h, g_pre, w_up, w_down, g_post, *cast)
    return outs[0], outs[1:]


def _rope_tables(seq):
    half = HEAD_DIM // 2
    inv_freq = ROPE_THETA ** (-jnp.arange(half, dtype=F32) * 2.0 / HEAD_DIM)
    ang = jnp.arange(seq, dtype=jnp.int32).astype(F32)[:, None] * inv_freq[None, :]
    cos = jnp.cos(ang)
    sin = jnp.sin(ang)
    return jnp.concatenate([cos, cos], axis=-1), jnp.concatenate([-sin, sin], axis=-1)


def kernel(x, norm_mix_pre, norm_mix_post, norm_mlp_pre, norm_mlp_post, w_in_ab, w_spatial,
           b_spatial, conv_w, w_out_ab, w_qkv, w_o, w_up, w_down):
    batch, seq, d = x.shape
    assert d == D_MODEL
    depth = norm_mix_pre.shape[0]
    m = batch * seq
    h = x.reshape(m, d)
    cos_tab, sin_tab = _rope_tables(seq)
    tiles_per_seq = seq // RMS_MATMUL_TM
    rope_specs = [pl.BlockSpec((RMS_MATMUL_TM, HEAD_DIM), lambda i, j: (i % tiles_per_seq, 0))] * 2

    w_in_bf, w_out_bf, w_qkv_bf, w_o_bf = (w.astype(BF16) for w in (w_in_ab, w_out_ab, w_qkv, w_o))

    for layer in range(depth):
        g_pre = norm_mix_pre[layer][None, :]
        g_post = norm_mix_post[layer][None, :]
        if layer % 2 == 0:
            e = layer // 2
            first_cast = (w_up, w_down) if layer == 0 else ()
            proj, first_bf = _rms_matmul(h, g_pre, w_in_bf, e, _gelu_epilogue, 2 * A_WIDTH,
                                         W_IN_TN, cast=first_cast, cast_li=layer)
            if layer == 0:
                w_up_bf, w_down_bf = first_bf
            bias_full = jnp.repeat(b_spatial[e].T, A_GROUP_DIM, axis=1)
            h = _mixer(proj, w_spatial[e], bias_full, conv_w[e], w_out_bf, e, g_post, h, seq)
        else:
            o = layer // 2
            qkv, _ = _rms_matmul(h, g_pre, w_qkv_bf, o, _rope_epilogue, 2 * D_MODEL, W_QKV_TN,
                                 extra=(cos_tab, sin_tab), extra_specs=rope_specs)
            att = _attention(qkv, batch, seq)
            h = _matmul_rms_res(att, w_o_bf, o, g_post, h)
        next_cast = (w_up, w_down) if layer + 1 < depth else ()
        h, next_bf = _mlp(h, norm_mlp_pre[layer][None, :], w_up_bf, w_down_bf,
                          norm_mlp_post[layer][None, :], cast=next_cast, cast_li=layer + 1)
        if next_bf:
            w_up_bf, w_down_bf = next_bf
    return h.reshape(batch, seq, d)
```

```python
import functools
import math

import jax
import jax.numpy as jnp
import numpy as np
from jax import lax
from jax.experimental import pallas as pl
from jax.experimental.pallas import tpu as pltpu

F32 = jnp.float32
BF16 = jnp.bfloat16

D_MODEL = 2048
A_WIDTH = D_MODEL // 2
B_WIDTH = D_MODEL - A_WIDTH
A_GROUPS = 8
A_GROUP_DIM = A_WIDTH // A_GROUPS
CHUNK = 128
CONV_WIDTH = 3
HEAD_DIM = 128
N_HEADS = D_MODEL // HEAD_DIM
DILATED_BRANCHES = ((128, 1), (512, 4), (2048, 16))
ATTN_BLOCK = 128
ATTN_PERM = 16
ATTN_PERM_STEP = 4
ATTN_GROUP = 32
ROPE_THETA = 10000.0
FFN_DIM = 4 * D_MODEL
RMS_EPS = 1e-6
LN_EPS = 1e-5

V7X_LANES = 128
BF16_SUBLANES = 16
V7X_VMEM_LIMIT_BYTES = 60 * 1024 * 1024

RMS_MATMUL_TM = 1024
W_QKV_TN = 2048
RMS_MATMUL_SUB_TN = 512
EVEN_MIXER_TM = 256
RES_TM = 512
MLP_TM = 1024
MLP_TF = 1024
MLP_SUB = 512
CONV_HALO = 8
NORM_ROW_CHUNK = 16
RMS_MATMUL_CAST_STEPS = 64
MLP_CAST_STEPS = 128


def _compiler_params(semantics):
    return pltpu.CompilerParams(dimension_semantics=semantics,
                                vmem_limit_bytes=V7X_VMEM_LIMIT_BYTES)


def _rmsnorm(x, g):
    ms = jnp.mean(x * x, axis=-1, keepdims=True)
    return (x * lax.rsqrt(ms + RMS_EPS)) * g


def _for_row_chunks(n_rows, fn):
    for c in range(n_rows // NORM_ROW_CHUNK):
        fn(pl.ds(c * NORM_ROW_CHUNK, NORM_ROW_CHUNK))


def _weight_cast_plan(cast, cast_li, n_outer, n_inner, max_steps):
    steps = min(max_steps, 2 ** int(math.log2(n_outer * n_inner)))
    chunk = lambda i, j: jnp.minimum(i * n_inner + j, steps - 1)
    in_specs, out_specs, out_shapes = [], [], []
    for c in cast:
        _, rows, width = c.shape
        assert rows % (steps * BF16_SUBLANES) == 0
        rows_per_step = rows // steps
        in_specs.append(pl.BlockSpec((None, rows_per_step, width),
                                     lambda i, j: (cast_li, chunk(i, j), 0)))
        out_specs.append(pl.BlockSpec((rows_per_step, width), lambda i, j: (chunk(i, j), 0)))
        out_shapes.append(jax.ShapeDtypeStruct((rows, width), BF16))
    return in_specs, out_specs, out_shapes


def _cast_chunks(cast_in, cast_out):
    for src, dst in zip(cast_in, cast_out):
        dst[...] = src[...].astype(BF16)


def _rms_matmul_kernel(x_ref, g_ref, w_ref, *rest, epilogue, n_special, n_extra, n_cast):
    extra_refs = rest[:n_extra]
    cast_in = rest[n_extra:n_extra + n_cast]
    o_ref = rest[n_extra + n_cast]
    cast_out = rest[n_extra + n_cast + 1:n_extra + 2 * n_cast + 1]
    hn_ref = rest[n_extra + 2 * n_cast + 1]
    j = pl.program_id(1)

    @pl.when(j == 0)
    def _():
        hn_ref[...] = _rmsnorm(x_ref[...], g_ref[...]).astype(BF16)

    sub = RMS_MATMUL_SUB_TN
    cols = [pl.ds(c * sub, sub) for c in range(w_ref.shape[1] // sub)]

    def product(col):
        return jnp.dot(hn_ref[...], w_ref[:, col], preferred_element_type=F32)

    def cast_weight_chunks():
        _cast_chunks(cast_in, cast_out)

    @pl.when(j < n_special)
    def _():
        cast_weight_chunks()
        for col in cols:
            epilogue(product(col), extra_refs, o_ref.at[:, col])

    @pl.when(j >= n_special)
    def _():
        cast_weight_chunks()
        for col in cols:
            o_ref[:, col] = product(col)


def _rms_matmul(x, g, w, li, epilogue, special_cols, tn, extra=(), extra_specs=(), cast=(),
                cast_li=0):
    m, k = x.shape
    n = w.shape[2]
    tm = RMS_MATMUL_TM
    assert m % tm == 0 and n % tn == 0 and special_cols % tn == 0
    n_col_steps = n // tn
    kern = functools.partial(_rms_matmul_kernel, epilogue=epilogue, n_special=special_cols // tn,
                             n_extra=len(extra), n_cast=len(cast))
    cast_in_specs, cast_out_specs, cast_out_shapes = _weight_cast_plan(
        cast, cast_li, m // tm, n_col_steps, RMS_MATMUL_CAST_STEPS)
    outs = pl.pallas_call(
        kern,
        grid=(m // tm, n_col_steps),
        in_specs=[pl.BlockSpec((tm, k), lambda i, j: (i, 0)),
                  pl.BlockSpec((1, k), lambda i, j: (0, 0)),
                  pl.BlockSpec((None, k, tn), lambda i, j: (li, 0, j)),
                  *extra_specs, *cast_in_specs],
        out_specs=[pl.BlockSpec((tm, tn), lambda i, j: (i, j)), *cast_out_specs],
        out_shape=[jax.ShapeDtypeStruct((m, n), F32), *cast_out_shapes],
        scratch_shapes=[pltpu.VMEM((tm, k), BF16)],
        compiler_params=_compiler_params(("arbitrary", "arbitrary")),
        name="rms_matmul",
    )(x, g, w, *extra, *cast)
    return outs[0], outs[1:]


def _rope_epilogue(p, extra_refs, o_ref):
    cos_ref, sin_ref = extra_refs
    cos = cos_ref[...]
    sin = sin_ref[...]
    for hh in range(p.shape[1] // HEAD_DIM):
        t = p[:, hh * HEAD_DIM:(hh + 1) * HEAD_DIM]
        o_ref[:, hh * HEAD_DIM:(hh + 1) * HEAD_DIM] = (
            t * cos + pltpu.roll(t, HEAD_DIM // 2, 1) * sin)


def _even_mixer_kernel(x_ref, g_pre_ref, w_in_ref, ws_ref, bias_ref, cw_ref, w_out_ref,
                       g_post_ref, *rest, tiles_per_seq, n_cast):
    cast_in = rest[:n_cast]
    o_ref = rest[n_cast]
    cast_out = rest[n_cast + 1:2 * n_cast + 1]
    proj_ref, z_ref = rest[2 * n_cast + 1:]
    tm = x_ref.shape[0]
    i = pl.program_id(0)

    @pl.when(i == 0)
    def _():
        z_ref[...] = jnp.zeros_like(z_ref)

    _cast_chunks(cast_in, cast_out)

    hn = _rmsnorm(x_ref[...], g_pre_ref[...]).astype(BF16)
    sub = RMS_MATMUL_SUB_TN
    for c in range(w_in_ref.shape[1] // sub):
        cols = pl.ds(c * sub, sub)
        p = jnp.dot(hn, w_in_ref[:, cols], preferred_element_type=F32)
        proj_ref[:, cols] = jax.nn.gelu(p) if (c + 1) * sub <= 2 * A_WIDTH else p
    au_ref, av_ref, gb_ref, gc_ref, bx_ref = (
        proj_ref.at[:, pl.ds(s * A_WIDTH, A_WIDTH)] for s in range(5))

    av = av_ref[...]
    mu = jnp.mean(av, axis=-1, keepdims=True)
    cen = av - mu
    var = jnp.mean(cen * cen, axis=-1, keepdims=True)
    vn = (cen * lax.rsqrt(var + LN_EPS)).astype(BF16)
    row = lax.broadcasted_iota(jnp.int32, (CHUNK, CHUNK), 0)
    col = lax.broadcasted_iota(jnp.int32, (CHUNK, CHUNK), 1)
    causal = col <= row
    a_cols = []
    for g in range(A_GROUPS):
        cs = slice(g * A_GROUP_DIM, (g + 1) * A_GROUP_DIM)
        w_causal = jnp.where(causal, ws_ref[g], 0.0).astype(BF16)
        chunks = []
        for c in range(tm // CHUNK):
            rs = slice(c * CHUNK, (c + 1) * CHUNK)
            mixed = jnp.dot(w_causal, vn[rs, cs], preferred_element_type=F32) + bias_ref[:, cs]
            chunks.append((au_ref[rs, cs] * mixed).astype(BF16))
        a_cols.append(jnp.concatenate(chunks, axis=0))

    z = gc_ref[...] * bx_ref[...]
    at_seq_start = (i % tiles_per_seq) == 0
    z_before = jnp.where(at_seq_start, 0.0, z_ref[tm:tm + CONV_HALO, :])
    z_ref[0:CONV_HALO, :] = z_before
    z_ref[CONV_HALO:CONV_HALO + tm, :] = z
    y = (cw_ref[2:3, :] * z
         + cw_ref[1:2, :] * z_ref[CONV_HALO - 1:CONV_HALO - 1 + tm, :]
         + cw_ref[0:1, :] * z_ref[CONV_HALO - 2:CONV_HALO - 2 + tm, :])
    b_out = (gb_ref[...] * y).astype(BF16)

    mixed_ab = jnp.concatenate(a_cols + [b_out], axis=1)
    f = jnp.dot(mixed_ab, w_out_ref[...], preferred_element_type=F32)
    o_ref[...] = x_ref[...] + _rmsnorm(f, g_post_ref[...])


def _even_mixer(h, g_pre, w_in, w_spatial, bias_full, conv_w, w_out, li, g_post, seq,
                cast=(), cast_li=0):
    m, d = h.shape
    n_in = w_in.shape[2]
    tm = EVEN_MIXER_TM
    assert seq % tm == 0 and tm % CHUNK == 0 and A_WIDTH == B_WIDTH
    assert n_in == 2 * A_WIDTH + 3 * B_WIDTH and (2 * A_WIDTH) % RMS_MATMUL_SUB_TN == 0
    cast_in_specs, cast_out_specs, cast_out_shapes = _weight_cast_plan(
        cast, cast_li, m // tm, 1, RMS_MATMUL_CAST_STEPS)
    const = lambda shape: pl.BlockSpec(shape, lambda i, j: (0,) * len(shape))
    resident = lambda rows, cols: pl.BlockSpec((None, rows, cols), lambda i, j: (li, 0, 0),
                                               pipeline_mode=pl.Buffered(1))
    kern = functools.partial(_even_mixer_kernel, tiles_per_seq=seq // tm, n_cast=len(cast))
    outs = pl.pallas_call(
        kern,
        grid=(m // tm, 1),
        in_specs=[pl.BlockSpec((tm, d), lambda i, j: (i, 0)),
                  const((1, d)),
                  resident(d, n_in),
                  const((A_GROUPS, CHUNK, CHUNK)),
                  const((CHUNK, A_WIDTH)),
                  const((CONV_WIDTH, B_WIDTH)),
                  resident(d, d),
                  const((1, d)),
                  *cast_in_specs],
        out_specs=[pl.BlockSpec((tm, d), lambda i, j: (i, 0)), *cast_out_specs],
        out_shape=[jax.ShapeDtypeStruct((m, d), F32), *cast_out_shapes],
        scratch_shapes=[pltpu.VMEM((tm, n_in), F32),
                        pltpu.VMEM((CONV_HALO + tm, B_WIDTH), F32)],
        compiler_params=_compiler_params(("arbitrary", "arbitrary")),
        name="even_mixer",
    )(h, g_pre, w_in, w_spatial, bias_full, conv_w, w_out, g_post, *cast)
    return outs[0], outs[1:]


def _attn_band_bias():
    blk = ATTN_BLOCK
    tables = []
    for _, d in DILATED_BRANCHES:
        pieces = ATTN_PERM // d
        rows = blk // pieces
        i = np.arange(blk)
        run = i // rows
        if pieces == ATTN_PERM:
            run = ATTN_PERM_STEP * (run % ATTN_PERM_STEP) + run // ATTN_PERM_STEP
        uq = pieces * (i % rows) + run
        uk = np.concatenate([uq, blk + uq])
        step = uq[:, None] + blk - uk[None, :]
        valid = (step >= 0) & (step <= blk)
        first = valid & (uk[None, :] >= blk)
        tables.append(np.stack([np.where(valid, 0.0, -np.inf), np.where(first, 0.0, -np.inf)]))
    return np.stack(tables).astype(np.float32)


def _attn_block(qb, k_prev, k_cur, v_prev, v_cur, bias, old):
    kw = jnp.concatenate([k_prev, k_cur], axis=0)
    vw = jnp.concatenate([v_prev, v_cur], axis=0)
    s = lax.dot_general(qb, kw, (((1,), (1,)), ((), ())), preferred_element_type=F32) + bias
    m_blk = jnp.max(s, axis=-1, keepdims=True)
    m_new = jnp.broadcast_to(m_blk, (s.shape[0], HEAD_DIM))
    if old is not None:
        m_new = jnp.maximum(old[0], m_new)
    p = jnp.exp2(s - jnp.concatenate([m_new, m_new], axis=1)).astype(BF16)
    v_ones = jnp.concatenate([vw, jnp.ones_like(vw)], axis=1)
    pv = jnp.dot(p, v_ones, preferred_element_type=F32)
    a_new, l_new = pv[:, :HEAD_DIM], pv[:, HEAD_DIM:]
    if old is not None:
        corr = jnp.exp2(old[0] - m_new)
        l_new = old[1] * corr + l_new
        a_new = old[2] * corr + a_new
    return m_new, l_new, a_new


def _attn_kernel(q_ref, k_ref, v_ref, bias_ref, o_ref, qp_ref, kp_ref, vp_ref,
                 qh_ref, kh_ref, vh_ref, acc_ref, m_ref, l_ref, tmp_ref, *, seq, dilations):
    blk = ATTN_BLOCK
    lp = seq // ATTN_PERM
    step = ATTN_PERM_STEP
    lq = seq // step
    q_scale = HEAD_DIM ** -0.5 * math.log2(math.e)
    for src_ref, dst_ref, half_ref, scale in ((q_ref, qp_ref, qh_ref, q_scale),
                                              (k_ref, kp_ref, kh_ref, None),
                                              (v_ref, vp_ref, vh_ref, None)):
        for lo in range(step):
            tmp_ref[lo * lq:(lo + 1) * lq, :] = src_ref[pl.ds(lo, lq, stride=step), :]
        for lo in range(step):
            for hi in range(step):
                run = lo * step + hi
                rows = tmp_ref[pl.ds(lo * lq + hi, lp, stride=step), :]
                if scale is not None:
                    rows = rows * scale
                dst_ref[run * lp:(run + 1) * lp, :] = rows
                half_ref[run * lp:(run + 1) * lp, :] = rows.astype(BF16)

    for bi, d in enumerate(dilations):
        pieces = ATTN_PERM // d
        rows = blk // pieces
        nb = lp // rows
        gn = min(nb, ATTN_GROUP)
        gs = min(ATTN_GROUP // gn, d)
        n_groups = nb // gn
        assert nb % gn == 0 and d % gs == 0
        packed = rows % BF16_SUBLANES == 0
        q_src, k_src, v_src = (qh_ref, kh_ref, vh_ref) if packed else (qp_ref, kp_ref, vp_ref)

        def starts(r_sub, n, d=d, pieces=pieces, rows=rows):
            return [pl.multiple_of((r_sub * pieces + a) * lp + n * rows, rows) for a in range(pieces)]

        def gather(ref, st, rows=rows):
            parts = [ref[pl.ds(s0, rows), :] for s0 in st]
            return parts[0] if len(parts) == 1 else jnp.concatenate(parts, axis=0)

        def gather_half(ref, st):
            return gather(ref, st).astype(BF16)

        def scatter(ref, st, val, rows=rows):
            for a, s0 in enumerate(st):
                ref[pl.ds(s0, rows), :] = val[a * rows:(a + 1) * rows]

        def body(it, carry, bi=bi, gn=gn, gs=gs, n_groups=n_groups,
                 q_src=q_src, k_src=k_src, v_src=v_src):
            sg = it // n_groups
            n0 = (it % n_groups) * gn
            work = []
            for si in range(gs):
                r_sub = sg * gs + si
                st_prev = starts(r_sub, jnp.maximum(n0 - 1, 0))
                kb = [gather_half(k_src, st_prev)]
                vb = [gather_half(v_src, st_prev)]
                for j in range(gn):
                    st = starts(r_sub, n0 + j)
                    kb.append(gather_half(k_src, st))
                    vb.append(gather_half(v_src, st))
                    if j > 0:
                        bias = bias_ref[bi, 0]
                    elif n_groups == 1:
                        bias = bias_ref[bi, 1]
                    else:
                        bias = bias_ref[bi, jnp.where(n0 == 0, 1, 0)]
                    old = None if bi == 0 else (gather(m_ref, st), gather(l_ref, st),
                                                gather(acc_ref, st))
                    work.append((st, gather_half(q_src, st), kb[j], kb[j + 1],
                                 vb[j], vb[j + 1], bias, old))
            results = [(w[0],) + _attn_block(*w[1:]) for w in work]
            for st, m_new, l_new, a_new in results:
                scatter(m_ref, st, m_new)
                scatter(l_ref, st, l_new)
                scatter(acc_ref, st, a_new)
            return carry

        lax.fori_loop(0, (d // gs) * n_groups, body, 0)

    for lo in range(step):
        for hi in range(step):
            src = slice((lo * step + hi) * lp, (lo * step + hi + 1) * lp)
            tmp_ref[pl.ds(lo * lq + hi, lp, stride=step), :] = acc_ref[src, :] / l_ref[src, :]
    for lo in range(step):
        o_ref[pl.ds(lo, lq, stride=step), :] = tmp_ref[lo * lq:(lo + 1) * lq, :]


def _attention(qkv, batch, seq):
    dilations = tuple(d for _, d in DILATED_BRANCHES)
    for window, d in DILATED_BRANCHES:
        assert window // d == ATTN_BLOCK and seq % (d * ATTN_BLOCK) == 0
        assert ATTN_PERM % d == 0 and ATTN_BLOCK % (ATTN_PERM // d) == 0
    qkv3 = qkv.reshape(batch, seq, 3 * D_MODEL)
    bias = jnp.asarray(_attn_band_bias())
    spec = lambda part: pl.BlockSpec((None, seq, HEAD_DIM),
                                     lambda b, h, part=part: (b, 0, part * N_HEADS + h))
    kern = functools.partial(_attn_kernel, seq=seq, dilations=dilations)
    out = pl.pallas_call(
        kern,
        grid=(batch, N_HEADS),
        in_specs=[spec(0), spec(1), spec(2),
                  pl.BlockSpec(bias.shape, lambda b, h: (0, 0, 0, 0))],
        out_specs=pl.BlockSpec((None, seq, HEAD_DIM), lambda b, h: (b, 0, h)),
        out_shape=jax.ShapeDtypeStruct((batch, seq, D_MODEL), F32),
        scratch_shapes=[pltpu.VMEM((seq, HEAD_DIM), F32)] * 3 + [pltpu.VMEM((seq, HEAD_DIM), BF16)] * 3
                       + [pltpu.VMEM((seq, HEAD_DIM), F32)] * 4,
        compiler_params=_compiler_params(("parallel", "parallel")),
        name="attention",
    )(qkv3, qkv3, qkv3, bias)
    return out.reshape(batch * seq, D_MODEL)


def _matmul_rms_res_kernel(a_ref, w_ref, g_ref, h_ref, o_ref):
    f = jnp.dot(a_ref[...].astype(BF16), w_ref[...], preferred_element_type=F32)
    o_ref[...] = h_ref[...] + _rmsnorm(f, g_ref[...])


def _matmul_rms_res(a, w, li, g, h):
    m, k = a.shape
    n = w.shape[2]
    tm = RES_TM
    assert m % tm == 0
    return pl.pallas_call(
        _matmul_rms_res_kernel,
        grid=(m // tm,),
        in_specs=[pl.BlockSpec((tm, k), lambda i: (i, 0)),
                  pl.BlockSpec((None, k, n), lambda i: (li, 0, 0)),
                  pl.BlockSpec((1, n), lambda i: (0, 0)),
                  pl.BlockSpec((tm, n), lambda i: (i, 0))],
        out_specs=pl.BlockSpec((tm, n), lambda i: (i, 0)),
        out_shape=jax.ShapeDtypeStruct((m, n), F32),
        compiler_params=_compiler_params(("parallel",)),
        name="matmul_rms_res",
    )(a, w, g, h)


def _mlp_kernel(x_ref, g_pre_ref, w_up_ref, w_down_ref, g_post_ref, *rest, n_cast):
    cast_in = rest[:n_cast]
    o_ref = rest[n_cast]
    cast_out = rest[n_cast + 1:2 * n_cast + 1]
    hn_ref = rest[2 * n_cast + 1]
    k = pl.program_id(1)

    @pl.when(k == 0)
    def _():
        hn_ref[...] = _rmsnorm(x_ref[...], g_pre_ref[...]).astype(BF16)
        o_ref[...] = jnp.zeros_like(o_ref)

    sub = MLP_SUB
    _cast_chunks(cast_in, cast_out)
    for c in range(w_up_ref.shape[1] // sub):
        ffn = pl.ds(c * sub, sub)
        u = jnp.dot(hn_ref[...], w_up_ref[:, ffn], preferred_element_type=F32)
        u = jnp.square(jnp.maximum(u, 0.0)).astype(BF16)
        for n in range(o_ref.shape[1] // sub):
            cols = pl.ds(n * sub, sub)
            o_ref[:, cols] += jnp.dot(u, w_down_ref[ffn, cols], preferred_element_type=F32)

    @pl.when(k == pl.num_programs(1) - 1)
    def _():
        def finish_rows(rows):
            o_ref[rows, :] = x_ref[rows, :] + _rmsnorm(o_ref[rows, :], g_post_ref[...])
        _for_row_chunks(o_ref.shape[0], finish_rows)


def _mlp(h, g_pre, w_up, w_down, g_post, cast=(), cast_li=0):
    m, d = h.shape
    f = w_up.shape[1]
    tm, tf = MLP_TM, MLP_TF
    assert m % tm == 0 and f % tf == 0
    cast_in_specs, cast_out_specs, cast_out_shapes = _weight_cast_plan(
        cast, cast_li, m // tm, f // tf, MLP_CAST_STEPS)
    outs = pl.pallas_call(
        functools.partial(_mlp_kernel, n_cast=len(cast)),
        grid=(m // tm, f // tf),
        in_specs=[pl.BlockSpec((tm, d), lambda i, k: (i, 0)),
                  pl.BlockSpec((1, d), lambda i, k: (0, 0)),
                  pl.BlockSpec((d, tf), lambda i, k: (0, k)),
                  pl.BlockSpec((tf, d), lambda i, k: (k, 0)),
                  pl.BlockSpec((1, d), lambda i, k: (0, 0)),
                  *cast_in_specs],
        out_specs=[pl.BlockSpec((tm, d), lambda i, k: (i, 0)), *cast_out_specs],
        out_shape=[jax.ShapeDtypeStruct((m, d), F32), *cast_out_shapes],
        scratch_shapes=[pltpu.VMEM((tm, d), BF16)],
        compiler_params=_compiler_params(("arbitrary", "arbitrary")),
        name="mlp",
    )(h, g_pre, w_up, w_down, g_post, *cast)
    return outs[0], outs[1:]


def _rope_tables(seq):
    half = HEAD_DIM // 2
    inv_freq = ROPE_THETA ** (-jnp.arange(half, dtype=F32) * 2.0 / HEAD_DIM)
    ang = jnp.arange(seq, dtype=jnp.int32).astype(F32)[:, None] * inv_freq[None, :]
    cos = jnp.cos(ang)
    sin = jnp.sin(ang)
    return jnp.concatenate([cos, cos], axis=-1), jnp.concatenate([-sin, sin], axis=-1)


def kernel(x, norm_mix_pre, norm_mix_post, norm_mlp_pre, norm_mlp_post, w_in_ab, w_spatial,
           b_spatial, conv_w, w_out_ab, w_qkv, w_o, w_up, w_down):
    batch, seq, d = x.shape
    assert d == D_MODEL
    depth = norm_mix_pre.shape[0]
    m = batch * seq
    h = x.reshape(m, d)
    cos_tab, sin_tab = _rope_tables(seq)
    tiles_per_seq = seq // RMS_MATMUL_TM
    rope_specs = [pl.BlockSpec((RMS_MATMUL_TM, HEAD_DIM), lambda i, j: (i % tiles_per_seq, 0))] * 2

    w_in_bf, w_out_bf, w_qkv_bf, w_o_bf = (w.astype(BF16) for w in (w_in_ab, w_out_ab, w_qkv, w_o))

    for layer in range(depth):
        g_pre = norm_mix_pre[layer][None, :]
        g_post = norm_mix_post[layer][None, :]
        if layer % 2 == 0:
            e = layer // 2
            first_cast = (w_up, w_down) if layer == 0 else ()
            bias_full = jnp.repeat(b_spatial[e].T, A_GROUP_DIM, axis=1)
            h, first_bf = _even_mixer(h, g_pre, w_in_bf, w_spatial[e], bias_full, conv_w[e],
                                      w_out_bf, e, g_post, seq, cast=first_cast, cast_li=layer)
            if layer == 0:
                w_up_bf, w_down_bf = first_bf
        else:
            o = layer // 2
            qkv, _ = _rms_matmul(h, g_pre, w_qkv_bf, o, _rope_epilogue, 2 * D_MODEL, W_QKV_TN,
                                 extra=(cos_tab, sin_tab), extra_specs=rope_specs)
            att = _attention(qkv, batch, seq)
            h = _matmul_rms_res(att, w_o_bf, o, g_post, h)
        next_cast = (w_up, w_down) if layer + 1 < depth else ()
        h, next_bf = _mlp(h, norm_mlp_pre[layer][None, :], w_up_bf, w_down_bf,
                          norm_mlp_post[layer][None, :], cast=next_cast, cast_li=layer + 1)
        if next_bf:
            w_up_bf, w_down_bf = next_bf
    return h.reshape(batch, seq, d)
```

```python
import functools
import math

import jax
import jax.numpy as jnp
import numpy as np
from jax import lax
from jax.experimental import pallas as pl
from jax.experimental.pallas import tpu as pltpu

F32 = jnp.float32
BF16 = jnp.bfloat16

D_MODEL = 2048
A_WIDTH = D_MODEL // 2
B_WIDTH = D_MODEL - A_WIDTH
A_GROUPS = 8
A_GROUP_DIM = A_WIDTH // A_GROUPS
CHUNK = 128
CONV_WIDTH = 3
HEAD_DIM = 128
N_HEADS = D_MODEL // HEAD_DIM
DILATED_BRANCHES = ((128, 1), (512, 4), (2048, 16))
ATTN_BLOCK = 128
ATTN_PERM = 16
ATTN_PERM_STEP = 4
ATTN_GROUP = 32
ROPE_THETA = 10000.0
FFN_DIM = 4 * D_MODEL
RMS_EPS = 1e-6
LN_EPS = 1e-5

V7X_LANES = 128
BF16_SUBLANES = 16
V7X_VMEM_LIMIT_BYTES = 60 * 1024 * 1024

QKV_PROJ_TM = 256
RMS_MATMUL_SUB_TN = 512
EVEN_MIXER_TM = 256
RES_TM = 512
MLP_TM = 1024
MLP_TF = 1024
MLP_SUB = 512
CONV_HALO = 8
NORM_ROW_CHUNK = 16
RMS_MATMUL_CAST_STEPS = 64
MLP_CAST_STEPS = 128


def _compiler_params(semantics):
    return pltpu.CompilerParams(dimension_semantics=semantics,
                                vmem_limit_bytes=V7X_VMEM_LIMIT_BYTES)


def _rmsnorm(x, g):
    ms = jnp.mean(x * x, axis=-1, keepdims=True)
    return (x * lax.rsqrt(ms + RMS_EPS)) * g


def _for_row_chunks(n_rows, fn):
    for c in range(n_rows // NORM_ROW_CHUNK):
        fn(pl.ds(c * NORM_ROW_CHUNK, NORM_ROW_CHUNK))


def _weight_cast_plan(cast, cast_li, n_outer, n_inner, max_steps):
    steps = min(max_steps, 2 ** int(math.log2(n_outer * n_inner)))
    chunk = lambda i, j: jnp.minimum(i * n_inner + j, steps - 1)
    in_specs, out_specs, out_shapes = [], [], []
    for c in cast:
        _, rows, width = c.shape
        assert rows % (steps * BF16_SUBLANES) == 0
        rows_per_step = rows // steps
        in_specs.append(pl.BlockSpec((None, rows_per_step, width),
                                     lambda i, j: (cast_li, chunk(i, j), 0)))
        out_specs.append(pl.BlockSpec((rows_per_step, width), lambda i, j: (chunk(i, j), 0)))
        out_shapes.append(jax.ShapeDtypeStruct((rows, width), BF16))
    return in_specs, out_specs, out_shapes


def _cast_chunks(cast_in, cast_out):
    for src, dst in zip(cast_in, cast_out):
        dst[...] = src[...].astype(BF16)


def _qkv_proj_kernel(x_ref, g_ref, w_ref, cos_ref, sin_ref, o_ref):
    hn = _rmsnorm(x_ref[...], g_ref[...]).astype(BF16)
    cos = cos_ref[...]
    sin = sin_ref[...]
    sub = RMS_MATMUL_SUB_TN
    for c in range(w_ref.shape[1] // sub):
        p = jnp.dot(hn, w_ref[:, pl.ds(c * sub, sub)], preferred_element_type=F32)
        if (c + 1) * sub <= 2 * D_MODEL:
            for hh in range(sub // HEAD_DIM):
                t = p[:, hh * HEAD_DIM:(hh + 1) * HEAD_DIM]
                o_ref[:, pl.ds(c * sub + hh * HEAD_DIM, HEAD_DIM)] = (
                    t * cos + pltpu.roll(t, HEAD_DIM // 2, 1) * sin)
        else:
            o_ref[:, pl.ds(c * sub, sub)] = p


def _qkv_proj(x, g, w, li, cos_tab, sin_tab, seq):
    m, k = x.shape
    n = w.shape[2]
    tm = QKV_PROJ_TM
    assert m % tm == 0 and seq % tm == 0 and (2 * D_MODEL) % RMS_MATMUL_SUB_TN == 0
    tiles_per_seq = seq // tm
    table = pl.BlockSpec((tm, HEAD_DIM), lambda i: (i % tiles_per_seq, 0))
    return pl.pallas_call(
        _qkv_proj_kernel,
        grid=(m // tm,),
        in_specs=[pl.BlockSpec((tm, k), lambda i: (i, 0)),
                  pl.BlockSpec((1, k), lambda i: (0, 0)),
                  pl.BlockSpec((None, k, n), lambda i: (li, 0, 0), pipeline_mode=pl.Buffered(1)),
                  table, table],
        out_specs=pl.BlockSpec((tm, n), lambda i: (i, 0)),
        out_shape=jax.ShapeDtypeStruct((m, n), F32),
        compiler_params=_compiler_params(("parallel",)),
        name="qkv_proj",
    )(x, g, w, cos_tab, sin_tab)


def _even_mixer_kernel(x_ref, g_pre_ref, w_in_ref, ws_ref, bias_ref, cw_ref, w_out_ref,
                       g_post_ref, *rest, tiles_per_seq, n_cast):
    cast_in = rest[:n_cast]
    o_ref = rest[n_cast]
    cast_out = rest[n_cast + 1:2 * n_cast + 1]
    proj_ref, z_ref = rest[2 * n_cast + 1:]
    tm = x_ref.shape[0]
    i = pl.program_id(0)

    @pl.when(i == 0)
    def _():
        z_ref[...] = jnp.zeros_like(z_ref)

    _cast_chunks(cast_in, cast_out)

    hn = _rmsnorm(x_ref[...], g_pre_ref[...]).astype(BF16)
    sub = RMS_MATMUL_SUB_TN
    for c in range(w_in_ref.shape[1] // sub):
        cols = pl.ds(c * sub, sub)
        p = jnp.dot(hn, w_in_ref[:, cols], preferred_element_type=F32)
        proj_ref[:, cols] = jax.nn.gelu(p) if (c + 1) * sub <= 2 * A_WIDTH else p
    au_ref, av_ref, gb_ref, gc_ref, bx_ref = (
        proj_ref.at[:, pl.ds(s * A_WIDTH, A_WIDTH)] for s in range(5))

    av = av_ref[...]
    mu = jnp.mean(av, axis=-1, keepdims=True)
    cen = av - mu
    var = jnp.mean(cen * cen, axis=-1, keepdims=True)
    vn = (cen * lax.rsqrt(var + LN_EPS)).astype(BF16)
    row = lax.broadcasted_iota(jnp.int32, (CHUNK, CHUNK), 0)
    col = lax.broadcasted_iota(jnp.int32, (CHUNK, CHUNK), 1)
    causal = col <= row
    a_cols = []
    for g in range(A_GROUPS):
        cs = slice(g * A_GROUP_DIM, (g + 1) * A_GROUP_DIM)
        w_causal = jnp.where(causal, ws_ref[g], 0.0).astype(BF16)
        chunks = []
        for c in range(tm // CHUNK):
            rs = slice(c * CHUNK, (c + 1) * CHUNK)
            mixed = jnp.dot(w_causal, vn[rs, cs], preferred_element_type=F32) + bias_ref[:, cs]
            chunks.append((au_ref[rs, cs] * mixed).astype(BF16))
        a_cols.append(jnp.concatenate(chunks, axis=0))

    z = gc_ref[...] * bx_ref[...]
    at_seq_start = (i % tiles_per_seq) == 0
    z_before = jnp.where(at_seq_start, 0.0, z_ref[tm:tm + CONV_HALO, :])
    z_ref[0:CONV_HALO, :] = z_before
    z_ref[CONV_HALO:CONV_HALO + tm, :] = z
    y = (cw_ref[2:3, :] * z
         + cw_ref[1:2, :] * z_ref[CONV_HALO - 1:CONV_HALO - 1 + tm, :]
         + cw_ref[0:1, :] * z_ref[CONV_HALO - 2:CONV_HALO - 2 + tm, :])
    b_out = (gb_ref[...] * y).astype(BF16)

    mixed_ab = jnp.concatenate(a_cols + [b_out], axis=1)
    f = jnp.dot(mixed_ab, w_out_ref[...], preferred_element_type=F32)
    o_ref[...] = x_ref[...] + _rmsnorm(f, g_post_ref[...])


def _even_mixer(h, g_pre, w_in, w_spatial, bias_full, conv_w, w_out, li, g_post, seq,
                cast=(), cast_li=0):
    m, d = h.shape
    n_in = w_in.shape[2]
    tm = EVEN_MIXER_TM
    assert seq % tm == 0 and tm % CHUNK == 0 and A_WIDTH == B_WIDTH
    assert n_in == 2 * A_WIDTH + 3 * B_WIDTH and (2 * A_WIDTH) % RMS_MATMUL_SUB_TN == 0
    cast_in_specs, cast_out_specs, cast_out_shapes = _weight_cast_plan(
        cast, cast_li, m // tm, 1, RMS_MATMUL_CAST_STEPS)
    const = lambda shape: pl.BlockSpec(shape, lambda i, j: (0,) * len(shape))
    resident = lambda rows, cols: pl.BlockSpec((None, rows, cols), lambda i, j: (li, 0, 0),
                                               pipeline_mode=pl.Buffered(1))
    kern = functools.partial(_even_mixer_kernel, tiles_per_seq=seq // tm, n_cast=len(cast))
    outs = pl.pallas_call(
        kern,
        grid=(m // tm, 1),
        in_specs=[pl.BlockSpec((tm, d), lambda i, j: (i, 0)),
                  const((1, d)),
                  resident(d, n_in),
                  const((A_GROUPS, CHUNK, CHUNK)),
                  const((CHUNK, A_WIDTH)),
                  const((CONV_WIDTH, B_WIDTH)),
                  resident(d, d),
                  const((1, d)),
                  *cast_in_specs],
        out_specs=[pl.BlockSpec((tm, d), lambda i, j: (i, 0)), *cast_out_specs],
        out_shape=[jax.ShapeDtypeStruct((m, d), F32), *cast_out_shapes],
        scratch_shapes=[pltpu.VMEM((tm, n_in), F32),
                        pltpu.VMEM((CONV_HALO + tm, B_WIDTH), F32)],
        compiler_params=_compiler_params(("arbitrary", "arbitrary")),
        name="even_mixer",
    )(h, g_pre, w_in, w_spatial, bias_full, conv_w, w_out, g_post, *cast)
    return outs[0], outs[1:]


def _attn_band_bias():
    blk = ATTN_BLOCK
    tables = []
    for _, d in DILATED_BRANCHES:
        pieces = ATTN_PERM // d
        rows = blk // pieces
        i = np.arange(blk)
        run = i // rows
        if pieces == ATTN_PERM:
            run = ATTN_PERM_STEP * (run % ATTN_PERM_STEP) + run // ATTN_PERM_STEP
        uq = pieces * (i % rows) + run
        uk = np.concatenate([uq, blk + uq])
        step = uq[:, None] + blk - uk[None, :]
        valid = (step >= 0) & (step <= blk)
        first = valid & (uk[None, :] >= blk)
        tables.append(np.stack([np.where(valid, 0.0, -np.inf), np.where(first, 0.0, -np.inf)]))
    return np.stack(tables).astype(np.float32)


def _attn_block(qb, k_prev, k_cur, v_prev, v_cur, bias, old):
    kw = jnp.concatenate([k_prev, k_cur], axis=0)
    vw = jnp.concatenate([v_prev, v_cur], axis=0)
    s = lax.dot_general(qb, kw, (((1,), (1,)), ((), ())), preferred_element_type=F32) + bias
    m_blk = jnp.max(s, axis=-1, keepdims=True)
    m_new = jnp.broadcast_to(m_blk, (s.shape[0], HEAD_DIM))
    if old is not None:
        m_new = jnp.maximum(old[0], m_new)
    p = jnp.exp2(s - jnp.concatenate([m_new, m_new], axis=1)).astype(BF16)
    v_ones = jnp.concatenate([vw, jnp.ones_like(vw)], axis=1)
    pv = jnp.dot(p, v_ones, preferred_element_type=F32)
    a_new, l_new = pv[:, :HEAD_DIM], pv[:, HEAD_DIM:]
    if old is not None:
        corr = jnp.exp2(old[0] - m_new)
        l_new = old[1] * corr + l_new
        a_new = old[2] * corr + a_new
    return m_new, l_new, a_new


def _attn_kernel(q_ref, k_ref, v_ref, bias_ref, o_ref, qp_ref, kp_ref, vp_ref,
                 qh_ref, kh_ref, vh_ref, acc_ref, m_ref, l_ref, tmp_ref, *, seq, dilations):
    blk = ATTN_BLOCK
    lp = seq // ATTN_PERM
    step = ATTN_PERM_STEP
    lq = seq // step
    q_scale = HEAD_DIM ** -0.5 * math.log2(math.e)
    for src_ref, dst_ref, half_ref, scale in ((q_ref, qp_ref, qh_ref, q_scale),
                                              (k_ref, kp_ref, kh_ref, None),
                                              (v_ref, vp_ref, vh_ref, None)):
        for lo in range(step):
            tmp_ref[lo * lq:(lo + 1) * lq, :] = src_ref[pl.ds(lo, lq, stride=step), :]
        for lo in range(step):
            for hi in range(step):
                run = lo * step + hi
                rows = tmp_ref[pl.ds(lo * lq + hi, lp, stride=step), :]
                if scale is not None:
                    rows = rows * scale
                dst_ref[run * lp:(run + 1) * lp, :] = rows
                half_ref[run * lp:(run + 1) * lp, :] = rows.astype(BF16)

    for bi, d in enumerate(dilations):
        pieces = ATTN_PERM // d
        rows = blk // pieces
        nb = lp // rows
        gn = min(nb, ATTN_GROUP)
        gs = min(ATTN_GROUP // gn, d)
        n_groups = nb // gn
        assert nb % gn == 0 and d % gs == 0
        packed = rows % BF16_SUBLANES == 0
        q_src, k_src, v_src = (qh_ref, kh_ref, vh_ref) if packed else (qp_ref, kp_ref, vp_ref)

        def starts(r_sub, n, d=d, pieces=pieces, rows=rows):
            return [pl.multiple_of((r_sub * pieces + a) * lp + n * rows, rows) for a in range(pieces)]

        def gather(ref, st, rows=rows):
            parts = [ref[pl.ds(s0, rows), :] for s0 in st]
            return parts[0] if len(parts) == 1 else jnp.concatenate(parts, axis=0)

        def gather_half(ref, st):
            return gather(ref, st).astype(BF16)

        def scatter(ref, st, val, rows=rows):
            for a, s0 in enumerate(st):
                ref[pl.ds(s0, rows), :] = val[a * rows:(a + 1) * rows]

        def body(it, carry, bi=bi, gn=gn, gs=gs, n_groups=n_groups,
                 q_src=q_src, k_src=k_src, v_src=v_src):
            sg = it // n_groups
            n0 = (it % n_groups) * gn
            work = []
            for si in range(gs):
                r_sub = sg * gs + si
                st_prev = starts(r_sub, jnp.maximum(n0 - 1, 0))
                kb = [gather_half(k_src, st_prev)]
                vb = [gather_half(v_src, st_prev)]
                for j in range(gn):
                    st = starts(r_sub, n0 + j)
                    kb.append(gather_half(k_src, st))
                    vb.append(gather_half(v_src, st))
                    if j > 0:
                        bias = bias_ref[bi, 0]
                    elif n_groups == 1:
                        bias = bias_ref[bi, 1]
                    else:
                        bias = bias_ref[bi, jnp.where(n0 == 0, 1, 0)]
                    old = None if bi == 0 else (gather(m_ref, st), gather(l_ref, st),
                                                gather(acc_ref, st))
                    work.append((st, gather_half(q_src, st), kb[j], kb[j + 1],
                                 vb[j], vb[j + 1], bias, old))
            results = [(w[0],) + _attn_block(*w[1:]) for w in work]
            for st, m_new, l_new, a_new in results:
                scatter(m_ref, st, m_new)
                scatter(l_ref, st, l_new)
                scatter(acc_ref, st, a_new)
            return carry

        lax.fori_loop(0, (d // gs) * n_groups, body, 0)

    for lo in range(step):
        for hi in range(step):
            src = slice((lo * step + hi) * lp, (lo * step + hi + 1) * lp)
            tmp_ref[pl.ds(lo * lq + hi, lp, stride=step), :] = acc_ref[src, :] / l_ref[src, :]
    for lo in range(step):
        o_ref[pl.ds(lo, lq, stride=step), :] = tmp_ref[lo * lq:(lo + 1) * lq, :]


def _attention(qkv, batch, seq):
    dilations = tuple(d for _, d in DILATED_BRANCHES)
    for window, d in DILATED_BRANCHES:
        assert window // d == ATTN_BLOCK and seq % (d * ATTN_BLOCK) == 0
        assert ATTN_PERM % d == 0 and ATTN_BLOCK % (ATTN_PERM // d) == 0
    qkv3 = qkv.reshape(batch, seq, 3 * D_MODEL)
    bias = jnp.asarray(_attn_band_bias())
    spec = lambda part: pl.BlockSpec((None, seq, HEAD_DIM),
                                     lambda b, h, part=part: (b, 0, part * N_HEADS + h))
    kern = functools.partial(_attn_kernel, seq=seq, dilations=dilations)
    out = pl.pallas_call(
        kern,
        grid=(batch, N_HEADS),
        in_specs=[spec(0), spec(1), spec(2),
                  pl.BlockSpec(bias.shape, lambda b, h: (0, 0, 0, 0))],
        out_specs=pl.BlockSpec((None, seq, HEAD_DIM), lambda b, h: (b, 0, h)),
        out_shape=jax.ShapeDtypeStruct((batch, seq, D_MODEL), F32),
        scratch_shapes=[pltpu.VMEM((seq, HEAD_DIM), F32)] * 3 + [pltpu.VMEM((seq, HEAD_DIM), BF16)] * 3
                       + [pltpu.VMEM((seq, HEAD_DIM), F32)] * 4,
        compiler_params=_compiler_params(("parallel", "parallel")),
        name="attention",
    )(qkv3, qkv3, qkv3, bias)
    return out.reshape(batch * seq, D_MODEL)


def _matmul_rms_res_kernel(a_ref, w_ref, g_ref, h_ref, o_ref):
    f = jnp.dot(a_ref[...].astype(BF16), w_ref[...], preferred_element_type=F32)
    o_ref[...] = h_ref[...] + _rmsnorm(f, g_ref[...])


def _matmul_rms_res(a, w, li, g, h):
    m, k = a.shape
    n = w.shape[2]
    tm = RES_TM
    assert m % tm == 0
    return pl.pallas_call(
        _matmul_rms_res_kernel,
        grid=(m // tm,),
        in_specs=[pl.BlockSpec((tm, k), lambda i: (i, 0)),
                  pl.BlockSpec((None, k, n), lambda i: (li, 0, 0)),
                  pl.BlockSpec((1, n), lambda i: (0, 0)),
                  pl.BlockSpec((tm, n), lambda i: (i, 0))],
        out_specs=pl.BlockSpec((tm, n), lambda i: (i, 0)),
        out_shape=jax.ShapeDtypeStruct((m, n), F32),
        compiler_params=_compiler_params(("parallel",)),
        name="matmul_rms_res",
    )(a, w, g, h)


def _mlp_kernel(x_ref, g_pre_ref, w_up_ref, w_down_ref, g_post_ref, *rest, n_cast):
    cast_in = rest[:n_cast]
    o_ref = rest[n_cast]
    cast_out = rest[n_cast + 1:2 * n_cast + 1]
    hn_ref = rest[2 * n_cast + 1]
    k = pl.program_id(1)

    @pl.when(k == 0)
    def _():
        hn_ref[...] = _rmsnorm(x_ref[...], g_pre_ref[...]).astype(BF16)
        o_ref[...] = jnp.zeros_like(o_ref)

    sub = MLP_SUB
    _cast_chunks(cast_in, cast_out)
    for c in range(w_up_ref.shape[1] // sub):
        ffn = pl.ds(c * sub, sub)
        u = jnp.dot(hn_ref[...], w_up_ref[:, ffn], preferred_element_type=F32)
        u = jnp.square(jnp.maximum(u, 0.0)).astype(BF16)
        for n in range(o_ref.shape[1] // sub):
            cols = pl.ds(n * sub, sub)
            o_ref[:, cols] += jnp.dot(u, w_down_ref[ffn, cols], preferred_element_type=F32)

    @pl.when(k == pl.num_programs(1) - 1)
    def _():
        def finish_rows(rows):
            o_ref[rows, :] = x_ref[rows, :] + _rmsnorm(o_ref[rows, :], g_post_ref[...])
        _for_row_chunks(o_ref.shape[0], finish_rows)


def _mlp(h, g_pre, w_up, w_down, g_post, cast=(), cast_li=0):
    m, d = h.shape
    f = w_up.shape[1]
    tm, tf = MLP_TM, MLP_TF
    assert m % tm == 0 and f % tf == 0
    cast_in_specs, cast_out_specs, cast_out_shapes = _weight_cast_plan(
        cast, cast_li, m // tm, f // tf, MLP_CAST_STEPS)
    outs = pl.pallas_call(
        functools.partial(_mlp_kernel, n_cast=len(cast)),
        grid=(m // tm, f // tf),
        in_specs=[pl.BlockSpec((tm, d), lambda i, k: (i, 0)),
                  pl.BlockSpec((1, d), lambda i, k: (0, 0)),
                  pl.BlockSpec((d, tf), lambda i, k: (0, k)),
                  pl.BlockSpec((tf, d), lambda i, k: (k, 0)),
                  pl.BlockSpec((1, d), lambda i, k: (0, 0)),
                  *cast_in_specs],
        out_specs=[pl.BlockSpec((tm, d), lambda i, k: (i, 0)), *cast_out_specs],
        out_shape=[jax.ShapeDtypeStruct((m, d), F32), *cast_out_shapes],
        scratch_shapes=[pltpu.VMEM((tm, d), BF16)],
        compiler_params=_compiler_params(("arbitrary", "arbitrary")),
        name="mlp",
    )(h, g_pre, w_up, w_down, g_post, *cast)
    return outs[0], outs[1:]


def _rope_tables(seq):
    half = HEAD_DIM // 2
    inv_freq = ROPE_THETA ** (-jnp.arange(half, dtype=F32) * 2.0 / HEAD_DIM)
    ang = jnp.arange(seq, dtype=jnp.int32).astype(F32)[:, None] * inv_freq[None, :]
    cos = jnp.cos(ang)
    sin = jnp.sin(ang)
    return jnp.concatenate([cos, cos], axis=-1), jnp.concatenate([-sin, sin], axis=-1)


def kernel(x, norm_mix_pre, norm_mix_post, norm_mlp_pre, norm_mlp_post, w_in_ab, w_spatial,
           b_spatial, conv_w, w_out_ab, w_qkv, w_o, w_up, w_down):
    batch, seq, d = x.shape
    assert d == D_MODEL
    depth = norm_mix_pre.shape[0]
    m = batch * seq
    h = x.reshape(m, d)
    cos_tab, sin_tab = _rope_tables(seq)

    w_in_bf, w_out_bf, w_qkv_bf, w_o_bf = (w.astype(BF16) for w in (w_in_ab, w_out_ab, w_qkv, w_o))

    for layer in range(depth):
        g_pre = norm_mix_pre[layer][None, :]
        g_post = norm_mix_post[layer][None, :]
        if layer % 2 == 0:
            e = layer // 2
            first_cast = (w_up, w_down) if layer == 0 else ()
            bias_full = jnp.repeat(b_spatial[e].T, A_GROUP_DIM, axis=1)
            h, first_bf = _even_mixer(h, g_pre, w_in_bf, w_spatial[e], bias_full, conv_w[e],
                                      w_out_bf, e, g_post, seq, cast=first_cast, cast_li=layer)
            if layer == 0:
                w_up_bf, w_down_bf = first_bf
        else:
            o = layer // 2
            qkv = _qkv_proj(h, g_pre, w_qkv_bf, o, cos_tab, sin_tab, seq)
            att = _attention(qkv, batch, seq)
            h = _matmul_rms_res(att, w_o_bf, o, g_post, h)
        next_cast = (w_up, w_down) if layer + 1 < depth else ()
        h, next_bf = _mlp(h, norm_mlp_pre[layer][None, :], w_up_bf, w_down_bf,
                          norm_mlp_post[layer][None, :], cast=next_cast, cast_li=layer + 1)
        if next_bf:
            w_up_bf, w_down_bf = next_bf
    return h.reshape(batch, seq, d)
```

```python
import functools
import math

import jax
import jax.numpy as jnp
import numpy as np
from jax import lax
from jax.experimental import pallas as pl
from jax.experimental.pallas import tpu as pltpu

F32 = jnp.float32
BF16 = jnp.bfloat16

D_MODEL = 2048
A_WIDTH = D_MODEL // 2
B_WIDTH = D_MODEL - A_WIDTH
A_GROUPS = 8
A_GROUP_DIM = A_WIDTH // A_GROUPS
CHUNK = 128
CONV_WIDTH = 3
HEAD_DIM = 128
N_HEADS = D_MODEL // HEAD_DIM
DILATED_BRANCHES = ((128, 1), (512, 4), (2048, 16))
ATTN_BLOCK = 128
ATTN_PERM = 16
ATTN_PERM_STEP = 4
ATTN_GROUP = 32
ROPE_THETA = 10000.0
FFN_DIM = 4 * D_MODEL
RMS_EPS = 1e-6
LN_EPS = 1e-5

V7X_LANES = 128
BF16_SUBLANES = 16
V7X_VMEM_LIMIT_BYTES = 62 * 1024 * 1024

QKV_PROJ_TM = 256
RMS_MATMUL_SUB_TN = 512
EVEN_MIXER_TM = 256
RES_TM = 512
MLP_TM = 1024
MLP_TF = 1024
MLP_SUB = 512
CONV_HALO = 8
MLP_LAST_SPLIT = 2
NORM_ROW_CHUNK = 16
RMS_MATMUL_CAST_STEPS = 64
MLP_CAST_STEPS = 128


def _compiler_params(semantics):
    return pltpu.CompilerParams(dimension_semantics=semantics,
                                vmem_limit_bytes=V7X_VMEM_LIMIT_BYTES)


def _rmsnorm(x, g):
    ms = jnp.mean(x * x, axis=-1, keepdims=True)
    return (x * lax.rsqrt(ms + RMS_EPS)) * g


def _weight_cast_plan(cast, cast_li, n_outer, n_inner, max_steps):
    steps = min(max_steps, 2 ** int(math.log2(n_outer * n_inner)))
    chunk = lambda i, j: jnp.minimum(i * n_inner + j, steps - 1)
    in_specs, out_specs, out_shapes = [], [], []
    for c in cast:
        _, rows, width = c.shape
        assert rows % (steps * BF16_SUBLANES) == 0
        rows_per_step = rows // steps
        in_specs.append(pl.BlockSpec((None, rows_per_step, width),
                                     lambda i, j: (cast_li, chunk(i, j), 0)))
        out_specs.append(pl.BlockSpec((rows_per_step, width), lambda i, j: (chunk(i, j), 0)))
        out_shapes.append(jax.ShapeDtypeStruct((rows, width), BF16))
    return in_specs, out_specs, out_shapes


def _cast_chunks(cast_in, cast_out):
    for src, dst in zip(cast_in, cast_out):
        dst[...] = src[...].astype(BF16)


def _qkv_proj_kernel(x_ref, g_ref, w_ref, cos_ref, sin_ref, o_ref):
    hn = _rmsnorm(x_ref[...], g_ref[...]).astype(BF16)
    cos = cos_ref[...]
    sin = sin_ref[...]
    sub = RMS_MATMUL_SUB_TN
    for c in range(w_ref.shape[1] // sub):
        p = jnp.dot(hn, w_ref[:, pl.ds(c * sub, sub)], preferred_element_type=F32)
        if (c + 1) * sub <= 2 * D_MODEL:
            for hh in range(sub // HEAD_DIM):
                t = p[:, hh * HEAD_DIM:(hh + 1) * HEAD_DIM]
                o_ref[:, pl.ds(c * sub + hh * HEAD_DIM, HEAD_DIM)] = (
                    t * cos + pltpu.roll(t, HEAD_DIM // 2, 1) * sin)
        else:
            o_ref[:, pl.ds(c * sub, sub)] = p


def _qkv_proj(x, g, w, li, cos_tab, sin_tab, seq):
    m, k = x.shape
    n = w.shape[2]
    tm = QKV_PROJ_TM
    assert m % tm == 0 and seq % tm == 0 and (2 * D_MODEL) % RMS_MATMUL_SUB_TN == 0
    tiles_per_seq = seq // tm
    table = pl.BlockSpec((tm, HEAD_DIM), lambda i: (i % tiles_per_seq, 0))
    return pl.pallas_call(
        _qkv_proj_kernel,
        grid=(m // tm,),
        in_specs=[pl.BlockSpec((tm, k), lambda i: (i, 0)),
                  pl.BlockSpec((1, k), lambda i: (0, 0)),
                  pl.BlockSpec((None, k, n), lambda i: (li, 0, 0), pipeline_mode=pl.Buffered(1)),
                  table, table],
        out_specs=pl.BlockSpec((tm, n), lambda i: (i, 0)),
        out_shape=jax.ShapeDtypeStruct((m, n), F32),
        compiler_params=_compiler_params(("parallel",)),
        name="qkv_proj",
    )(x, g, w, cos_tab, sin_tab)


def _even_mixer_kernel(x_ref, g_pre_ref, w_in_ref, ws_ref, bias_ref, cw_ref, w_out_ref,
                       g_post_ref, *rest, tiles_per_seq, n_cast):
    cast_in = rest[:n_cast]
    o_ref = rest[n_cast]
    cast_out = rest[n_cast + 1:2 * n_cast + 1]
    proj_ref, z_ref = rest[2 * n_cast + 1:]
    tm = x_ref.shape[0]
    i = pl.program_id(0)

    @pl.when(i == 0)
    def _():
        z_ref[...] = jnp.zeros_like(z_ref)

    _cast_chunks(cast_in, cast_out)

    hn = _rmsnorm(x_ref[...], g_pre_ref[...]).astype(BF16)
    sub = RMS_MATMUL_SUB_TN
    for c in range(w_in_ref.shape[1] // sub):
        cols = pl.ds(c * sub, sub)
        p = jnp.dot(hn, w_in_ref[:, cols], preferred_element_type=F32)
        proj_ref[:, cols] = jax.nn.gelu(p) if (c + 1) * sub <= 2 * A_WIDTH else p
    au_ref, av_ref, gb_ref, gc_ref, bx_ref = (
        proj_ref.at[:, pl.ds(s * A_WIDTH, A_WIDTH)] for s in range(5))

    av = av_ref[...]
    mu = jnp.mean(av, axis=-1, keepdims=True)
    cen = av - mu
    var = jnp.mean(cen * cen, axis=-1, keepdims=True)
    vn = (cen * lax.rsqrt(var + LN_EPS)).astype(BF16)
    row = lax.broadcasted_iota(jnp.int32, (CHUNK, CHUNK), 0)
    col = lax.broadcasted_iota(jnp.int32, (CHUNK, CHUNK), 1)
    causal = col <= row
    a_cols = []
    for g in range(A_GROUPS):
        cs = slice(g * A_GROUP_DIM, (g + 1) * A_GROUP_DIM)
        w_causal = jnp.where(causal, ws_ref[g], 0.0).astype(BF16)
        chunks = []
        for c in range(tm // CHUNK):
            rs = slice(c * CHUNK, (c + 1) * CHUNK)
            mixed = jnp.dot(w_causal, vn[rs, cs], preferred_element_type=F32) + bias_ref[:, cs]
            chunks.append((au_ref[rs, cs] * mixed).astype(BF16))
        a_cols.append(jnp.concatenate(chunks, axis=0))

    z = gc_ref[...] * bx_ref[...]
    at_seq_start = (i % tiles_per_seq) == 0
    z_before = jnp.where(at_seq_start, 0.0, z_ref[tm:tm + CONV_HALO, :])
    z_ref[0:CONV_HALO, :] = z_before
    z_ref[CONV_HALO:CONV_HALO + tm, :] = z
    y = (cw_ref[2:3, :] * z
         + cw_ref[1:2, :] * z_ref[CONV_HALO - 1:CONV_HALO - 1 + tm, :]
         + cw_ref[0:1, :] * z_ref[CONV_HALO - 2:CONV_HALO - 2 + tm, :])
    b_out = (gb_ref[...] * y).astype(BF16)

    mixed_ab = jnp.concatenate(a_cols + [b_out], axis=1)
    f = jnp.dot(mixed_ab, w_out_ref[...], preferred_element_type=F32)
    o_ref[...] = x_ref[...] + _rmsnorm(f, g_post_ref[...])


def _even_mixer(h, g_pre, w_in, w_spatial, bias_full, conv_w, w_out, li, g_post, seq,
                cast=(), cast_li=0):
    m, d = h.shape
    n_in = w_in.shape[2]
    tm = EVEN_MIXER_TM
    assert seq % tm == 0 and tm % CHUNK == 0 and A_WIDTH == B_WIDTH
    assert n_in == 2 * A_WIDTH + 3 * B_WIDTH and (2 * A_WIDTH) % RMS_MATMUL_SUB_TN == 0
    cast_in_specs, cast_out_specs, cast_out_shapes = _weight_cast_plan(
        cast, cast_li, m // tm, 1, RMS_MATMUL_CAST_STEPS)
    const = lambda shape: pl.BlockSpec(shape, lambda i, j: (0,) * len(shape))
    resident = lambda rows, cols: pl.BlockSpec((None, rows, cols), lambda i, j: (li, 0, 0),
                                               pipeline_mode=pl.Buffered(1))
    kern = functools.partial(_even_mixer_kernel, tiles_per_seq=seq // tm, n_cast=len(cast))
    outs = pl.pallas_call(
        kern,
        grid=(m // tm, 1),
        in_specs=[pl.BlockSpec((tm, d), lambda i, j: (i, 0)),
                  const((1, d)),
                  resident(d, n_in),
                  const((A_GROUPS, CHUNK, CHUNK)),
                  const((CHUNK, A_WIDTH)),
                  const((CONV_WIDTH, B_WIDTH)),
                  resident(d, d),
                  const((1, d)),
                  *cast_in_specs],
        out_specs=[pl.BlockSpec((tm, d), lambda i, j: (i, 0)), *cast_out_specs],
        out_shape=[jax.ShapeDtypeStruct((m, d), F32), *cast_out_shapes],
        scratch_shapes=[pltpu.VMEM((tm, n_in), F32),
                        pltpu.VMEM((CONV_HALO + tm, B_WIDTH), F32)],
        compiler_params=_compiler_params(("arbitrary", "arbitrary")),
        name="even_mixer",
    )(h, g_pre, w_in, w_spatial, bias_full, conv_w, w_out, g_post, *cast)
    return outs[0], outs[1:]


def _attn_band_bias():
    blk = ATTN_BLOCK
    tables = []
    for _, d in DILATED_BRANCHES:
        pieces = ATTN_PERM // d
        rows = blk // pieces
        i = np.arange(blk)
        run = i // rows
        if pieces == ATTN_PERM:
            run = ATTN_PERM_STEP * (run % ATTN_PERM_STEP) + run // ATTN_PERM_STEP
        uq = pieces * (i % rows) + run
        uk = np.concatenate([uq, blk + uq])
        step = uq[:, None] + blk - uk[None, :]
        valid = (step >= 0) & (step <= blk)
        first = valid & (uk[None, :] >= blk)
        tables.append(np.stack([np.where(valid, 0.0, -np.inf), np.where(first, 0.0, -np.inf)]))
    return np.stack(tables).astype(np.float32)


def _attn_block(qb, k_prev, k_cur, v_prev, v_cur, bias, old):
    kw = jnp.concatenate([k_prev, k_cur], axis=0)
    vw = jnp.concatenate([v_prev, v_cur], axis=0)
    s = lax.dot_general(qb, kw, (((1,), (1,)), ((), ())), preferred_element_type=F32) + bias
    m_blk = jnp.max(s, axis=-1, keepdims=True)
    m_new = jnp.broadcast_to(m_blk, (s.shape[0], HEAD_DIM))
    if old is not None:
        m_new = jnp.maximum(old[0], m_new)
    p = jnp.exp2(s - jnp.concatenate([m_new, m_new], axis=1)).astype(BF16)
    v_ones = jnp.concatenate([vw, jnp.ones_like(vw)], axis=1)
    pv = jnp.dot(p, v_ones, preferred_element_type=F32)
    a_new, l_new = pv[:, :HEAD_DIM], pv[:, HEAD_DIM:]
    if old is not None:
        corr = jnp.exp2(old[0] - m_new)
        l_new = old[1] * corr + l_new
        a_new = old[2] * corr + a_new
    return m_new, l_new, a_new


def _attn_kernel(q_ref, k_ref, v_ref, bias_ref, o_ref, qp_ref, kp_ref, vp_ref,
                 qh_ref, kh_ref, vh_ref, acc_ref, m_ref, l_ref, tmp_ref, *, seq, dilations):
    blk = ATTN_BLOCK
    lp = seq // ATTN_PERM
    step = ATTN_PERM_STEP
    lq = seq // step
    q_scale = HEAD_DIM ** -0.5 * math.log2(math.e)
    for src_ref, dst_ref, half_ref, scale in ((q_ref, qp_ref, qh_ref, q_scale),
                                              (k_ref, kp_ref, kh_ref, None),
                                              (v_ref, vp_ref, vh_ref, None)):
        for lo in range(step):
            tmp_ref[lo * lq:(lo + 1) * lq, :] = src_ref[pl.ds(lo, lq, stride=step), :]
        for lo in range(step):
            for hi in range(step):
                run = lo * step + hi
                rows = tmp_ref[pl.ds(lo * lq + hi, lp, stride=step), :]
                if scale is not None:
                    rows = rows * scale
                dst_ref[run * lp:(run + 1) * lp, :] = rows
                half_ref[run * lp:(run + 1) * lp, :] = rows.astype(BF16)

    for bi, d in enumerate(dilations):
        pieces = ATTN_PERM // d
        rows = blk // pieces
        nb = lp // rows
        gn = min(nb, ATTN_GROUP)
        gs = min(ATTN_GROUP // gn, d)
        n_groups = nb // gn
        assert nb % gn == 0 and d % gs == 0
        packed = rows % BF16_SUBLANES == 0
        q_src, k_src, v_src = (qh_ref, kh_ref, vh_ref) if packed else (qp_ref, kp_ref, vp_ref)

        def starts(r_sub, n, d=d, pieces=pieces, rows=rows):
            return [pl.multiple_of((r_sub * pieces + a) * lp + n * rows, rows) for a in range(pieces)]

        def gather(ref, st, rows=rows):
            parts = [ref[pl.ds(s0, rows), :] for s0 in st]
            return parts[0] if len(parts) == 1 else jnp.concatenate(parts, axis=0)

        def gather_half(ref, st):
            return gather(ref, st).astype(BF16)

        def scatter(ref, st, val, rows=rows):
            for a, s0 in enumerate(st):
                ref[pl.ds(s0, rows), :] = val[a * rows:(a + 1) * rows]

        def body(it, carry, bi=bi, gn=gn, gs=gs, n_groups=n_groups,
                 q_src=q_src, k_src=k_src, v_src=v_src):
            sg = it // n_groups
            n0 = (it % n_groups) * gn
            work = []
            for si in range(gs):
                r_sub = sg * gs + si
                st_prev = starts(r_sub, jnp.maximum(n0 - 1, 0))
                kb = [gather_half(k_src, st_prev)]
                vb = [gather_half(v_src, st_prev)]
                for j in range(gn):
                    st = starts(r_sub, n0 + j)
                    kb.append(gather_half(k_src, st))
                    vb.append(gather_half(v_src, st))
                    if j > 0:
                        bias = bias_ref[bi, 0]
                    elif n_groups == 1:
                        bias = bias_ref[bi, 1]
                    else:
                        bias = bias_ref[bi, jnp.where(n0 == 0, 1, 0)]
                    old = None if bi == 0 else (gather(m_ref, st), gather(l_ref, st),
                                                gather(acc_ref, st))
                    work.append((st, gather_half(q_src, st), kb[j], kb[j + 1],
                                 vb[j], vb[j + 1], bias, old))
            results = [(w[0],) + _attn_block(*w[1:]) for w in work]
            for st, m_new, l_new, a_new in results:
                scatter(m_ref, st, m_new)
                scatter(l_ref, st, l_new)
                scatter(acc_ref, st, a_new)
            return carry

        lax.fori_loop(0, (d // gs) * n_groups, body, 0)

    for lo in range(step):
        for hi in range(step):
            src = slice((lo * step + hi) * lp, (lo * step + hi + 1) * lp)
            tmp_ref[pl.ds(lo * lq + hi, lp, stride=step), :] = acc_ref[src, :] / l_ref[src, :]
    for lo in range(step):
        o_ref[pl.ds(lo, lq, stride=step), :] = tmp_ref[lo * lq:(lo + 1) * lq, :]


def _attention(qkv, batch, seq):
    dilations = tuple(d for _, d in DILATED_BRANCHES)
    for window, d in DILATED_BRANCHES:
        assert window // d == ATTN_BLOCK and seq % (d * ATTN_BLOCK) == 0
        assert ATTN_PERM % d == 0 and ATTN_BLOCK % (ATTN_PERM // d) == 0
    qkv3 = qkv.reshape(batch, seq, 3 * D_MODEL)
    bias = jnp.asarray(_attn_band_bias())
    spec = lambda part: pl.BlockSpec((None, seq, HEAD_DIM),
                                     lambda b, h, part=part: (b, 0, part * N_HEADS + h))
    kern = functools.partial(_attn_kernel, seq=seq, dilations=dilations)
    out = pl.pallas_call(
        kern,
        grid=(batch, N_HEADS),
        in_specs=[spec(0), spec(1), spec(2),
                  pl.BlockSpec(bias.shape, lambda b, h: (0, 0, 0, 0))],
        out_specs=pl.BlockSpec((None, seq, HEAD_DIM), lambda b, h: (b, 0, h)),
        out_shape=jax.ShapeDtypeStruct((batch, seq, D_MODEL), F32),
        scratch_shapes=[pltpu.VMEM((seq, HEAD_DIM), F32)] * 3 + [pltpu.VMEM((seq, HEAD_DIM), BF16)] * 3
                       + [pltpu.VMEM((seq, HEAD_DIM), F32)] * 4,
        compiler_params=_compiler_params(("parallel", "parallel")),
        name="attention",
    )(qkv3, qkv3, qkv3, bias)
    return out.reshape(batch * seq, D_MODEL)


def _matmul_rms_res_kernel(a_ref, w_ref, g_ref, h_ref, o_ref):
    f = jnp.dot(a_ref[...].astype(BF16), w_ref[...], preferred_element_type=F32)
    o_ref[...] = h_ref[...] + _rmsnorm(f, g_ref[...])


def _matmul_rms_res(a, w, li, g, h):
    m, k = a.shape
    n = w.shape[2]
    tm = RES_TM
    assert m % tm == 0
    return pl.pallas_call(
        _matmul_rms_res_kernel,
        grid=(m // tm,),
        in_specs=[pl.BlockSpec((tm, k), lambda i: (i, 0)),
                  pl.BlockSpec((None, k, n), lambda i: (li, 0, 0)),
                  pl.BlockSpec((1, n), lambda i: (0, 0)),
                  pl.BlockSpec((tm, n), lambda i: (i, 0))],
        out_specs=pl.BlockSpec((tm, n), lambda i: (i, 0)),
        out_shape=jax.ShapeDtypeStruct((m, n), F32),
        compiler_params=_compiler_params(("parallel",)),
        name="matmul_rms_res",
    )(a, w, g, h)


def _mlp_kernel(x_ref, g_pre_ref, w_up_ref, w_down_ref, g_post_ref, *rest, n_cast):
    cast_in = rest[:n_cast]
    o_ref = rest[n_cast]
    cast_out = rest[n_cast + 1:2 * n_cast + 1]
    hn_ref = rest[2 * n_cast + 1]
    k = pl.program_id(1)

    def ffn_step(first, last):
        sub = MLP_SUB
        tm = o_ref.shape[0]
        _cast_chunks(cast_in, cast_out)
        group = tm // MLP_LAST_SPLIT if last else tm
        for r0 in range(0, tm, group):
            rows = pl.ds(r0, group)
            if first:
                hn = _rmsnorm(x_ref[rows, :], g_pre_ref[...]).astype(BF16)
                hn_ref[rows, :] = hn
            else:
                hn = hn_ref[rows, :]
            for c in range(w_up_ref.shape[1] // sub):
                ffn = pl.ds(c * sub, sub)
                u = jnp.dot(hn, w_up_ref[:, ffn], preferred_element_type=F32)
                u = jnp.square(jnp.maximum(u, 0.0)).astype(BF16)
                for n in range(o_ref.shape[1] // sub):
                    cols = pl.ds(n * sub, sub)
                    part = jnp.dot(u, w_down_ref[ffn, cols], preferred_element_type=F32)
                    if first and c == 0:
                        o_ref[rows, cols] = part
                    else:
                        o_ref[rows, cols] += part
            if last:
                for r1 in range(r0, r0 + group, NORM_ROW_CHUNK):
                    chunk = pl.ds(r1, NORM_ROW_CHUNK)
                    o_ref[chunk, :] = x_ref[chunk, :] + _rmsnorm(o_ref[chunk, :], g_post_ref[...])

    n_steps = pl.num_programs(1)

    @pl.when(k == 0)
    def _():
        ffn_step(first=True, last=False)

    @pl.when((k > 0) & (k < n_steps - 1))
    def _():
        ffn_step(first=False, last=False)

    @pl.when(k == n_steps - 1)
    def _():
        ffn_step(first=False, last=True)


def _mlp(h, g_pre, w_up, w_down, g_post, cast=(), cast_li=0):
    m, d = h.shape
    f = w_up.shape[1]
    tm, tf = MLP_TM, MLP_TF
    assert m % tm == 0 and f % tf == 0
    cast_in_specs, cast_out_specs, cast_out_shapes = _weight_cast_plan(
        cast, cast_li, m // tm, f // tf, MLP_CAST_STEPS)
    outs = pl.pallas_call(
        functools.partial(_mlp_kernel, n_cast=len(cast)),
        grid=(m // tm, f // tf),
        in_specs=[pl.BlockSpec((tm, d), lambda i, k: (i, 0)),
                  pl.BlockSpec((1, d), lambda i, k: (0, 0)),
                  pl.BlockSpec((d, tf), lambda i, k: (0, k)),
                  pl.BlockSpec((tf, d), lambda i, k: (k, 0)),
                  pl.BlockSpec((1, d), lambda i, k: (0, 0)),
                  *cast_in_specs],
        out_specs=[pl.BlockSpec((tm, d), lambda i, k: (i, 0)), *cast_out_specs],
        out_shape=[jax.ShapeDtypeStruct((m, d), F32), *cast_out_shapes],
        scratch_shapes=[pltpu.VMEM((tm, d), BF16)],
        compiler_params=_compiler_params(("arbitrary", "arbitrary")),
        name="mlp",
    )(h, g_pre, w_up, w_down, g_post, *cast)
    return outs[0], outs[1:]


def _rope_tables(seq):
    half = HEAD_DIM // 2
    inv_freq = ROPE_THETA ** (-jnp.arange(half, dtype=F32) * 2.0 / HEAD_DIM)
    ang = jnp.arange(seq, dtype=jnp.int32).astype(F32)[:, None] * inv_freq[None, :]
    cos = jnp.cos(ang)
    sin = jnp.sin(ang)
    return jnp.concatenate([cos, cos], axis=-1), jnp.concatenate([-sin, sin], axis=-1)


def kernel(x, norm_mix_pre, norm_mix_post, norm_mlp_pre, norm_mlp_post, w_in_ab, w_spatial,
           b_spatial, conv_w, w_out_ab, w_qkv, w_o, w_up, w_down):
    batch, seq, d = x.shape
    assert d == D_MODEL
    depth = norm_mix_pre.shape[0]
    m = batch * seq
    h = x.reshape(m, d)
    cos_tab, sin_tab = _rope_tables(seq)

    w_in_bf, w_out_bf, w_qkv_bf, w_o_bf = (w.astype(BF16) for w in (w_in_ab, w_out_ab, w_qkv, w_o))

    for layer in range(depth):
        g_pre = norm_mix_pre[layer][None, :]
        g_post = norm_mix_post[layer][None, :]
        if layer % 2 == 0:
            e = layer // 2
            first_cast = (w_up, w_down) if layer == 0 else ()
            bias_full = jnp.repeat(b_spatial[e].T, A_GROUP_DIM, axis=1)
            h, first_bf = _even_mixer(h, g_pre, w_in_bf, w_spatial[e], bias_full, conv_w[e],
                                      w_out_bf, e, g_post, seq, cast=first_cast, cast_li=layer)
            if layer == 0:
                w_up_bf, w_down_bf = first_bf
        else:
            o = layer // 2
            qkv = _qkv_proj(h, g_pre, w_qkv_bf, o, cos_tab, sin_tab, seq)
            att = _attention(qkv, batch, seq)
            h = _matmul_rms_res(att, w_o_bf, o, g_post, h)
        next_cast = (w_up, w_down) if layer + 1 < depth else ()
        h, next_bf = _mlp(h, norm_mlp_pre[layer][None, :], w_up_bf, w_down_bf,
                          norm_mlp_post[layer][None, :], cast=next_cast, cast_li=layer + 1)
        if next_bf:
            w_up_bf, w_down_bf = next_bf
    return h.reshape(batch, seq, d)
```

```python
import functools
import math

import jax
import jax.numpy as jnp
import numpy as np
from jax import lax
from jax.experimental import pallas as pl
from jax.experimental.pallas import tpu as pltpu

F32 = jnp.float32
BF16 = jnp.bfloat16

D_MODEL = 2048
A_WIDTH = D_MODEL // 2
B_WIDTH = D_MODEL - A_WIDTH
A_GROUPS = 8
A_GROUP_DIM = A_WIDTH // A_GROUPS
CHUNK = 128
CONV_WIDTH = 3
HEAD_DIM = 128
N_HEADS = D_MODEL // HEAD_DIM
DILATED_BRANCHES = ((128, 1), (512, 4), (2048, 16))
ATTN_BLOCK = 128
ATTN_PERM = 16
ATTN_PERM_STEP = 4
ATTN_GROUP = 32
ROPE_THETA = 10000.0
FFN_DIM = 4 * D_MODEL
RMS_EPS = 1e-6
LN_EPS = 1e-5

V7X_LANES = 128
BF16_SUBLANES = 16
V7X_VMEM_LIMIT_BYTES = 62 * 1024 * 1024

QKV_PROJ_TM = 256
RMS_MATMUL_SUB_TN = 512
EVEN_MIXER_TM = 256
RES_TM = 512
MLP_TM = 1024
MLP_TF = 1024
MLP_SUB = 512
CONV_HALO = 8
MLP_LAST_SPLIT = 2
NORM_ROW_CHUNK = 16
WEIGHT_CAST_STEPS = 64
MLP_CAST_STEPS = 128


def _compiler_params(semantics):
    return pltpu.CompilerParams(dimension_semantics=semantics,
                                vmem_limit_bytes=V7X_VMEM_LIMIT_BYTES)


def _rmsnorm(x, g):
    ms = jnp.mean(x * x, axis=-1, keepdims=True)
    return (x * lax.rsqrt(ms + RMS_EPS)) * g


def _weight_cast_plan(cast, cast_li, n_outer, n_inner, max_steps):
    steps = min(max_steps, 2 ** int(math.log2(n_outer * n_inner)))
    chunk = lambda i, j: jnp.minimum(i * n_inner + j, steps - 1)
    in_specs, out_specs, out_shapes = [], [], []
    for c in cast:
        _, rows, width = c.shape
        assert rows % (steps * BF16_SUBLANES) == 0
        rows_per_step = rows // steps
        in_specs.append(pl.BlockSpec((None, rows_per_step, width),
                                     lambda i, j: (cast_li, chunk(i, j), 0)))
        out_specs.append(pl.BlockSpec((rows_per_step, width), lambda i, j: (chunk(i, j), 0)))
        out_shapes.append(jax.ShapeDtypeStruct((rows, width), BF16))
    return in_specs, out_specs, out_shapes


def _cast_chunks(cast_in, cast_out):
    for src, dst in zip(cast_in, cast_out):
        dst[...] = src[...].astype(BF16)


def _qkv_proj_kernel(x_ref, g_ref, w_ref, cos_ref, sin_ref, *rest, n_cast):
    cast_in = rest[:n_cast]
    o_ref = rest[n_cast]
    cast_out = rest[n_cast + 1:]
    _cast_chunks(cast_in, cast_out)
    hn = _rmsnorm(x_ref[...], g_ref[...]).astype(BF16)
    cos = cos_ref[...]
    sin = sin_ref[...]
    sub = RMS_MATMUL_SUB_TN
    for c in range(w_ref.shape[1] // sub):
        p = jnp.dot(hn, w_ref[:, pl.ds(c * sub, sub)], preferred_element_type=F32)
        if (c + 1) * sub <= 2 * D_MODEL:
            for hh in range(sub // HEAD_DIM):
                t = p[:, hh * HEAD_DIM:(hh + 1) * HEAD_DIM]
                o_ref[:, pl.ds(c * sub + hh * HEAD_DIM, HEAD_DIM)] = (
                    t * cos + pltpu.roll(t, HEAD_DIM // 2, 1) * sin)
        else:
            o_ref[:, pl.ds(c * sub, sub)] = p


def _qkv_proj(x, g, w, cos_tab, sin_tab, seq, cast=(), cast_li=0):
    m, k = x.shape
    n = w.shape[1]
    tm = QKV_PROJ_TM
    assert m % tm == 0 and seq % tm == 0 and (2 * D_MODEL) % RMS_MATMUL_SUB_TN == 0
    tiles_per_seq = seq // tm
    cast_in_specs, cast_out_specs, cast_out_shapes = _weight_cast_plan(
        cast, cast_li, m // tm, 1, WEIGHT_CAST_STEPS)
    table = pl.BlockSpec((tm, HEAD_DIM), lambda i, j: (i % tiles_per_seq, 0))
    outs = pl.pallas_call(
        functools.partial(_qkv_proj_kernel, n_cast=len(cast)),
        grid=(m // tm, 1),
        in_specs=[pl.BlockSpec((tm, k), lambda i, j: (i, 0)),
                  pl.BlockSpec((1, k), lambda i, j: (0, 0)),
                  pl.BlockSpec((k, n), lambda i, j: (0, 0), pipeline_mode=pl.Buffered(1)),
                  table, table, *cast_in_specs],
        out_specs=[pl.BlockSpec((tm, n), lambda i, j: (i, 0)), *cast_out_specs],
        out_shape=[jax.ShapeDtypeStruct((m, n), F32), *cast_out_shapes],
        compiler_params=_compiler_params(("arbitrary", "arbitrary")),
        name="qkv_proj",
    )(x, g, w, cos_tab, sin_tab, *cast)
    return outs[0], outs[1:]


def _even_mixer_kernel(x_ref, g_pre_ref, w_in_ref, ws_ref, bias_ref, cw_ref, w_out_ref,
                       g_post_ref, *rest, tiles_per_seq, n_cast):
    cast_in = rest[:n_cast]
    o_ref = rest[n_cast]
    cast_out = rest[n_cast + 1:2 * n_cast + 1]
    proj_ref, z_ref = rest[2 * n_cast + 1:]
    tm = x_ref.shape[0]
    i = pl.program_id(0)

    @pl.when(i == 0)
    def _():
        z_ref[...] = jnp.zeros_like(z_ref)

    _cast_chunks(cast_in, cast_out)

    hn = _rmsnorm(x_ref[...], g_pre_ref[...]).astype(BF16)
    sub = RMS_MATMUL_SUB_TN
    for c in range(w_in_ref.shape[1] // sub):
        cols = pl.ds(c * sub, sub)
        p = jnp.dot(hn, w_in_ref[:, cols], preferred_element_type=F32)
        proj_ref[:, cols] = jax.nn.gelu(p) if (c + 1) * sub <= 2 * A_WIDTH else p
    au_ref, av_ref, gb_ref, gc_ref, bx_ref = (
        proj_ref.at[:, pl.ds(s * A_WIDTH, A_WIDTH)] for s in range(5))

    av = av_ref[...]
    mu = jnp.mean(av, axis=-1, keepdims=True)
    cen = av - mu
    var = jnp.mean(cen * cen, axis=-1, keepdims=True)
    vn = (cen * lax.rsqrt(var + LN_EPS)).astype(BF16)
    row = lax.broadcasted_iota(jnp.int32, (CHUNK, CHUNK), 0)
    col = lax.broadcasted_iota(jnp.int32, (CHUNK, CHUNK), 1)
    causal = col <= row
    a_cols = []
    for g in range(A_GROUPS):
        cs = slice(g * A_GROUP_DIM, (g + 1) * A_GROUP_DIM)
        w_causal = jnp.where(causal, ws_ref[g], 0.0).astype(BF16)
        chunks = []
        for c in range(tm // CHUNK):
            rs = slice(c * CHUNK, (c + 1) * CHUNK)
            mixed = jnp.dot(w_causal, vn[rs, cs], preferred_element_type=F32) + bias_ref[:, cs]
            chunks.append((au_ref[rs, cs] * mixed).astype(BF16))
        a_cols.append(jnp.concatenate(chunks, axis=0))

    z = gc_ref[...] * bx_ref[...]
    at_seq_start = (i % tiles_per_seq) == 0
    z_before = jnp.where(at_seq_start, 0.0, z_ref[tm:tm + CONV_HALO, :])
    z_ref[0:CONV_HALO, :] = z_before
    z_ref[CONV_HALO:CONV_HALO + tm, :] = z
    y = (cw_ref[2:3, :] * z
         + cw_ref[1:2, :] * z_ref[CONV_HALO - 1:CONV_HALO - 1 + tm, :]
         + cw_ref[0:1, :] * z_ref[CONV_HALO - 2:CONV_HALO - 2 + tm, :])
    b_out = (gb_ref[...] * y).astype(BF16)

    mixed_ab = jnp.concatenate(a_cols + [b_out], axis=1)
    f = jnp.dot(mixed_ab, w_out_ref[...], preferred_element_type=F32)
    o_ref[...] = x_ref[...] + _rmsnorm(f, g_post_ref[...])


def _even_mixer(h, g_pre, w_in, w_spatial, bias_full, conv_w, w_out, g_post, seq,
                cast=(), cast_li=0):
    m, d = h.shape
    n_in = w_in.shape[1]
    tm = EVEN_MIXER_TM
    assert seq % tm == 0 and tm % CHUNK == 0 and A_WIDTH == B_WIDTH
    assert n_in == 2 * A_WIDTH + 3 * B_WIDTH and (2 * A_WIDTH) % RMS_MATMUL_SUB_TN == 0
    cast_in_specs, cast_out_specs, cast_out_shapes = _weight_cast_plan(
        cast, cast_li, m // tm, 1, WEIGHT_CAST_STEPS)
    const = lambda shape: pl.BlockSpec(shape, lambda i, j: (0,) * len(shape))
    resident = lambda rows, cols: pl.BlockSpec((rows, cols), lambda i, j: (0, 0),
                                               pipeline_mode=pl.Buffered(1))
    kern = functools.partial(_even_mixer_kernel, tiles_per_seq=seq // tm, n_cast=len(cast))
    outs = pl.pallas_call(
        kern,
        grid=(m // tm, 1),
        in_specs=[pl.BlockSpec((tm, d), lambda i, j: (i, 0)),
                  const((1, d)),
                  resident(d, n_in),
                  const((A_GROUPS, CHUNK, CHUNK)),
                  const((CHUNK, A_WIDTH)),
                  const((CONV_WIDTH, B_WIDTH)),
                  resident(d, d),
                  const((1, d)),
                  *cast_in_specs],
        out_specs=[pl.BlockSpec((tm, d), lambda i, j: (i, 0)), *cast_out_specs],
        out_shape=[jax.ShapeDtypeStruct((m, d), F32), *cast_out_shapes],
        scratch_shapes=[pltpu.VMEM((tm, n_in), F32),
                        pltpu.VMEM((CONV_HALO + tm, B_WIDTH), F32)],
        compiler_params=_compiler_params(("arbitrary", "arbitrary")),
        name="even_mixer",
    )(h, g_pre, w_in, w_spatial, bias_full, conv_w, w_out, g_post, *cast)
    return outs[0], outs[1:]


def _attn_band_bias():
    blk = ATTN_BLOCK
    tables = []
    for _, d in DILATED_BRANCHES:
        pieces = ATTN_PERM // d
        rows = blk // pieces
        i = np.arange(blk)
        run = i // rows
        if pieces == ATTN_PERM:
            run = ATTN_PERM_STEP * (run % ATTN_PERM_STEP) + run // ATTN_PERM_STEP
        uq = pieces * (i % rows) + run
        uk = np.concatenate([uq, blk + uq])
        step = uq[:, None] + blk - uk[None, :]
        valid = (step >= 0) & (step <= blk)
        first = valid & (uk[None, :] >= blk)
        tables.append(np.stack([np.where(valid, 0.0, -np.inf), np.where(first, 0.0, -np.inf)]))
    return np.stack(tables).astype(np.float32)


def _attn_block(qb, k_prev, k_cur, v_prev, v_cur, bias, old):
    kw = jnp.concatenate([k_prev, k_cur], axis=0)
    vw = jnp.concatenate([v_prev, v_cur], axis=0)
    s = lax.dot_general(qb, kw, (((1,), (1,)), ((), ())), preferred_element_type=F32) + bias
    m_blk = jnp.max(s, axis=-1, keepdims=True)
    m_new = jnp.broadcast_to(m_blk, (s.shape[0], HEAD_DIM))
    if old is not None:
        m_new = jnp.maximum(old[0], m_new)
    p = jnp.exp2(s - jnp.concatenate([m_new, m_new], axis=1)).astype(BF16)
    v_ones = jnp.concatenate([vw, jnp.ones_like(vw)], axis=1)
    pv = jnp.dot(p, v_ones, preferred_element_type=F32)
    a_new, l_new = pv[:, :HEAD_DIM], pv[:, HEAD_DIM:]
    if old is not None:
        corr = jnp.exp2(old[0] - m_new)
        l_new = old[1] * corr + l_new
        a_new = old[2] * corr + a_new
    return m_new, l_new, a_new


def _attn_kernel(q_ref, k_ref, v_ref, bias_ref, o_ref, qp_ref, kp_ref, vp_ref,
                 qh_ref, kh_ref, vh_ref, acc_ref, m_ref, l_ref, tmp_ref, *, seq, dilations):
    blk = ATTN_BLOCK
    lp = seq // ATTN_PERM
    step = ATTN_PERM_STEP
    lq = seq // step
    q_scale = HEAD_DIM ** -0.5 * math.log2(math.e)
    for src_ref, dst_ref, half_ref, scale in ((q_ref, qp_ref, qh_ref, q_scale),
                                              (k_ref, kp_ref, kh_ref, None),
                                              (v_ref, vp_ref, vh_ref, None)):
        for lo in range(step):
            tmp_ref[lo * lq:(lo + 1) * lq, :] = src_ref[pl.ds(lo, lq, stride=step), :]
        for lo in range(step):
            for hi in range(step):
                run = lo * step + hi
                rows = tmp_ref[pl.ds(lo * lq + hi, lp, stride=step), :]
                if scale is not None:
                    rows = rows * scale
                dst_ref[run * lp:(run + 1) * lp, :] = rows
                half_ref[run * lp:(run + 1) * lp, :] = rows.astype(BF16)

    for bi, d in enumerate(dilations):
        pieces = ATTN_PERM // d
        rows = blk // pieces
        nb = lp // rows
        gn = min(nb, ATTN_GROUP)
        gs = min(ATTN_GROUP // gn, d)
        n_groups = nb // gn
        assert nb % gn == 0 and d % gs == 0
        packed = rows % BF16_SUBLANES == 0
        q_src, k_src, v_src = (qh_ref, kh_ref, vh_ref) if packed else (qp_ref, kp_ref, vp_ref)

        def starts(r_sub, n, d=d, pieces=pieces, rows=rows):
            return [pl.multiple_of((r_sub * pieces + a) * lp + n * rows, rows) for a in range(pieces)]

        def gather(ref, st, rows=rows):
            parts = [ref[pl.ds(s0, rows), :] for s0 in st]
            return parts[0] if len(parts) == 1 else jnp.concatenate(parts, axis=0)

        def gather_half(ref, st):
            return gather(ref, st).astype(BF16)

        def scatter(ref, st, val, rows=rows):
            for a, s0 in enumerate(st):
                ref[pl.ds(s0, rows), :] = val[a * rows:(a + 1) * rows]

        def body(it, carry, bi=bi, gn=gn, gs=gs, n_groups=n_groups,
                 q_src=q_src, k_src=k_src, v_src=v_src):
            sg = it // n_groups
            n0 = (it % n_groups) * gn
            work = []
            for si in range(gs):
                r_sub = sg * gs + si
                st_prev = starts(r_sub, jnp.maximum(n0 - 1, 0))
                kb = [gather_half(k_src, st_prev)]
                vb = [gather_half(v_src, st_prev)]
                for j in range(gn):
                    st = starts(r_sub, n0 + j)
                    kb.append(gather_half(k_src, st))
                    vb.append(gather_half(v_src, st))
                    if j > 0:
                        bias = bias_ref[bi, 0]
                    elif n_groups == 1:
                        bias = bias_ref[bi, 1]
                    else:
                        bias = bias_ref[bi, jnp.where(n0 == 0, 1, 0)]
                    old = None if bi == 0 else (gather(m_ref, st), gather(l_ref, st),
                                                gather(acc_ref, st))
                    work.append((st, gather_half(q_src, st), kb[j], kb[j + 1],
                                 vb[j], vb[j + 1], bias, old))
            results = [(w[0],) + _attn_block(*w[1:]) for w in work]
            for st, m_new, l_new, a_new in results:
                scatter(m_ref, st, m_new)
                scatter(l_ref, st, l_new)
                scatter(acc_ref, st, a_new)
            return carry

        lax.fori_loop(0, (d // gs) * n_groups, body, 0)

    for lo in range(step):
        for hi in range(step):
            src = slice((lo * step + hi) * lp, (lo * step + hi + 1) * lp)
            tmp_ref[pl.ds(lo * lq + hi, lp, stride=step), :] = acc_ref[src, :] / l_ref[src, :]
    for lo in range(step):
        o_ref[pl.ds(lo, lq, stride=step), :] = tmp_ref[lo * lq:(lo + 1) * lq, :]


def _attention(qkv, batch, seq):
    dilations = tuple(d for _, d in DILATED_BRANCHES)
    for window, d in DILATED_BRANCHES:
        assert window // d == ATTN_BLOCK and seq % (d * ATTN_BLOCK) == 0
        assert ATTN_PERM % d == 0 and ATTN_BLOCK % (ATTN_PERM // d) == 0
    qkv3 = qkv.reshape(batch, seq, 3 * D_MODEL)
    bias = jnp.asarray(_attn_band_bias())
    spec = lambda part: pl.BlockSpec((None, seq, HEAD_DIM),
                                     lambda b, h, part=part: (b, 0, part * N_HEADS + h))
    kern = functools.partial(_attn_kernel, seq=seq, dilations=dilations)
    out = pl.pallas_call(
        kern,
        grid=(batch, N_HEADS),
        in_specs=[spec(0), spec(1), spec(2),
                  pl.BlockSpec(bias.shape, lambda b, h: (0, 0, 0, 0))],
        out_specs=pl.BlockSpec((None, seq, HEAD_DIM), lambda b, h: (b, 0, h)),
        out_shape=jax.ShapeDtypeStruct((batch, seq, D_MODEL), F32),
        scratch_shapes=[pltpu.VMEM((seq, HEAD_DIM), F32)] * 3 + [pltpu.VMEM((seq, HEAD_DIM), BF16)] * 3
                       + [pltpu.VMEM((seq, HEAD_DIM), F32)] * 4,
        compiler_params=_compiler_params(("parallel", "parallel")),
        name="attention",
    )(qkv3, qkv3, qkv3, bias)
    return out.reshape(batch * seq, D_MODEL)


def _matmul_rms_res_kernel(a_ref, w_ref, g_ref, h_ref, o_ref):
    f = jnp.dot(a_ref[...].astype(BF16), w_ref[...], preferred_element_type=F32)
    o_ref[...] = h_ref[...] + _rmsnorm(f, g_ref[...])


def _matmul_rms_res(a, w, g, h):
    m, k = a.shape
    n = w.shape[1]
    tm = RES_TM
    assert m % tm == 0
    return pl.pallas_call(
        _matmul_rms_res_kernel,
        grid=(m // tm,),
        in_specs=[pl.BlockSpec((tm, k), lambda i: (i, 0)),
                  pl.BlockSpec((k, n), lambda i: (0, 0)),
                  pl.BlockSpec((1, n), lambda i: (0, 0)),
                  pl.BlockSpec((tm, n), lambda i: (i, 0))],
        out_specs=pl.BlockSpec((tm, n), lambda i: (i, 0)),
        out_shape=jax.ShapeDtypeStruct((m, n), F32),
        compiler_params=_compiler_params(("parallel",)),
        name="matmul_rms_res",
    )(a, w, g, h)


def _mlp_kernel(x_ref, g_pre_ref, w_up_ref, w_down_ref, g_post_ref, *rest, n_cast):
    cast_in = rest[:n_cast]
    o_ref = rest[n_cast]
    cast_out = rest[n_cast + 1:2 * n_cast + 1]
    hn_ref = rest[2 * n_cast + 1]
    k = pl.program_id(1)

    def ffn_step(first, last):
        sub = MLP_SUB
        tm = o_ref.shape[0]
        _cast_chunks(cast_in, cast_out)
        group = tm // MLP_LAST_SPLIT if last else tm
        for r0 in range(0, tm, group):
            rows = pl.ds(r0, group)
            if first:
                hn = _rmsnorm(x_ref[rows, :], g_pre_ref[...]).astype(BF16)
                hn_ref[rows, :] = hn
            else:
                hn = hn_ref[rows, :]
            for c in range(w_up_ref.shape[1] // sub):
                ffn = pl.ds(c * sub, sub)
                u = jnp.dot(hn, w_up_ref[:, ffn], preferred_element_type=F32)
                u = jnp.square(jnp.maximum(u, 0.0)).astype(BF16)
                for n in range(o_ref.shape[1] // sub):
                    cols = pl.ds(n * sub, sub)
                    part = jnp.dot(u, w_down_ref[ffn, cols], preferred_element_type=F32)
                    if first and c == 0:
                        o_ref[rows, cols] = part
                    else:
                        o_ref[rows, cols] += part
            if last:
                for r1 in range(r0, r0 + group, NORM_ROW_CHUNK):
                    chunk = pl.ds(r1, NORM_ROW_CHUNK)
                    o_ref[chunk, :] = x_ref[chunk, :] + _rmsnorm(o_ref[chunk, :], g_post_ref[...])

    n_steps = pl.num_programs(1)

    @pl.when(k == 0)
    def _():
        ffn_step(first=True, last=False)

    @pl.when((k > 0) & (k < n_steps - 1))
    def _():
        ffn_step(first=False, last=False)

    @pl.when(k == n_steps - 1)
    def _():
        ffn_step(first=False, last=True)


def _mlp(h, g_pre, w_up, w_down, g_post, cast=(), cast_li=0):
    m, d = h.shape
    f = w_up.shape[1]
    tm, tf = MLP_TM, MLP_TF
    assert m % tm == 0 and f % tf == 0
    cast_in_specs, cast_out_specs, cast_out_shapes = _weight_cast_plan(
        cast, cast_li, m // tm, f // tf, MLP_CAST_STEPS)
    outs = pl.pallas_call(
        functools.partial(_mlp_kernel, n_cast=len(cast)),
        grid=(m // tm, f // tf),
        in_specs=[pl.BlockSpec((tm, d), lambda i, k: (i, 0)),
                  pl.BlockSpec((1, d), lambda i, k: (0, 0)),
                  pl.BlockSpec((d, tf), lambda i, k: (0, k)),
                  pl.BlockSpec((tf, d), lambda i, k: (k, 0)),
                  pl.BlockSpec((1, d), lambda i, k: (0, 0)),
                  *cast_in_specs],
        out_specs=[pl.BlockSpec((tm, d), lambda i, k: (i, 0)), *cast_out_specs],
        out_shape=[jax.ShapeDtypeStruct((m, d), F32), *cast_out_shapes],
        scratch_shapes=[pltpu.VMEM((tm, d), BF16)],
        compiler_params=_compiler_params(("arbitrary", "arbitrary")),
        name="mlp",
    )(h, g_pre, w_up, w_down, g_post, *cast)
    return outs[0], outs[1:]


def _rope_tables(seq):
    half = HEAD_DIM // 2
    inv_freq = ROPE_THETA ** (-jnp.arange(half, dtype=F32) * 2.0 / HEAD_DIM)
    ang = jnp.arange(seq, dtype=jnp.int32).astype(F32)[:, None] * inv_freq[None, :]
    cos = jnp.cos(ang)
    sin = jnp.sin(ang)
    return jnp.concatenate([cos, cos], axis=-1), jnp.concatenate([-sin, sin], axis=-1)


def kernel(x, norm_mix_pre, norm_mix_post, norm_mlp_pre, norm_mlp_post, w_in_ab, w_spatial,
           b_spatial, conv_w, w_out_ab, w_qkv, w_o, w_up, w_down):
    batch, seq, d = x.shape
    assert d == D_MODEL
    depth = norm_mix_pre.shape[0]
    m = batch * seq
    h = x.reshape(m, d)
    cos_tab, sin_tab = _rope_tables(seq)

    def mixer_weights(layer):
        return (w_in_ab, w_out_ab) if layer % 2 == 0 else (w_qkv, w_o)

    mix_bf = tuple(w[0].astype(BF16) for w in mixer_weights(0))
    mlp_bf = None
    for layer in range(depth):
        g_pre = norm_mix_pre[layer][None, :]
        g_post = norm_mix_post[layer][None, :]
        idx = layer // 2
        cast = mixer_weights(layer + 1) if layer + 1 < depth else ()
        n_next = len(cast)
        if layer == 0:
            cast = cast + (w_up, w_down)
        cast_li = (layer + 1) // 2
        if layer % 2 == 0:
            bias_full = jnp.repeat(b_spatial[idx].T, A_GROUP_DIM, axis=1)
            h, cast_bf = _even_mixer(h, g_pre, mix_bf[0], w_spatial[idx], bias_full, conv_w[idx],
                                     mix_bf[1], g_post, seq, cast=cast, cast_li=cast_li)
        else:
            qkv, cast_bf = _qkv_proj(h, g_pre, mix_bf[0], cos_tab, sin_tab, seq,
                                     cast=cast, cast_li=cast_li)
            att = _attention(qkv, batch, seq)
            h = _matmul_rms_res(att, mix_bf[1], g_post, h)
        mix_bf = cast_bf[:n_next]
        if layer == 0:
            mlp_bf = cast_bf[n_next:]
        next_cast = (w_up, w_down) if layer + 1 < depth else ()
        h, next_bf = _mlp(h, norm_mlp_pre[layer][None, :], mlp_bf[0], mlp_bf[1],
                          norm_mlp_post[layer][None, :], cast=next_cast, cast_li=layer + 1)
        if next_bf:
            mlp_bf = next_bf
    return h.reshape(batch, seq, d)
```

```python
import functools
import math

import jax
import jax.numpy as jnp
import numpy as np
from jax import lax
from jax.experimental import pallas as pl
from jax.experimental.pallas import tpu as pltpu

F32 = jnp.float32
BF16 = jnp.bfloat16

D_MODEL = 2048
A_WIDTH = D_MODEL // 2
B_WIDTH = D_MODEL - A_WIDTH
A_GROUPS = 8
A_GROUP_DIM = A_WIDTH // A_GROUPS
CHUNK = 128
CONV_WIDTH = 3
HEAD_DIM = 128
N_HEADS = D_MODEL // HEAD_DIM
DILATED_BRANCHES = ((128, 1), (512, 4), (2048, 16))
ATTN_BLOCK = 128
ATTN_PERM = 16
ATTN_PERM_STEP = 4
ATTN_GROUP = 32
ROPE_THETA = 10000.0
FFN_DIM = 4 * D_MODEL
RMS_EPS = 1e-6
LN_EPS = 1e-5

V7X_LANES = 128
BF16_SUBLANES = 16
V7X_VMEM_LIMIT_BYTES = 62 * 1024 * 1024

QKV_PROJ_TM = 256
RMS_MATMUL_SUB_TN = 512
EVEN_MIXER_TM = 256
RES_TM = 512
MLP_TM = 1024
MLP_TF = 1024
MLP_SUB = 512
CONV_HALO = 8
MLP_LAST_SPLIT = 2
NORM_ROW_CHUNK = 16
WEIGHT_CAST_STEPS = 64
MLP_CAST_STEPS = 128


def _compiler_params(semantics):
    return pltpu.CompilerParams(dimension_semantics=semantics,
                                vmem_limit_bytes=V7X_VMEM_LIMIT_BYTES)


def _rmsnorm(x, g):
    ms = jnp.mean(x * x, axis=-1, keepdims=True)
    return (x * lax.rsqrt(ms + RMS_EPS)) * g


def _weight_cast_plan(cast, cast_li, n_outer, n_inner, max_steps):
    steps = min(max_steps, 2 ** int(math.log2(n_outer * n_inner)))
    chunk = lambda i, j: jnp.minimum(i * n_inner + j, steps - 1)
    in_specs, out_specs, out_shapes = [], [], []
    for c in cast:
        _, rows, width = c.shape
        assert rows % (steps * BF16_SUBLANES) == 0
        rows_per_step = rows // steps
        in_specs.append(pl.BlockSpec((None, rows_per_step, width),
                                     lambda i, j: (cast_li, chunk(i, j), 0)))
        out_specs.append(pl.BlockSpec((rows_per_step, width), lambda i, j: (chunk(i, j), 0)))
        out_shapes.append(jax.ShapeDtypeStruct((rows, width), BF16))
    return in_specs, out_specs, out_shapes


def _cast_chunks(cast_in, cast_out):
    for src, dst in zip(cast_in, cast_out):
        dst[...] = src[...].astype(BF16)


def _qkv_proj_kernel(x_ref, g_ref, w_ref, cos_ref, sin_ref, *rest, n_cast):
    cast_in = rest[:n_cast]
    o_ref = rest[n_cast]
    cast_out = rest[n_cast + 1:2 * n_cast + 1]
    slab_ref, pass_ref = rest[2 * n_cast + 1:]
    _cast_chunks(cast_in, cast_out)
    hn = _rmsnorm(x_ref[...], g_ref[...]).astype(BF16)
    cos = cos_ref[...]
    sin = sin_ref[...]
    q_scale = HEAD_DIM ** -0.5 * math.log2(math.e)
    sub = RMS_MATMUL_SUB_TN
    rows_per_run = x_ref.shape[0] // ATTN_PERM
    for c in range(w_ref.shape[1] // sub):
        p = jnp.dot(hn, w_ref[:, pl.ds(c * sub, sub)], preferred_element_type=F32)
        for hh in range(sub // HEAD_DIM):
            col = c * sub + hh * HEAD_DIM
            t = p[:, hh * HEAD_DIM:(hh + 1) * HEAD_DIM]
            if col < 2 * D_MODEL:
                t = t * cos + pltpu.roll(t, HEAD_DIM // 2, 1) * sin
            if col < D_MODEL:
                t = t * q_scale
            slab = c * (sub // HEAD_DIM) + hh
            slab_ref[slab] = t
            st = ATTN_PERM_STEP
            quarter = x_ref.shape[0] // st
            for lo in range(st):
                pass_ref[slab, lo * quarter:(lo + 1) * quarter, :] = (
                    slab_ref[slab, pl.ds(lo, quarter, stride=st), :])
            for lo in range(st):
                for hi in range(st):
                    o_ref[lo * st + hi, :, pl.ds(col, HEAD_DIM)] = (
                        pass_ref[slab, pl.ds(lo * quarter + hi, rows_per_run, stride=st), :])


def _qkv_proj(x, g, w, cos_tab, sin_tab, batch, seq, cast=(), cast_li=0):
    m, k = x.shape
    n = w.shape[1]
    tm = QKV_PROJ_TM
    assert m % tm == 0 and seq % tm == 0 and (2 * D_MODEL) % RMS_MATMUL_SUB_TN == 0
    assert tm % (ATTN_PERM * 8) == 0 and ATTN_PERM == ATTN_PERM_STEP ** 2
    tiles_per_seq = seq // tm
    cast_in_specs, cast_out_specs, cast_out_shapes = _weight_cast_plan(
        cast, cast_li, m // tm, 1, WEIGHT_CAST_STEPS)
    table = pl.BlockSpec((tm, HEAD_DIM), lambda i, j: (i % tiles_per_seq, 0))
    outs = pl.pallas_call(
        functools.partial(_qkv_proj_kernel, n_cast=len(cast)),
        grid=(m // tm, 1),
        in_specs=[pl.BlockSpec((tm, k), lambda i, j: (i, 0)),
                  pl.BlockSpec((1, k), lambda i, j: (0, 0)),
                  pl.BlockSpec((k, n), lambda i, j: (0, 0), pipeline_mode=pl.Buffered(1)),
                  table, table, *cast_in_specs],
        out_specs=[pl.BlockSpec((None, ATTN_PERM, tm // ATTN_PERM, n),
                                lambda i, j: (i // tiles_per_seq, 0, i % tiles_per_seq, 0)),
                   *cast_out_specs],
        out_shape=[jax.ShapeDtypeStruct((batch, ATTN_PERM, seq // ATTN_PERM, n), F32),
                   *cast_out_shapes],
        scratch_shapes=[pltpu.VMEM((n // HEAD_DIM, tm, HEAD_DIM), F32)] * 2,
        compiler_params=_compiler_params(("arbitrary", "arbitrary")),
        name="qkv_proj",
    )(x, g, w, cos_tab, sin_tab, *cast)
    return outs[0].reshape(batch, seq, n), outs[1:]


def _even_mixer_kernel(x_ref, g_pre_ref, w_in_ref, ws_ref, bias_ref, cw_ref, w_out_ref,
                       g_post_ref, *rest, tiles_per_seq, n_cast):
    cast_in = rest[:n_cast]
    o_ref = rest[n_cast]
    cast_out = rest[n_cast + 1:2 * n_cast + 1]
    proj_ref, z_ref = rest[2 * n_cast + 1:]
    tm = x_ref.shape[0]
    i = pl.program_id(0)

    @pl.when(i == 0)
    def _():
        z_ref[...] = jnp.zeros_like(z_ref)

    _cast_chunks(cast_in, cast_out)

    hn = _rmsnorm(x_ref[...], g_pre_ref[...]).astype(BF16)
    sub = RMS_MATMUL_SUB_TN
    for c in range(w_in_ref.shape[1] // sub):
        cols = pl.ds(c * sub, sub)
        p = jnp.dot(hn, w_in_ref[:, cols], preferred_element_type=F32)
        proj_ref[:, cols] = jax.nn.gelu(p) if (c + 1) * sub <= 2 * A_WIDTH else p
    au_ref, av_ref, gb_ref, gc_ref, bx_ref = (
        proj_ref.at[:, pl.ds(s * A_WIDTH, A_WIDTH)] for s in range(5))

    av = av_ref[...]
    mu = jnp.mean(av, axis=-1, keepdims=True)
    cen = av - mu
    var = jnp.mean(cen * cen, axis=-1, keepdims=True)
    vn = (cen * lax.rsqrt(var + LN_EPS)).astype(BF16)
    row = lax.broadcasted_iota(jnp.int32, (CHUNK, CHUNK), 0)
    col = lax.broadcasted_iota(jnp.int32, (CHUNK, CHUNK), 1)
    causal = col <= row
    a_cols = []
    for g in range(A_GROUPS):
        cs = slice(g * A_GROUP_DIM, (g + 1) * A_GROUP_DIM)
        w_causal = jnp.where(causal, ws_ref[g], 0.0).astype(BF16)
        chunks = []
        for c in range(tm // CHUNK):
            rs = slice(c * CHUNK, (c + 1) * CHUNK)
            mixed = jnp.dot(w_causal, vn[rs, cs], preferred_element_type=F32) + bias_ref[:, cs]
            chunks.append((au_ref[rs, cs] * mixed).astype(BF16))
        a_cols.append(jnp.concatenate(chunks, axis=0))

    z = gc_ref[...] * bx_ref[...]
    at_seq_start = (i % tiles_per_seq) == 0
    z_before = jnp.where(at_seq_start, 0.0, z_ref[tm:tm + CONV_HALO, :])
    z_ref[0:CONV_HALO, :] = z_before
    z_ref[CONV_HALO:CONV_HALO + tm, :] = z
    y = (cw_ref[2:3, :] * z
         + cw_ref[1:2, :] * z_ref[CONV_HALO - 1:CONV_HALO - 1 + tm, :]
         + cw_ref[0:1, :] * z_ref[CONV_HALO - 2:CONV_HALO - 2 + tm, :])
    b_out = (gb_ref[...] * y).astype(BF16)

    mixed_ab = jnp.concatenate(a_cols + [b_out], axis=1)
    f = jnp.dot(mixed_ab, w_out_ref[...], preferred_element_type=F32)
    o_ref[...] = x_ref[...] + _rmsnorm(f, g_post_ref[...])


def _even_mixer(h, g_pre, w_in, w_spatial, bias_full, conv_w, w_out, g_post, seq,
                cast=(), cast_li=0):
    m, d = h.shape
    n_in = w_in.shape[1]
    tm = EVEN_MIXER_TM
    assert seq % tm == 0 and tm % CHUNK == 0 and A_WIDTH == B_WIDTH
    assert n_in == 2 * A_WIDTH + 3 * B_WIDTH and (2 * A_WIDTH) % RMS_MATMUL_SUB_TN == 0
    cast_in_specs, cast_out_specs, cast_out_shapes = _weight_cast_plan(
        cast, cast_li, m // tm, 1, WEIGHT_CAST_STEPS)
    const = lambda shape: pl.BlockSpec(shape, lambda i, j: (0,) * len(shape))
    resident = lambda rows, cols: pl.BlockSpec((rows, cols), lambda i, j: (0, 0),
                                               pipeline_mode=pl.Buffered(1))
    kern = functools.partial(_even_mixer_kernel, tiles_per_seq=seq // tm, n_cast=len(cast))
    outs = pl.pallas_call(
        kern,
        grid=(m // tm, 1),
        in_specs=[pl.BlockSpec((tm, d), lambda i, j: (i, 0)),
                  const((1, d)),
                  resident(d, n_in),
                  const((A_GROUPS, CHUNK, CHUNK)),
                  const((CHUNK, A_WIDTH)),
                  const((CONV_WIDTH, B_WIDTH)),
                  resident(d, d),
                  const((1, d)),
                  *cast_in_specs],
        out_specs=[pl.BlockSpec((tm, d), lambda i, j: (i, 0)), *cast_out_specs],
        out_shape=[jax.ShapeDtypeStruct((m, d), F32), *cast_out_shapes],
        scratch_shapes=[pltpu.VMEM((tm, n_in), F32),
                        pltpu.VMEM((CONV_HALO + tm, B_WIDTH), F32)],
        compiler_params=_compiler_params(("arbitrary", "arbitrary")),
        name="even_mixer",
    )(h, g_pre, w_in, w_spatial, bias_full, conv_w, w_out, g_post, *cast)
    return outs[0], outs[1:]


def _attn_band_bias():
    blk = ATTN_BLOCK
    tables = []
    for _, d in DILATED_BRANCHES:
        pieces = ATTN_PERM // d
        rows = blk // pieces
        i = np.arange(blk)
        run = i // rows
        if pieces == ATTN_PERM:
            run = ATTN_PERM_STEP * (run % ATTN_PERM_STEP) + run // ATTN_PERM_STEP
        uq = pieces * (i % rows) + run
        uk = np.concatenate([uq, blk + uq])
        step = uq[:, None] + blk - uk[None, :]
        valid = (step >= 0) & (step <= blk)
        first = valid & (uk[None, :] >= blk)
        tables.append(np.stack([np.where(valid, 0.0, -np.inf), np.where(first, 0.0, -np.inf)]))
    return np.stack(tables).astype(np.float32)


def _attn_block(qb, k_prev, k_cur, v_prev, v_cur, bias, old):
    kw = jnp.concatenate([k_prev, k_cur], axis=0)
    vw = jnp.concatenate([v_prev, v_cur], axis=0)
    s = lax.dot_general(qb, kw, (((1,), (1,)), ((), ())), preferred_element_type=F32) + bias
    m_blk = jnp.max(s, axis=-1, keepdims=True)
    m_new = jnp.broadcast_to(m_blk, (s.shape[0], HEAD_DIM))
    if old is not None:
        m_new = jnp.maximum(old[0], m_new)
    p = jnp.exp2(s - jnp.concatenate([m_new, m_new], axis=1)).astype(BF16)
    v_ones = jnp.concatenate([vw, jnp.ones_like(vw)], axis=1)
    pv = jnp.dot(p, v_ones, preferred_element_type=F32)
    a_new, l_new = pv[:, :HEAD_DIM], pv[:, HEAD_DIM:]
    if old is not None:
        corr = jnp.exp2(old[0] - m_new)
        l_new = old[1] * corr + l_new
        a_new = old[2] * corr + a_new
    return m_new, l_new, a_new


def _attn_kernel(qp_ref, kp_ref, vp_ref, bias_ref, o_ref, qh_ref, kh_ref, vh_ref,
                 acc_ref, m_ref, l_ref, tmp_ref, *, seq, dilations):
    blk = ATTN_BLOCK
    lp = seq // ATTN_PERM
    step = ATTN_PERM_STEP
    lq = seq // step
    for src_ref, half_ref in ((qp_ref, qh_ref), (kp_ref, kh_ref), (vp_ref, vh_ref)):
        half_ref[...] = src_ref[...].astype(BF16)

    for bi, d in enumerate(dilations):
        pieces = ATTN_PERM // d
        rows = blk // pieces
        nb = lp // rows
        gn = min(nb, ATTN_GROUP)
        gs = min(ATTN_GROUP // gn, d)
        n_groups = nb // gn
        assert nb % gn == 0 and d % gs == 0
        packed = rows % BF16_SUBLANES == 0
        q_src, k_src, v_src = (qh_ref, kh_ref, vh_ref) if packed else (qp_ref, kp_ref, vp_ref)

        def starts(r_sub, n, d=d, pieces=pieces, rows=rows):
            return [pl.multiple_of((r_sub * pieces + a) * lp + n * rows, rows) for a in range(pieces)]

        def gather(ref, st, rows=rows):
            parts = [ref[pl.ds(s0, rows), :] for s0 in st]
            return parts[0] if len(parts) == 1 else jnp.concatenate(parts, axis=0)

        def gather_half(ref, st):
            return gather(ref, st).astype(BF16)

        def scatter(ref, st, val, rows=rows):
            for a, s0 in enumerate(st):
                ref[pl.ds(s0, rows), :] = val[a * rows:(a + 1) * rows]

        def body(it, carry, bi=bi, gn=gn, gs=gs, n_groups=n_groups,
                 q_src=q_src, k_src=k_src, v_src=v_src):
            sg = it // n_groups
            n0 = (it % n_groups) * gn
            work = []
            for si in range(gs):
                r_sub = sg * gs + si
                st_prev = starts(r_sub, jnp.maximum(n0 - 1, 0))
                kb = [gather_half(k_src, st_prev)]
                vb = [gather_half(v_src, st_prev)]
                for j in range(gn):
                    st = starts(r_sub, n0 + j)
                    kb.append(gather_half(k_src, st))
                    vb.append(gather_half(v_src, st))
                    if j > 0:
                        bias = bias_ref[bi, 0]
                    elif n_groups == 1:
                        bias = bias_ref[bi, 1]
                    else:
                        bias = bias_ref[bi, jnp.where(n0 == 0, 1, 0)]
                    old = None if bi == 0 else (gather(m_ref, st), gather(l_ref, st),
                                                gather(acc_ref, st))
                    work.append((st, gather_half(q_src, st), kb[j], kb[j + 1],
                                 vb[j], vb[j + 1], bias, old))
            results = [(w[0],) + _attn_block(*w[1:]) for w in work]
            for st, m_new, l_new, a_new in results:
                scatter(m_ref, st, m_new)
                scatter(l_ref, st, l_new)
                scatter(acc_ref, st, a_new)
            return carry

        lax.fori_loop(0, (d // gs) * n_groups, body, 0)

    for lo in range(step):
        for hi in range(step):
            src = slice((lo * step + hi) * lp, (lo * step + hi + 1) * lp)
            tmp_ref[pl.ds(lo * lq + hi, lp, stride=step), :] = acc_ref[src, :] / l_ref[src, :]
    for lo in range(step):
        o_ref[pl.ds(lo, lq, stride=step), :] = tmp_ref[lo * lq:(lo + 1) * lq, :]


def _attention(qkv3, batch, seq):
    dilations = tuple(d for _, d in DILATED_BRANCHES)
    for window, d in DILATED_BRANCHES:
        assert window // d == ATTN_BLOCK and seq % (d * ATTN_BLOCK) == 0
        assert ATTN_PERM % d == 0 and ATTN_BLOCK % (ATTN_PERM // d) == 0
    bias = jnp.asarray(_attn_band_bias())
    spec = lambda part: pl.BlockSpec((None, seq, HEAD_DIM),
                                     lambda b, h, part=part: (b, 0, part * N_HEADS + h))
    kern = functools.partial(_attn_kernel, seq=seq, dilations=dilations)
    out = pl.pallas_call(
        kern,
        grid=(batch, N_HEADS),
        in_specs=[spec(0), spec(1), spec(2),
                  pl.BlockSpec(bias.shape, lambda b, h: (0, 0, 0, 0))],
        out_specs=pl.BlockSpec((None, seq, HEAD_DIM), lambda b, h: (b, 0, h)),
        out_shape=jax.ShapeDtypeStruct((batch, seq, D_MODEL), F32),
        scratch_shapes=[pltpu.VMEM((seq, HEAD_DIM), BF16)] * 3
                       + [pltpu.VMEM((seq, HEAD_DIM), F32)] * 4,
        compiler_params=_compiler_params(("parallel", "parallel")),
        name="attention",
    )(qkv3, qkv3, qkv3, bias)
    return out.reshape(batch * seq, D_MODEL)


def _matmul_rms_res_kernel(a_ref, w_ref, g_ref, h_ref, o_ref):
    f = jnp.dot(a_ref[...].astype(BF16), w_ref[...], preferred_element_type=F32)
    o_ref[...] = h_ref[...] + _rmsnorm(f, g_ref[...])


def _matmul_rms_res(a, w, g, h):
    m, k = a.shape
    n = w.shape[1]
    tm = RES_TM
    assert m % tm == 0
    return pl.pallas_call(
        _matmul_rms_res_kernel,
        grid=(m // tm,),
        in_specs=[pl.BlockSpec((tm, k), lambda i: (i, 0)),
                  pl.BlockSpec((k, n), lambda i: (0, 0)),
                  pl.BlockSpec((1, n), lambda i: (0, 0)),
                  pl.BlockSpec((tm, n), lambda i: (i, 0))],
        out_specs=pl.BlockSpec((tm, n), lambda i: (i, 0)),
        out_shape=jax.ShapeDtypeStruct((m, n), F32),
        compiler_params=_compiler_params(("parallel",)),
        name="matmul_rms_res",
    )(a, w, g, h)


def _mlp_kernel(x_ref, g_pre_ref, w_up_ref, w_down_ref, g_post_ref, *rest, n_cast):
    cast_in = rest[:n_cast]
    o_ref = rest[n_cast]
    cast_out = rest[n_cast + 1:2 * n_cast + 1]
    hn_ref = rest[2 * n_cast + 1]
    k = pl.program_id(1)

    def ffn_step(first, last):
        sub = MLP_SUB
        tm = o_ref.shape[0]
        _cast_chunks(cast_in, cast_out)
        group = tm // MLP_LAST_SPLIT if last else tm
        for r0 in range(0, tm, group):
            rows = pl.ds(r0, group)
            if first:
                hn = _rmsnorm(x_ref[rows, :], g_pre_ref[...]).astype(BF16)
                hn_ref[rows, :] = hn
            else:
                hn = hn_ref[rows, :]
            for c in range(w_up_ref.shape[1] // sub):
                ffn = pl.ds(c * sub, sub)
                u = jnp.dot(hn, w_up_ref[:, ffn], preferred_element_type=F32)
                u = jnp.square(jnp.maximum(u, 0.0)).astype(BF16)
                for n in range(o_ref.shape[1] // sub):
                    cols = pl.ds(n * sub, sub)
                    part = jnp.dot(u, w_down_ref[ffn, cols], preferred_element_type=F32)
                    if first and c == 0:
                        o_ref[rows, cols] = part
                    else:
                        o_ref[rows, cols] += part
            if last:
                for r1 in range(r0, r0 + group, NORM_ROW_CHUNK):
                    chunk = pl.ds(r1, NORM_ROW_CHUNK)
                    o_ref[chunk, :] = x_ref[chunk, :] + _rmsnorm(o_ref[chunk, :], g_post_ref[...])

    n_steps = pl.num_programs(1)

    @pl.when(k == 0)
    def _():
        ffn_step(first=True, last=False)

    @pl.when((k > 0) & (k < n_steps - 1))
    def _():
        ffn_step(first=False, last=False)

    @pl.when(k == n_steps - 1)
    def _():
        ffn_step(first=False, last=True)


def _mlp(h, g_pre, w_up, w_down, g_post, cast=(), cast_li=0):
    m, d = h.shape
    f = w_up.shape[1]
    tm, tf = MLP_TM, MLP_TF
    assert m % tm == 0 and f % tf == 0
    cast_in_specs, cast_out_specs, cast_out_shapes = _weight_cast_plan(
        cast, cast_li, m // tm, f // tf, MLP_CAST_STEPS)
    outs = pl.pallas_call(
        functools.partial(_mlp_kernel, n_cast=len(cast)),
        grid=(m // tm, f // tf),
        in_specs=[pl.BlockSpec((tm, d), lambda i, k: (i, 0)),
                  pl.BlockSpec((1, d), lambda i, k: (0, 0)),
                  pl.BlockSpec((d, tf), lambda i, k: (0, k)),
                  pl.BlockSpec((tf, d), lambda i, k: (k, 0)),
                  pl.BlockSpec((1, d), lambda i, k: (0, 0)),
                  *cast_in_specs],
        out_specs=[pl.BlockSpec((tm, d), lambda i, k: (i, 0)), *cast_out_specs],
        out_shape=[jax.ShapeDtypeStruct((m, d), F32), *cast_out_shapes],
        scratch_shapes=[pltpu.VMEM((tm, d), BF16)],
        compiler_params=_compiler_params(("arbitrary", "arbitrary")),
        name="mlp",
    )(h, g_pre, w_up, w_down, g_post, *cast)
    return outs[0], outs[1:]


def _rope_tables(seq):
    half = HEAD_DIM // 2
    inv_freq = ROPE_THETA ** (-jnp.arange(half, dtype=F32) * 2.0 / HEAD_DIM)
    ang = jnp.arange(seq, dtype=jnp.int32).astype(F32)[:, None] * inv_freq[None, :]
    cos = jnp.cos(ang)
    sin = jnp.sin(ang)
    return jnp.concatenate([cos, cos], axis=-1), jnp.concatenate([-sin, sin], axis=-1)


def kernel(x, norm_mix_pre, norm_mix_post, norm_mlp_pre, norm_mlp_post, w_in_ab, w_spatial,
           b_spatial, conv_w, w_out_ab, w_qkv, w_o, w_up, w_down):
    batch, seq, d = x.shape
    assert d == D_MODEL
    depth = norm_mix_pre.shape[0]
    m = batch * seq
    h = x.reshape(m, d)
    cos_tab, sin_tab = _rope_tables(seq)

    def mixer_weights(layer):
        return (w_in_ab, w_out_ab) if layer % 2 == 0 else (w_qkv, w_o)

    mix_bf = tuple(w[0].astype(BF16) for w in mixer_weights(0))
    mlp_bf = None
    for layer in range(depth):
        g_pre = norm_mix_pre[layer][None, :]
        g_post = norm_mix_post[layer][None, :]
        idx = layer // 2
        cast = mixer_weights(layer + 1) if layer + 1 < depth else ()
        n_next = len(cast)
        if layer == 0:
            cast = cast + (w_up, w_down)
        cast_li = (layer + 1) // 2
        if layer % 2 == 0:
            bias_full = jnp.repeat(b_spatial[idx].T, A_GROUP_DIM, axis=1)
            h, cast_bf = _even_mixer(h, g_pre, mix_bf[0], w_spatial[idx], bias_full, conv_w[idx],
                                     mix_bf[1], g_post, seq, cast=cast, cast_li=cast_li)
        else:
            qkv, cast_bf = _qkv_proj(h, g_pre, mix_bf[0], cos_tab, sin_tab, batch, seq,
                                     cast=cast, cast_li=cast_li)
            att = _attention(qkv, batch, seq)
            h = _matmul_rms_res(att, mix_bf[1], g_post, h)
        mix_bf = cast_bf[:n_next]
        if layer == 0:
            mlp_bf = cast_bf[n_next:]
        next_cast = (w_up, w_down) if layer + 1 < depth else ()
        h, next_bf = _mlp(h, norm_mlp_pre[layer][None, :], mlp_bf[0], mlp_bf[1],
                          norm_mlp_post[layer][None, :], cast=next_cast, cast_li=layer + 1)
        if next_bf:
            mlp_bf = next_bf
    return h.reshape(batch, seq, d)
```

```python
import functools
import math

import jax
import jax.numpy as jnp
import numpy as np
from jax import lax
from jax.experimental import pallas as pl
from jax.experimental.pallas import tpu as pltpu

F32 = jnp.float32
BF16 = jnp.bfloat16

D_MODEL = 2048
A_WIDTH = D_MODEL // 2
B_WIDTH = D_MODEL - A_WIDTH
A_GROUPS = 8
A_GROUP_DIM = A_WIDTH // A_GROUPS
CHUNK = 128
CONV_WIDTH = 3
HEAD_DIM = 128
N_HEADS = D_MODEL // HEAD_DIM
DILATED_BRANCHES = ((128, 1), (512, 4), (2048, 16))
ATTN_BLOCK = 128
ATTN_PERM = 16
ATTN_PERM_STEP = 4
ATTN_GROUP = 32
ROPE_THETA = 10000.0
RMS_EPS = 1e-6
LN_EPS = 1e-5

BF16_SUBLANES = 16
V7X_VMEM_LIMIT_BYTES = 62 * 1024 * 1024

QKV_PROJ_TM = 256
RMS_MATMUL_SUB_TN = 512
EVEN_MIXER_TM = 256
RES_TM = 512
MLP_TM = 1024
MLP_TF = 1024
MLP_SUB = 512
CONV_HALO = 8
MLP_LAST_SPLIT = 2
NORM_ROW_CHUNK = 16
WEIGHT_CAST_STEPS = 64
MLP_CAST_STEPS = 128


def _compiler_params(semantics):
    return pltpu.CompilerParams(dimension_semantics=semantics,
                                vmem_limit_bytes=V7X_VMEM_LIMIT_BYTES)


def _rmsnorm(x, g):
    ms = jnp.mean(x * x, axis=-1, keepdims=True)
    return (x * lax.rsqrt(ms + RMS_EPS)) * g


def _weight_cast_plan(cast, cast_li, n_outer, n_inner, max_steps):
    steps = min(max_steps, 2 ** int(math.log2(n_outer * n_inner)))
    chunk = lambda i, j: jnp.minimum(i * n_inner + j, steps - 1)
    in_specs, out_specs, out_shapes = [], [], []
    for c in cast:
        _, rows, width = c.shape
        assert rows % (steps * BF16_SUBLANES) == 0
        rows_per_step = rows // steps
        in_specs.append(pl.BlockSpec((None, rows_per_step, width),
                                     lambda i, j: (cast_li, chunk(i, j), 0)))
        out_specs.append(pl.BlockSpec((rows_per_step, width), lambda i, j: (chunk(i, j), 0)))
        out_shapes.append(jax.ShapeDtypeStruct((rows, width), BF16))
    return in_specs, out_specs, out_shapes


def _cast_chunks(cast_in, cast_out):
    for src, dst in zip(cast_in, cast_out):
        dst[...] = src[...].astype(BF16)


def _qkv_proj_kernel(x_ref, g_ref, w_ref, cos_ref, sin_ref, *rest, n_cast):
    cast_in = rest[:n_cast]
    o_ref = rest[n_cast]
    cast_out = rest[n_cast + 1:]
    _cast_chunks(cast_in, cast_out)
    hn = _rmsnorm(x_ref[...], g_ref[...]).astype(BF16)
    cos = cos_ref[...]
    sin = sin_ref[...]
    sub = RMS_MATMUL_SUB_TN
    for c in range(w_ref.shape[1] // sub):
        p = jnp.dot(hn, w_ref[:, pl.ds(c * sub, sub)], preferred_element_type=F32)
        if (c + 1) * sub <= 2 * D_MODEL:
            for hh in range(sub // HEAD_DIM):
                t = p[:, hh * HEAD_DIM:(hh + 1) * HEAD_DIM]
                o_ref[:, pl.ds(c * sub + hh * HEAD_DIM, HEAD_DIM)] = (
                    t * cos + pltpu.roll(t, HEAD_DIM // 2, 1) * sin)
        else:
            o_ref[:, pl.ds(c * sub, sub)] = p


def _qkv_proj(x, g, w, cos_tab, sin_tab, seq, cast=(), cast_li=0):
    m, k = x.shape
    n = w.shape[1]
    tm = QKV_PROJ_TM
    assert m % tm == 0 and seq % tm == 0 and (2 * D_MODEL) % RMS_MATMUL_SUB_TN == 0
    tiles_per_seq = seq // tm
    cast_in_specs, cast_out_specs, cast_out_shapes = _weight_cast_plan(
        cast, cast_li, m // tm, 1, WEIGHT_CAST_STEPS)
    table = pl.BlockSpec((tm, HEAD_DIM), lambda i, j: (i % tiles_per_seq, 0))
    outs = pl.pallas_call(
        functools.partial(_qkv_proj_kernel, n_cast=len(cast)),
        grid=(m // tm, 1),
        in_specs=[pl.BlockSpec((tm, k), lambda i, j: (i, 0)),
                  pl.BlockSpec((1, k), lambda i, j: (0, 0)),
                  pl.BlockSpec((k, n), lambda i, j: (0, 0), pipeline_mode=pl.Buffered(1)),
                  table, table, *cast_in_specs],
        out_specs=[pl.BlockSpec((tm, n), lambda i, j: (i, 0)), *cast_out_specs],
        out_shape=[jax.ShapeDtypeStruct((m, n), F32), *cast_out_shapes],
        compiler_params=_compiler_params(("arbitrary", "arbitrary")),
        name="qkv_proj",
    )(x, g, w, cos_tab, sin_tab, *cast)
    return outs[0], outs[1:]


def _even_mixer_kernel(x_ref, g_pre_ref, w_in_ref, ws_ref, bias_ref, cw_ref, w_out_ref,
                       g_post_ref, *rest, tiles_per_seq, n_cast):
    cast_in = rest[:n_cast]
    o_ref = rest[n_cast]
    cast_out = rest[n_cast + 1:2 * n_cast + 1]
    proj_ref, z_ref = rest[2 * n_cast + 1:]
    tm = x_ref.shape[0]
    i = pl.program_id(0)

    @pl.when(i == 0)
    def _():
        z_ref[...] = jnp.zeros_like(z_ref)

    _cast_chunks(cast_in, cast_out)

    hn = _rmsnorm(x_ref[...], g_pre_ref[...]).astype(BF16)
    sub = RMS_MATMUL_SUB_TN
    for c in range(w_in_ref.shape[1] // sub):
        cols = pl.ds(c * sub, sub)
        p = jnp.dot(hn, w_in_ref[:, cols], preferred_element_type=F32)
        proj_ref[:, cols] = jax.nn.gelu(p) if (c + 1) * sub <= 2 * A_WIDTH else p
    au_ref, av_ref, gb_ref, gc_ref, bx_ref = (
        proj_ref.at[:, pl.ds(s * A_WIDTH, A_WIDTH)] for s in range(5))

    av = av_ref[...]
    mu = jnp.mean(av, axis=-1, keepdims=True)
    cen = av - mu
    var = jnp.mean(cen * cen, axis=-1, keepdims=True)
    vn = (cen * lax.rsqrt(var + LN_EPS)).astype(BF16)
    row = lax.broadcasted_iota(jnp.int32, (CHUNK, CHUNK), 0)
    col = lax.broadcasted_iota(jnp.int32, (CHUNK, CHUNK), 1)
    causal = col <= row
    a_cols = []
    for g in range(A_GROUPS):
        cs = slice(g * A_GROUP_DIM, (g + 1) * A_GROUP_DIM)
        w_causal = jnp.where(causal, ws_ref[g], 0.0).astype(BF16)
        chunks = []
        for c in range(tm // CHUNK):
            rs = slice(c * CHUNK, (c + 1) * CHUNK)
            mixed = jnp.dot(w_causal, vn[rs, cs], preferred_element_type=F32) + bias_ref[:, cs]
            chunks.append((au_ref[rs, cs] * mixed).astype(BF16))
        a_cols.append(jnp.concatenate(chunks, axis=0))

    z = gc_ref[...] * bx_ref[...]
    at_seq_start = (i % tiles_per_seq) == 0
    z_before = jnp.where(at_seq_start, 0.0, z_ref[tm:tm + CONV_HALO, :])
    z_ref[0:CONV_HALO, :] = z_before
    z_ref[CONV_HALO:CONV_HALO + tm, :] = z
    y = (cw_ref[2:3, :] * z
         + cw_ref[1:2, :] * z_ref[CONV_HALO - 1:CONV_HALO - 1 + tm, :]
         + cw_ref[0:1, :] * z_ref[CONV_HALO - 2:CONV_HALO - 2 + tm, :])
    b_out = (gb_ref[...] * y).astype(BF16)

    mixed_ab = jnp.concatenate(a_cols + [b_out], axis=1)
    f = jnp.dot(mixed_ab, w_out_ref[...], preferred_element_type=F32)
    o_ref[...] = x_ref[...] + _rmsnorm(f, g_post_ref[...])


def _even_mixer(h, g_pre, w_in, w_spatial, bias_full, conv_w, w_out, g_post, seq,
                cast=(), cast_li=0):
    m, d = h.shape
    n_in = w_in.shape[1]
    tm = EVEN_MIXER_TM
    assert seq % tm == 0 and tm % CHUNK == 0 and A_WIDTH == B_WIDTH
    assert n_in == 2 * A_WIDTH + 3 * B_WIDTH and (2 * A_WIDTH) % RMS_MATMUL_SUB_TN == 0
    cast_in_specs, cast_out_specs, cast_out_shapes = _weight_cast_plan(
        cast, cast_li, m // tm, 1, WEIGHT_CAST_STEPS)
    const = lambda shape: pl.BlockSpec(shape, lambda i, j: (0,) * len(shape))
    resident = lambda rows, cols: pl.BlockSpec((rows, cols), lambda i, j: (0, 0),
                                               pipeline_mode=pl.Buffered(1))
    kern = functools.partial(_even_mixer_kernel, tiles_per_seq=seq // tm, n_cast=len(cast))
    outs = pl.pallas_call(
        kern,
        grid=(m // tm, 1),
        in_specs=[pl.BlockSpec((tm, d), lambda i, j: (i, 0)),
                  const((1, d)),
                  resident(d, n_in),
                  const((A_GROUPS, CHUNK, CHUNK)),
                  const((CHUNK, A_WIDTH)),
                  const((CONV_WIDTH, B_WIDTH)),
                  resident(d, d),
                  const((1, d)),
                  *cast_in_specs],
        out_specs=[pl.BlockSpec((tm, d), lambda i, j: (i, 0)), *cast_out_specs],
        out_shape=[jax.ShapeDtypeStruct((m, d), F32), *cast_out_shapes],
        scratch_shapes=[pltpu.VMEM((tm, n_in), F32),
                        pltpu.VMEM((CONV_HALO + tm, B_WIDTH), F32)],
        compiler_params=_compiler_params(("arbitrary", "arbitrary")),
        name="even_mixer",
    )(h, g_pre, w_in, w_spatial, bias_full, conv_w, w_out, g_post, *cast)
    return outs[0], outs[1:]


def _attn_band_bias():
    blk = ATTN_BLOCK
    tables = []
    for _, d in DILATED_BRANCHES:
        pieces = ATTN_PERM // d
        rows = blk // pieces
        i = np.arange(blk)
        run = i // rows
        if pieces == ATTN_PERM:
            run = ATTN_PERM_STEP * (run % ATTN_PERM_STEP) + run // ATTN_PERM_STEP
        uq = pieces * (i % rows) + run
        uk = np.concatenate([uq, blk + uq])
        step = uq[:, None] + blk - uk[None, :]
        valid = (step >= 0) & (step <= blk)
        first = valid & (uk[None, :] >= blk)
        tables.append(np.stack([np.where(valid, 0.0, -np.inf), np.where(first, 0.0, -np.inf)]))
    return np.stack(tables).astype(np.float32)


def _attn_block(qb, k_prev, k_cur, v_prev, v_cur, bias, old):
    kw = jnp.concatenate([k_prev, k_cur], axis=0)
    vw = jnp.concatenate([v_prev, v_cur], axis=0)
    s = lax.dot_general(qb, kw, (((1,), (1,)), ((), ())), preferred_element_type=F32) + bias
    m_blk = jnp.max(s, axis=-1, keepdims=True)
    m_new = jnp.broadcast_to(m_blk, (s.shape[0], HEAD_DIM))
    if old is not None:
        m_new = jnp.maximum(old[0], m_new)
    p = jnp.exp2(s - jnp.concatenate([m_new, m_new], axis=1)).astype(BF16)
    v_ones = jnp.concatenate([vw, jnp.ones_like(vw)], axis=1)
    pv = jnp.dot(p, v_ones, preferred_element_type=F32)
    a_new, l_new = pv[:, :HEAD_DIM], pv[:, HEAD_DIM:]
    if old is not None:
        corr = jnp.exp2(old[0] - m_new)
        l_new = old[1] * corr + l_new
        a_new = old[2] * corr + a_new
    return m_new, l_new, a_new


def _attn_kernel(q_ref, k_ref, v_ref, bias_ref, o_ref, qp_ref, kp_ref, vp_ref,
                 qh_ref, kh_ref, vh_ref, acc_ref, m_ref, l_ref, tmp_ref, *, seq, dilations):
    blk = ATTN_BLOCK
    lp = seq // ATTN_PERM
    step = ATTN_PERM_STEP
    lq = seq // step
    q_scale = HEAD_DIM ** -0.5 * math.log2(math.e)
    for src_ref, dst_ref, half_ref, scale in ((q_ref, qp_ref, qh_ref, q_scale),
                                              (k_ref, kp_ref, kh_ref, None),
                                              (v_ref, vp_ref, vh_ref, None)):
        for lo in range(step):
            tmp_ref[lo * lq:(lo + 1) * lq, :] = src_ref[pl.ds(lo, lq, stride=step), :]
        for lo in range(step):
            for hi in range(step):
                run = lo * step + hi
                rows = tmp_ref[pl.ds(lo * lq + hi, lp, stride=step), :]
                if scale is not None:
                    rows = rows * scale
                dst_ref[run * lp:(run + 1) * lp, :] = rows
                half_ref[run * lp:(run + 1) * lp, :] = rows.astype(BF16)

    for bi, d in enumerate(dilations):
        pieces = ATTN_PERM // d
        rows = blk // pieces
        nb = lp // rows
        gn = min(nb, ATTN_GROUP)
        gs = min(ATTN_GROUP // gn, d)
        n_groups = nb // gn
        assert nb % gn == 0 and d % gs == 0
        packed = rows % BF16_SUBLANES == 0
        q_src, k_src, v_src = (qh_ref, kh_ref, vh_ref) if packed else (qp_ref, kp_ref, vp_ref)

        def starts(r_sub, n, d=d, pieces=pieces, rows=rows):
            return [pl.multiple_of((r_sub * pieces + a) * lp + n * rows, rows) for a in range(pieces)]

        def gather(ref, st, rows=rows):
            parts = [ref[pl.ds(s0, rows), :] for s0 in st]
            return parts[0] if len(parts) == 1 else jnp.concatenate(parts, axis=0)

        def gather_half(ref, st):
            return gather(ref, st).astype(BF16)

        def scatter(ref, st, val, rows=rows):
            for a, s0 in enumerate(st):
                ref[pl.ds(s0, rows), :] = val[a * rows:(a + 1) * rows]

        def body(it, carry, bi=bi, gn=gn, gs=gs, n_groups=n_groups,
                 q_src=q_src, k_src=k_src, v_src=v_src):
            sg = it // n_groups
            n0 = (it % n_groups) * gn
            work = []
            for si in range(gs):
                r_sub = sg * gs + si
                st_prev = starts(r_sub, jnp.maximum(n0 - 1, 0))
                kb = [gather_half(k_src, st_prev)]
                vb = [gather_half(v_src, st_prev)]
                for j in range(gn):
                    st = starts(r_sub, n0 + j)
                    kb.append(gather_half(k_src, st))
                    vb.append(gather_half(v_src, st))
                    if j > 0:
                        bias = bias_ref[bi, 0]
                    elif n_groups == 1:
                        bias = bias_ref[bi, 1]
                    else:
                        bias = bias_ref[bi, jnp.where(n0 == 0, 1, 0)]
                    old = None if bi == 0 else (gather(m_ref, st), gather(l_ref, st),
                                                gather(acc_ref, st))
                    work.append((st, gather_half(q_src, st), kb[j], kb[j + 1],
                                 vb[j], vb[j + 1], bias, old))
            results = [(w[0],) + _attn_block(*w[1:]) for w in work]
            for st, m_new, l_new, a_new in results:
                scatter(m_ref, st, m_new)
                scatter(l_ref, st, l_new)
                scatter(acc_ref, st, a_new)
            return carry

        lax.fori_loop(0, (d // gs) * n_groups, body, 0)

    for lo in range(step):
        for hi in range(step):
            src = slice((lo * step + hi) * lp, (lo * step + hi + 1) * lp)
            tmp_ref[pl.ds(lo * lq + hi, lp, stride=step), :] = acc_ref[src, :] / l_ref[src, :]
    for lo in range(step):
        o_ref[pl.ds(lo, lq, stride=step), :] = tmp_ref[lo * lq:(lo + 1) * lq, :]


def _attention(qkv, batch, seq):
    dilations = tuple(d for _, d in DILATED_BRANCHES)
    for window, d in DILATED_BRANCHES:
        assert window // d == ATTN_BLOCK and seq % (d * ATTN_BLOCK) == 0
        assert ATTN_PERM % d == 0 and ATTN_BLOCK % (ATTN_PERM // d) == 0
    qkv3 = qkv.reshape(batch, seq, 3 * D_MODEL)
    bias = jnp.asarray(_attn_band_bias())
    spec = lambda part: pl.BlockSpec((None, seq, HEAD_DIM),
                                     lambda b, h, part=part: (b, 0, part * N_HEADS + h))
    kern = functools.partial(_attn_kernel, seq=seq, dilations=dilations)
    out = pl.pallas_call(
        kern,
        grid=(batch, N_HEADS),
        in_specs=[spec(0), spec(1), spec(2),
                  pl.BlockSpec(bias.shape, lambda b, h: (0, 0, 0, 0))],
        out_specs=pl.BlockSpec((None, seq, HEAD_DIM), lambda b, h: (b, 0, h)),
        out_shape=jax.ShapeDtypeStruct((batch, seq, D_MODEL), F32),
        scratch_shapes=[pltpu.VMEM((seq, HEAD_DIM), F32)] * 3 + [pltpu.VMEM((seq, HEAD_DIM), BF16)] * 3
                       + [pltpu.VMEM((seq, HEAD_DIM), F32)] * 4,
        compiler_params=_compiler_params(("parallel", "parallel")),
        name="attention",
    )(qkv3, qkv3, qkv3, bias)
    return out.reshape(batch * seq, D_MODEL)


def _matmul_rms_res_kernel(a_ref, w_ref, g_ref, h_ref, o_ref):
    f = jnp.dot(a_ref[...].astype(BF16), w_ref[...], preferred_element_type=F32)
    o_ref[...] = h_ref[...] + _rmsnorm(f, g_ref[...])


def _matmul_rms_res(a, w, g, h):
    m, k = a.shape
    n = w.shape[1]
    tm = RES_TM
    assert m % tm == 0
    return pl.pallas_call(
        _matmul_rms_res_kernel,
        grid=(m // tm,),
        in_specs=[pl.BlockSpec((tm, k), lambda i: (i, 0)),
                  pl.BlockSpec((k, n), lambda i: (0, 0)),
                  pl.BlockSpec((1, n), lambda i: (0, 0)),
                  pl.BlockSpec((tm, n), lambda i: (i, 0))],
        out_specs=pl.BlockSpec((tm, n), lambda i: (i, 0)),
        out_shape=jax.ShapeDtypeStruct((m, n), F32),
        compiler_params=_compiler_params(("parallel",)),
        name="matmul_rms_res",
    )(a, w, g, h)


def _mlp_kernel(x_ref, g_pre_ref, w_up_ref, w_down_ref, g_post_ref, *rest, n_cast):
    cast_in = rest[:n_cast]
    o_ref = rest[n_cast]
    cast_out = rest[n_cast + 1:2 * n_cast + 1]
    hn_ref = rest[2 * n_cast + 1]
    k = pl.program_id(1)

    def ffn_step(first, last):
        sub = MLP_SUB
        tm = o_ref.shape[0]
        _cast_chunks(cast_in, cast_out)
        group = tm // MLP_LAST_SPLIT if last else tm
        for r0 in range(0, tm, group):
            rows = pl.ds(r0, group)
            if first:
                hn = _rmsnorm(x_ref[rows, :], g_pre_ref[...]).astype(BF16)
                hn_ref[rows, :] = hn
            else:
                hn = hn_ref[rows, :]
            for c in range(w_up_ref.shape[1] // sub):
                ffn = pl.ds(c * sub, sub)
                u = jnp.dot(hn, w_up_ref[:, ffn], preferred_element_type=F32)
                u = jnp.square(jnp.maximum(u, 0.0)).astype(BF16)
                for n in range(o_ref.shape[1] // sub):
                    cols = pl.ds(n * sub, sub)
                    part = jnp.dot(u, w_down_ref[ffn, cols], preferred_element_type=F32)
                    if first and c == 0:
                        o_ref[rows, cols] = part
                    else:
                        o_ref[rows, cols] += part
            if last:
                for r1 in range(r0, r0 + group, NORM_ROW_CHUNK):
                    chunk = pl.ds(r1, NORM_ROW_CHUNK)
                    o_ref[chunk, :] = x_ref[chunk, :] + _rmsnorm(o_ref[chunk, :], g_post_ref[...])

    n_steps = pl.num_programs(1)

    @pl.when(k == 0)
    def _():
        ffn_step(first=True, last=False)

    @pl.when((k > 0) & (k < n_steps - 1))
    def _():
        ffn_step(first=False, last=False)

    @pl.when(k == n_steps - 1)
    def _():
        ffn_step(first=False, last=True)


def _mlp(h, g_pre, w_up, w_down, g_post, cast=(), cast_li=0):
    m, d = h.shape
    f = w_up.shape[1]
    tm, tf = MLP_TM, MLP_TF
    assert m % tm == 0 and f % tf == 0
    cast_in_specs, cast_out_specs, cast_out_shapes = _weight_cast_plan(
        cast, cast_li, m // tm, f // tf, MLP_CAST_STEPS)
    outs = pl.pallas_call(
        functools.partial(_mlp_kernel, n_cast=len(cast)),
        grid=(m // tm, f // tf),
        in_specs=[pl.BlockSpec((tm, d), lambda i, k: (i, 0)),
                  pl.BlockSpec((1, d), lambda i, k: (0, 0)),
                  pl.BlockSpec((d, tf), lambda i, k: (0, k)),
                  pl.BlockSpec((tf, d), lambda i, k: (k, 0)),
                  pl.BlockSpec((1, d), lambda i, k: (0, 0)),
                  *cast_in_specs],
        out_specs=[pl.BlockSpec((tm, d), lambda i, k: (i, 0)), *cast_out_specs],
        out_shape=[jax.ShapeDtypeStruct((m, d), F32), *cast_out_shapes],
        scratch_shapes=[pltpu.VMEM((tm, d), BF16)],
        compiler_params=_compiler_params(("arbitrary", "arbitrary")),
        name="mlp",
    )(h, g_pre, w_up, w_down, g_post, *cast)
    return outs[0], outs[1:]


def _rope_tables(seq):
    half = HEAD_DIM // 2
    inv_freq = np.float32(ROPE_THETA) ** (-np.arange(half, dtype=np.float32) * np.float32(2.0)
                                          / np.float32(HEAD_DIM))
    ang = np.arange(seq, dtype=np.float32)[:, None] * inv_freq[None, :].astype(np.float32)
    cos = np.cos(ang).astype(np.float32)
    sin = np.sin(ang).astype(np.float32)
    return (jnp.asarray(np.concatenate([cos, cos], axis=-1)),
            jnp.asarray(np.concatenate([-sin, sin], axis=-1)))


def kernel(x, norm_mix_pre, norm_mix_post, norm_mlp_pre, norm_mlp_post, w_in_ab, w_spatial,
           b_spatial, conv_w, w_out_ab, w_qkv, w_o, w_up, w_down):
    batch, seq, d = x.shape
    assert d == D_MODEL
    depth = norm_mix_pre.shape[0]
    m = batch * seq
    h = x.reshape(m, d)
    cos_tab, sin_tab = _rope_tables(seq)

    def mixer_weights(layer):
        return (w_in_ab, w_out_ab) if layer % 2 == 0 else (w_qkv, w_o)

    mix_bf = tuple(w[0].astype(BF16) for w in mixer_weights(0))
    mlp_bf = None
    for layer in range(depth):
        g_pre = norm_mix_pre[layer][None, :]
        g_post = norm_mix_post[layer][None, :]
        idx = layer // 2
        cast = mixer_weights(layer + 1) if layer + 1 < depth else ()
        n_next = len(cast)
        if layer == 0:
            cast = cast + (w_up, w_down)
        cast_li = (layer + 1) // 2
        if layer % 2 == 0:
            bias_full = jnp.repeat(b_spatial[idx].T, A_GROUP_DIM, axis=1)
            h, cast_bf = _even_mixer(h, g_pre, mix_bf[0], w_spatial[idx], bias_full, conv_w[idx],
                                     mix_bf[1], g_post, seq, cast=cast, cast_li=cast_li)
        else:
            qkv, cast_bf = _qkv_proj(h, g_pre, mix_bf[0], cos_tab, sin_tab, seq,
                                     cast=cast, cast_li=cast_li)
            att = _attention(qkv, batch, seq)
            h = _matmul_rms_res(att, mix_bf[1], g_post, h)
        mix_bf = cast_bf[:n_next]
        if layer == 0:
            mlp_bf = cast_bf[n_next:]
        next_cast = (w_up, w_down) if layer + 1 < depth else ()
        h, next_bf = _mlp(h, norm_mlp_pre[layer][None, :], mlp_bf[0], mlp_bf[1],
                          norm_mlp_post[layer][None, :], cast=next_cast, cast_li=layer + 1)
        if next_bf:
            mlp_bf = next_bf
    return h.reshape(batch, seq, d)
```

```python
import functools
import math

import jax
import jax.numpy as jnp
import numpy as np
from jax import lax
from jax.experimental import pallas as pl
from jax.experimental.pallas import tpu as pltpu

F32 = jnp.float32
BF16 = jnp.bfloat16

D_MODEL = 2048
A_WIDTH = D_MODEL // 2
B_WIDTH = D_MODEL - A_WIDTH
A_GROUPS = 8
A_GROUP_DIM = A_WIDTH // A_GROUPS
CHUNK = 128
CONV_WIDTH = 3
HEAD_DIM = 128
N_HEADS = D_MODEL // HEAD_DIM
DILATED_BRANCHES = ((128, 1), (512, 4), (2048, 16))
ATTN_BLOCK = 128
ATTN_PERM = 16
ATTN_PERM_STEP = 4
ATTN_GROUP = 32
ROPE_THETA = 10000.0
RMS_EPS = 1e-6
LN_EPS = 1e-5

BF16_SUBLANES = 16
V7X_VMEM_LIMIT_BYTES = 62 * 1024 * 1024

QKV_PROJ_TM = 256
RMS_MATMUL_SUB_TN = 512
EVEN_MIXER_TM = 256
RES_TM = 512
MLP_TM = 1024
MLP_TF = 1024
MLP_SUB = 512
CONV_HALO = 8
MLP_LAST_SPLIT = 2
NORM_ROW_CHUNK = 16
WEIGHT_CAST_STEPS = 64
MLP_CAST_STEPS = 128


def _compiler_params(semantics):
    return pltpu.CompilerParams(dimension_semantics=semantics,
                                vmem_limit_bytes=V7X_VMEM_LIMIT_BYTES)


def _rmsnorm(x, g):
    ms = jnp.mean(x * x, axis=-1, keepdims=True)
    return (x * lax.rsqrt(ms + RMS_EPS)) * g


def _weight_cast_plan(cast, cast_li, n_outer, n_inner, max_steps):
    steps = min(max_steps, 2 ** int(math.log2(n_outer * n_inner)))
    chunk = lambda i, j: jnp.minimum(i * n_inner + j, steps - 1)
    in_specs, out_specs, out_shapes = [], [], []
    for c, tiles in cast:
        _, rows, width = c.shape
        assert rows % (steps * BF16_SUBLANES) == 0 and width % tiles == 0
        rows_per_step = rows // steps
        in_specs.append(pl.BlockSpec((None, rows_per_step, width),
                                     lambda i, j: (cast_li, chunk(i, j), 0)))
        if tiles == 1:
            out_specs.append(pl.BlockSpec((rows_per_step, width), lambda i, j: (chunk(i, j), 0)))
            out_shapes.append(jax.ShapeDtypeStruct((rows, width), BF16))
        else:
            out_specs.append(pl.BlockSpec((tiles, rows_per_step, width // tiles),
                                          lambda i, j: (0, chunk(i, j), 0)))
            out_shapes.append(jax.ShapeDtypeStruct((tiles, rows, width // tiles), BF16))
    return in_specs, out_specs, out_shapes


def _cast_chunks(cast_in, cast_out):
    for src, dst in zip(cast_in, cast_out):
        if len(dst.shape) == 2:
            dst[...] = src[...].astype(BF16)
        else:
            width = dst.shape[2]
            for t in range(dst.shape[0]):
                dst[t] = src[:, t * width:(t + 1) * width].astype(BF16)


def _qkv_proj_kernel(x_ref, g_ref, w_ref, cos_ref, sin_ref, *rest, n_cast):
    cast_in = rest[:n_cast]
    o_ref = rest[n_cast]
    cast_out = rest[n_cast + 1:]
    _cast_chunks(cast_in, cast_out)
    hn = _rmsnorm(x_ref[...], g_ref[...]).astype(BF16)
    cos = cos_ref[...]
    sin = sin_ref[...]
    sub = RMS_MATMUL_SUB_TN
    for c in range(w_ref.shape[1] // sub):
        p = jnp.dot(hn, w_ref[:, pl.ds(c * sub, sub)], preferred_element_type=F32)
        if (c + 1) * sub <= 2 * D_MODEL:
            for hh in range(sub // HEAD_DIM):
                t = p[:, hh * HEAD_DIM:(hh + 1) * HEAD_DIM]
                o_ref[:, pl.ds(c * sub + hh * HEAD_DIM, HEAD_DIM)] = (
                    t * cos + pltpu.roll(t, HEAD_DIM // 2, 1) * sin)
        else:
            o_ref[:, pl.ds(c * sub, sub)] = p


def _qkv_proj(x, g, w, cos_tab, sin_tab, seq, cast=(), cast_li=0):
    m, k = x.shape
    n = w.shape[1]
    tm = QKV_PROJ_TM
    assert m % tm == 0 and seq % tm == 0 and (2 * D_MODEL) % RMS_MATMUL_SUB_TN == 0
    tiles_per_seq = seq // tm
    cast_in_specs, cast_out_specs, cast_out_shapes = _weight_cast_plan(
        cast, cast_li, m // tm, 1, WEIGHT_CAST_STEPS)
    table = pl.BlockSpec((tm, HEAD_DIM), lambda i, j: (i % tiles_per_seq, 0))
    outs = pl.pallas_call(
        functools.partial(_qkv_proj_kernel, n_cast=len(cast)),
        grid=(m // tm, 1),
        in_specs=[pl.BlockSpec((tm, k), lambda i, j: (i, 0)),
                  pl.BlockSpec((1, k), lambda i, j: (0, 0)),
                  pl.BlockSpec((k, n), lambda i, j: (0, 0), pipeline_mode=pl.Buffered(1)),
                  table, table, *cast_in_specs],
        out_specs=[pl.BlockSpec((tm, n), lambda i, j: (i, 0)), *cast_out_specs],
        out_shape=[jax.ShapeDtypeStruct((m, n), F32), *cast_out_shapes],
        compiler_params=_compiler_params(("arbitrary", "arbitrary")),
        name="qkv_proj",
    )(x, g, w, cos_tab, sin_tab, *[c for c, _ in cast])
    return outs[0], outs[1:]


def _even_mixer_kernel(x_ref, g_pre_ref, w_in_ref, ws_ref, bias_ref, cw_ref, w_out_ref,
                       g_post_ref, *rest, tiles_per_seq, n_cast):
    cast_in = rest[:n_cast]
    o_ref = rest[n_cast]
    cast_out = rest[n_cast + 1:2 * n_cast + 1]
    proj_ref, z_ref = rest[2 * n_cast + 1:]
    tm = x_ref.shape[0]
    i = pl.program_id(0)

    @pl.when(i == 0)
    def _():
        z_ref[...] = jnp.zeros_like(z_ref)

    _cast_chunks(cast_in, cast_out)

    hn = _rmsnorm(x_ref[...], g_pre_ref[...]).astype(BF16)
    sub = RMS_MATMUL_SUB_TN
    for c in range(w_in_ref.shape[1] // sub):
        cols = pl.ds(c * sub, sub)
        p = jnp.dot(hn, w_in_ref[:, cols], preferred_element_type=F32)
        proj_ref[:, cols] = jax.nn.gelu(p) if (c + 1) * sub <= 2 * A_WIDTH else p
    au_ref, av_ref, gb_ref, gc_ref, bx_ref = (
        proj_ref.at[:, pl.ds(s * A_WIDTH, A_WIDTH)] for s in range(5))

    av = av_ref[...]
    mu = jnp.mean(av, axis=-1, keepdims=True)
    cen = av - mu
    var = jnp.mean(cen * cen, axis=-1, keepdims=True)
    vn = (cen * lax.rsqrt(var + LN_EPS)).astype(BF16)
    row = lax.broadcasted_iota(jnp.int32, (CHUNK, CHUNK), 0)
    col = lax.broadcasted_iota(jnp.int32, (CHUNK, CHUNK), 1)
    causal = col <= row
    a_cols = []
    for g in range(A_GROUPS):
        cs = slice(g * A_GROUP_DIM, (g + 1) * A_GROUP_DIM)
        w_causal = jnp.where(causal, ws_ref[g], 0.0).astype(BF16)
        chunks = []
        for c in range(tm // CHUNK):
            rs = slice(c * CHUNK, (c + 1) * CHUNK)
            mixed = jnp.dot(w_causal, vn[rs, cs], preferred_element_type=F32) + bias_ref[:, cs]
            chunks.append((au_ref[rs, cs] * mixed).astype(BF16))
        a_cols.append(jnp.concatenate(chunks, axis=0))

    z = gc_ref[...] * bx_ref[...]
    at_seq_start = (i % tiles_per_seq) == 0
    z_before = jnp.where(at_seq_start, 0.0, z_ref[tm:tm + CONV_HALO, :])
    z_ref[0:CONV_HALO, :] = z_before
    z_ref[CONV_HALO:CONV_HALO + tm, :] = z
    y = (cw_ref[2:3, :] * z
         + cw_ref[1:2, :] * z_ref[CONV_HALO - 1:CONV_HALO - 1 + tm, :]
         + cw_ref[0:1, :] * z_ref[CONV_HALO - 2:CONV_HALO - 2 + tm, :])
    b_out = (gb_ref[...] * y).astype(BF16)

    mixed_ab = jnp.concatenate(a_cols + [b_out], axis=1)
    f = jnp.dot(mixed_ab, w_out_ref[...], preferred_element_type=F32)
    o_ref[...] = x_ref[...] + _rmsnorm(f, g_post_ref[...])


def _even_mixer(h, g_pre, w_in, w_spatial, bias_full, conv_w, w_out, g_post, seq,
                cast=(), cast_li=0):
    m, d = h.shape
    n_in = w_in.shape[1]
    tm = EVEN_MIXER_TM
    assert seq % tm == 0 and tm % CHUNK == 0 and A_WIDTH == B_WIDTH
    assert n_in == 2 * A_WIDTH + 3 * B_WIDTH and (2 * A_WIDTH) % RMS_MATMUL_SUB_TN == 0
    cast_in_specs, cast_out_specs, cast_out_shapes = _weight_cast_plan(
        cast, cast_li, m // tm, 1, WEIGHT_CAST_STEPS)
    const = lambda shape: pl.BlockSpec(shape, lambda i, j: (0,) * len(shape))
    resident = lambda rows, cols: pl.BlockSpec((rows, cols), lambda i, j: (0, 0),
                                               pipeline_mode=pl.Buffered(1))
    kern = functools.partial(_even_mixer_kernel, tiles_per_seq=seq // tm, n_cast=len(cast))
    outs = pl.pallas_call(
        kern,
        grid=(m // tm, 1),
        in_specs=[pl.BlockSpec((tm, d), lambda i, j: (i, 0)),
                  const((1, d)),
                  resident(d, n_in),
                  const((A_GROUPS, CHUNK, CHUNK)),
                  const((CHUNK, A_WIDTH)),
                  const((CONV_WIDTH, B_WIDTH)),
                  resident(d, d),
                  const((1, d)),
                  *cast_in_specs],
        out_specs=[pl.BlockSpec((tm, d), lambda i, j: (i, 0)), *cast_out_specs],
        out_shape=[jax.ShapeDtypeStruct((m, d), F32), *cast_out_shapes],
        scratch_shapes=[pltpu.VMEM((tm, n_in), F32),
                        pltpu.VMEM((CONV_HALO + tm, B_WIDTH), F32)],
        compiler_params=_compiler_params(("arbitrary", "arbitrary")),
        name="even_mixer",
    )(h, g_pre, w_in, w_spatial, bias_full, conv_w, w_out, g_post, *[c for c, _ in cast])
    return outs[0], outs[1:]


def _attn_band_bias():
    blk = ATTN_BLOCK
    tables = []
    for _, d in DILATED_BRANCHES:
        pieces = ATTN_PERM // d
        rows = blk // pieces
        i = np.arange(blk)
        run = i // rows
        if pieces == ATTN_PERM:
            run = ATTN_PERM_STEP * (run % ATTN_PERM_STEP) + run // ATTN_PERM_STEP
        uq = pieces * (i % rows) + run
        uk = np.concatenate([uq, blk + uq])
        step = uq[:, None] + blk - uk[None, :]
        valid = (step >= 0) & (step <= blk)
        first = valid & (uk[None, :] >= blk)
        tables.append(np.stack([np.where(valid, 0.0, -np.inf), np.where(first, 0.0, -np.inf)]))
    return np.stack(tables).astype(np.float32)


def _attn_block(qb, k_prev, k_cur, v_prev, v_cur, bias, old):
    kw = jnp.concatenate([k_prev, k_cur], axis=0)
    vw = jnp.concatenate([v_prev, v_cur], axis=0)
    s = lax.dot_general(qb, kw, (((1,), (1,)), ((), ())), preferred_element_type=F32) + bias
    m_blk = jnp.max(s, axis=-1, keepdims=True)
    m_new = jnp.broadcast_to(m_blk, (s.shape[0], HEAD_DIM))
    if old is not None:
        m_new = jnp.maximum(old[0], m_new)
    p = jnp.exp2(s - jnp.concatenate([m_new, m_new], axis=1)).astype(BF16)
    v_ones = jnp.concatenate([vw, jnp.ones_like(vw)], axis=1)
    pv = jnp.dot(p, v_ones, preferred_element_type=F32)
    a_new, l_new = pv[:, :HEAD_DIM], pv[:, HEAD_DIM:]
    if old is not None:
        corr = jnp.exp2(old[0] - m_new)
        l_new = old[1] * corr + l_new
        a_new = old[2] * corr + a_new
    return m_new, l_new, a_new


def _attn_kernel(q_ref, k_ref, v_ref, bias_ref, o_ref, qp_ref, kp_ref, vp_ref,
                 qh_ref, kh_ref, vh_ref, acc_ref, m_ref, l_ref, tmp_ref, *, seq, dilations):
    blk = ATTN_BLOCK
    lp = seq // ATTN_PERM
    step = ATTN_PERM_STEP
    lq = seq // step
    q_scale = HEAD_DIM ** -0.5 * math.log2(math.e)
    for src_ref, dst_ref, half_ref, scale in ((q_ref, qp_ref, qh_ref, q_scale),
                                              (k_ref, kp_ref, kh_ref, None),
                                              (v_ref, vp_ref, vh_ref, None)):
        for lo in range(step):
            tmp_ref[lo * lq:(lo + 1) * lq, :] = src_ref[pl.ds(lo, lq, stride=step), :]
        for lo in range(step):
            for hi in range(step):
                run = lo * step + hi
                rows = tmp_ref[pl.ds(lo * lq + hi, lp, stride=step), :]
                if scale is not None:
                    rows = rows * scale
                dst_ref[run * lp:(run + 1) * lp, :] = rows
                half_ref[run * lp:(run + 1) * lp, :] = rows.astype(BF16)

    for bi, d in enumerate(dilations):
        pieces = ATTN_PERM // d
        rows = blk // pieces
        nb = lp // rows
        gn = min(nb, ATTN_GROUP)
        gs = min(ATTN_GROUP // gn, d)
        n_groups = nb // gn
        assert nb % gn == 0 and d % gs == 0
        packed = rows % BF16_SUBLANES == 0
        q_src, k_src, v_src = (qh_ref, kh_ref, vh_ref) if packed else (qp_ref, kp_ref, vp_ref)

        def starts(r_sub, n, d=d, pieces=pieces, rows=rows):
            return [pl.multiple_of((r_sub * pieces + a) * lp + n * rows, rows) for a in range(pieces)]

        def gather(ref, st, rows=rows):
            parts = [ref[pl.ds(s0, rows), :] for s0 in st]
            return parts[0] if len(parts) == 1 else jnp.concatenate(parts, axis=0)

        def gather_half(ref, st):
            return gather(ref, st).astype(BF16)

        def scatter(ref, st, val, rows=rows):
            for a, s0 in enumerate(st):
                ref[pl.ds(s0, rows), :] = val[a * rows:(a + 1) * rows]

        def body(it, carry, bi=bi, gn=gn, gs=gs, n_groups=n_groups,
                 q_src=q_src, k_src=k_src, v_src=v_src):
            sg = it // n_groups
            n0 = (it % n_groups) * gn
            work = []
            for si in range(gs):
                r_sub = sg * gs + si
                st_prev = starts(r_sub, jnp.maximum(n0 - 1, 0))
                kb = [gather_half(k_src, st_prev)]
                vb = [gather_half(v_src, st_prev)]
                for j in range(gn):
                    st = starts(r_sub, n0 + j)
                    kb.append(gather_half(k_src, st))
                    vb.append(gather_half(v_src, st))
                    if j > 0:
                        bias = bias_ref[bi, 0]
                    elif n_groups == 1:
                        bias = bias_ref[bi, 1]
                    else:
                        bias = bias_ref[bi, jnp.where(n0 == 0, 1, 0)]
                    old = None if bi == 0 else (gather(m_ref, st), gather(l_ref, st),
                                                gather(acc_ref, st))
                    work.append((st, gather_half(q_src, st), kb[j], kb[j + 1],
                                 vb[j], vb[j + 1], bias, old))
            results = [(w[0],) + _attn_block(*w[1:]) for w in work]
            for st, m_new, l_new, a_new in results:
                scatter(m_ref, st, m_new)
                scatter(l_ref, st, l_new)
                scatter(acc_ref, st, a_new)
            return carry

        lax.fori_loop(0, (d // gs) * n_groups, body, 0)

    for lo in range(step):
        for hi in range(step):
            src = slice((lo * step + hi) * lp, (lo * step + hi + 1) * lp)
            tmp_ref[pl.ds(lo * lq + hi, lp, stride=step), :] = acc_ref[src, :] / l_ref[src, :]
    for lo in range(step):
        o_ref[pl.ds(lo, lq, stride=step), :] = tmp_ref[lo * lq:(lo + 1) * lq, :]


def _attention(qkv, batch, seq):
    dilations = tuple(d for _, d in DILATED_BRANCHES)
    for window, d in DILATED_BRANCHES:
        assert window // d == ATTN_BLOCK and seq % (d * ATTN_BLOCK) == 0
        assert ATTN_PERM % d == 0 and ATTN_BLOCK % (ATTN_PERM // d) == 0
    qkv3 = qkv.reshape(batch, seq, 3 * D_MODEL)
    bias = jnp.asarray(_attn_band_bias())
    spec = lambda part: pl.BlockSpec((None, seq, HEAD_DIM),
                                     lambda b, h, part=part: (b, 0, part * N_HEADS + h))
    kern = functools.partial(_attn_kernel, seq=seq, dilations=dilations)
    out = pl.pallas_call(
        kern,
        grid=(batch, N_HEADS),
        in_specs=[spec(0), spec(1), spec(2),
                  pl.BlockSpec(bias.shape, lambda b, h: (0, 0, 0, 0))],
        out_specs=pl.BlockSpec((None, seq, HEAD_DIM), lambda b, h: (b, 0, h)),
        out_shape=jax.ShapeDtypeStruct((batch, seq, D_MODEL), F32),
        scratch_shapes=[pltpu.VMEM((seq, HEAD_DIM), F32)] * 3 + [pltpu.VMEM((seq, HEAD_DIM), BF16)] * 3
                       + [pltpu.VMEM((seq, HEAD_DIM), F32)] * 4,
        compiler_params=_compiler_params(("parallel", "parallel")),
        name="attention",
    )(qkv3, qkv3, qkv3, bias)
    return out.reshape(batch * seq, D_MODEL)


def _matmul_rms_res_kernel(a_ref, w_ref, g_ref, h_ref, o_ref):
    f = jnp.dot(a_ref[...].astype(BF16), w_ref[...], preferred_element_type=F32)
    o_ref[...] = h_ref[...] + _rmsnorm(f, g_ref[...])


def _matmul_rms_res(a, w, g, h):
    m, k = a.shape
    n = w.shape[1]
    tm = RES_TM
    assert m % tm == 0
    return pl.pallas_call(
        _matmul_rms_res_kernel,
        grid=(m // tm,),
        in_specs=[pl.BlockSpec((tm, k), lambda i: (i, 0)),
                  pl.BlockSpec((k, n), lambda i: (0, 0)),
                  pl.BlockSpec((1, n), lambda i: (0, 0)),
                  pl.BlockSpec((tm, n), lambda i: (i, 0))],
        out_specs=pl.BlockSpec((tm, n), lambda i: (i, 0)),
        out_shape=jax.ShapeDtypeStruct((m, n), F32),
        compiler_params=_compiler_params(("parallel",)),
        name="matmul_rms_res",
    )(a, w, g, h)


def _mlp_kernel(x_ref, g_pre_ref, w_up_ref, w_down_ref, g_post_ref, *rest, n_cast):
    cast_in = rest[:n_cast]
    o_ref = rest[n_cast]
    cast_out = rest[n_cast + 1:2 * n_cast + 1]
    hn_ref = rest[2 * n_cast + 1]
    k = pl.program_id(1)

    def ffn_step(first, last):
        sub = MLP_SUB
        tm = o_ref.shape[0]
        _cast_chunks(cast_in, cast_out)
        group = tm // MLP_LAST_SPLIT if last else tm
        for r0 in range(0, tm, group):
            rows = pl.ds(r0, group)
            if first:
                hn = _rmsnorm(x_ref[rows, :], g_pre_ref[...]).astype(BF16)
                hn_ref[rows, :] = hn
            else:
                hn = hn_ref[rows, :]
            for c in range(w_up_ref.shape[1] // sub):
                ffn = pl.ds(c * sub, sub)
                u = jnp.dot(hn, w_up_ref[:, ffn], preferred_element_type=F32)
                u = jnp.square(jnp.maximum(u, 0.0)).astype(BF16)
                for n in range(o_ref.shape[1] // sub):
                    cols = pl.ds(n * sub, sub)
                    part = jnp.dot(u, w_down_ref[ffn, cols], preferred_element_type=F32)
                    if first and c == 0:
                        o_ref[rows, cols] = part
                    else:
                        o_ref[rows, cols] += part
            if last:
                for r1 in range(r0, r0 + group, NORM_ROW_CHUNK):
                    chunk = pl.ds(r1, NORM_ROW_CHUNK)
                    o_ref[chunk, :] = x_ref[chunk, :] + _rmsnorm(o_ref[chunk, :], g_post_ref[...])

    n_steps = pl.num_programs(1)

    @pl.when(k == 0)
    def _():
        ffn_step(first=True, last=False)

    @pl.when((k > 0) & (k < n_steps - 1))
    def _():
        ffn_step(first=False, last=False)

    @pl.when(k == n_steps - 1)
    def _():
        ffn_step(first=False, last=True)


def _mlp(h, g_pre, w_up, w_down, g_post, cast=(), cast_li=0):
    m, d = h.shape
    f = w_down.shape[0]
    tm, tf = MLP_TM, MLP_TF
    assert m % tm == 0 and f % tf == 0 and w_up.shape == (f // tf, d, tf)
    cast_in_specs, cast_out_specs, cast_out_shapes = _weight_cast_plan(
        cast, cast_li, m // tm, f // tf, MLP_CAST_STEPS)
    outs = pl.pallas_call(
        functools.partial(_mlp_kernel, n_cast=len(cast)),
        grid=(m // tm, f // tf),
        in_specs=[pl.BlockSpec((tm, d), lambda i, k: (i, 0)),
                  pl.BlockSpec((1, d), lambda i, k: (0, 0)),
                  pl.BlockSpec((None, d, tf), lambda i, k: (k, 0, 0)),
                  pl.BlockSpec((tf, d), lambda i, k: (k, 0)),
                  pl.BlockSpec((1, d), lambda i, k: (0, 0)),
                  *cast_in_specs],
        out_specs=[pl.BlockSpec((tm, d), lambda i, k: (i, 0)), *cast_out_specs],
        out_shape=[jax.ShapeDtypeStruct((m, d), F32), *cast_out_shapes],
        scratch_shapes=[pltpu.VMEM((tm, d), BF16)],
        compiler_params=_compiler_params(("arbitrary", "arbitrary")),
        name="mlp",
    )(h, g_pre, w_up, w_down, g_post, *[c for c, _ in cast])
    return outs[0], outs[1:]


def _rope_tables(seq):
    half = HEAD_DIM // 2
    inv_freq = np.float32(ROPE_THETA) ** (-np.arange(half, dtype=np.float32) * np.float32(2.0)
                                          / np.float32(HEAD_DIM))
    ang = np.arange(seq, dtype=np.float32)[:, None] * inv_freq[None, :].astype(np.float32)
    cos = np.cos(ang).astype(np.float32)
    sin = np.sin(ang).astype(np.float32)
    return (jnp.asarray(np.concatenate([cos, cos], axis=-1)),
            jnp.asarray(np.concatenate([-sin, sin], axis=-1)))


def kernel(x, norm_mix_pre, norm_mix_post, norm_mlp_pre, norm_mlp_post, w_in_ab, w_spatial,
           b_spatial, conv_w, w_out_ab, w_qkv, w_o, w_up, w_down):
    batch, seq, d = x.shape
    assert d == D_MODEL
    depth = norm_mix_pre.shape[0]
    m = batch * seq
    h = x.reshape(m, d)
    cos_tab, sin_tab = _rope_tables(seq)

    def mixer_weights(layer):
        return (w_in_ab, w_out_ab) if layer % 2 == 0 else (w_qkv, w_o)

    mlp_cast = ((w_up, w_up.shape[2] // MLP_TF), (w_down, 1))
    mix_bf = tuple(w[0].astype(BF16) for w in mixer_weights(0))
    mlp_bf = None
    for layer in range(depth):
        g_pre = norm_mix_pre[layer][None, :]
        g_post = norm_mix_post[layer][None, :]
        idx = layer // 2
        cast = tuple((w, 1) for w in mixer_weights(layer + 1)) if layer + 1 < depth else ()
        n_next = len(cast)
        if layer == 0:
            cast = cast + mlp_cast
        cast_li = (layer + 1) // 2
        if layer % 2 == 0:
            bias_full = jnp.repeat(b_spatial[idx].T, A_GROUP_DIM, axis=1)
            h, cast_bf = _even_mixer(h, g_pre, mix_bf[0], w_spatial[idx], bias_full, conv_w[idx],
                                     mix_bf[1], g_post, seq, cast=cast, cast_li=cast_li)
        else:
            qkv, cast_bf = _qkv_proj(h, g_pre, mix_bf[0], cos_tab, sin_tab, seq,
                                     cast=cast, cast_li=cast_li)
            att = _attention(qkv, batch, seq)
            h = _matmul_rms_res(att, mix_bf[1], g_post, h)
        mix_bf = cast_bf[:n_next]
        if layer == 0:
            mlp_bf = cast_bf[n_next:]
        next_cast = mlp_cast if layer + 1 < depth else ()
        h, next_bf = _mlp(h, norm_mlp_pre[layer][None, :], mlp_bf[0], mlp_bf[1],
                          norm_mlp_post[layer][None, :], cast=next_cast, cast_li=layer + 1)
        if next_bf:
            mlp_bf = next_bf
    return h.reshape(batch, seq, d)
```

```python
import functools
import math

import jax
import jax.numpy as jnp
import numpy as np
from jax import lax
from jax.experimental import pallas as pl
from jax.experimental.pallas import tpu as pltpu

F32 = jnp.float32
BF16 = jnp.bfloat16

D_MODEL = 2048
A_WIDTH = D_MODEL // 2
B_WIDTH = D_MODEL - A_WIDTH
A_GROUPS = 8
A_GROUP_DIM = A_WIDTH // A_GROUPS
CHUNK = 128
CONV_WIDTH = 3
HEAD_DIM = 128
N_HEADS = D_MODEL // HEAD_DIM
DILATED_BRANCHES = ((128, 1), (512, 4), (2048, 16))
ATTN_BLOCK = 128
ATTN_PERM = 16
ATTN_PERM_STEP = 4
ATTN_GROUP = 32
ROPE_THETA = 10000.0
RMS_EPS = 1e-6
LN_EPS = 1e-5

BF16_SUBLANES = 16
V7X_VMEM_LIMIT_BYTES = 62 * 1024 * 1024

QKV_PROJ_TM = 256
RMS_MATMUL_SUB_TN = 512
EVEN_MIXER_TM = 256
X_RING = 3
RES_TM = 512
MLP_TM = 1024
MLP_TF = 1024
MLP_SUB = 512
CONV_HALO = 8
MLP_LAST_SPLIT = 2
NORM_ROW_CHUNK = 16
WEIGHT_CAST_STEPS = 64
MLP_CAST_STEPS = 128


def _compiler_params(semantics):
    return pltpu.CompilerParams(dimension_semantics=semantics,
                                vmem_limit_bytes=V7X_VMEM_LIMIT_BYTES)


def _rmsnorm(x, g):
    ms = jnp.mean(x * x, axis=-1, keepdims=True)
    return (x * lax.rsqrt(ms + RMS_EPS)) * g


def _weight_cast_plan(cast, cast_li, n_outer, n_inner, max_steps):
    steps = min(max_steps, 2 ** int(math.log2(n_outer * n_inner)))
    chunk = lambda i, j: jnp.minimum(i * n_inner + j, steps - 1)
    in_specs, out_specs, out_shapes = [], [], []
    for c in cast:
        _, rows, width = c.shape
        assert rows % (steps * BF16_SUBLANES) == 0
        rows_per_step = rows // steps
        in_specs.append(pl.BlockSpec((None, rows_per_step, width),
                                     lambda i, j: (cast_li, chunk(i, j), 0)))
        out_specs.append(pl.BlockSpec((rows_per_step, width), lambda i, j: (chunk(i, j), 0)))
        out_shapes.append(jax.ShapeDtypeStruct((rows, width), BF16))
    return in_specs, out_specs, out_shapes


def _cast_chunks(cast_in, cast_out):
    for src, dst in zip(cast_in, cast_out):
        dst[...] = src[...].astype(BF16)


def _qkv_proj_kernel(x_ref, g_ref, w_ref, cos_ref, sin_ref, *rest, n_cast):
    cast_in = rest[:n_cast]
    o_ref = rest[n_cast]
    cast_out = rest[n_cast + 1:]
    _cast_chunks(cast_in, cast_out)
    hn = _rmsnorm(x_ref[...], g_ref[...]).astype(BF16)
    cos = cos_ref[...]
    sin = sin_ref[...]
    sub = RMS_MATMUL_SUB_TN
    for c in range(w_ref.shape[1] // sub):
        p = jnp.dot(hn, w_ref[:, pl.ds(c * sub, sub)], preferred_element_type=F32)
        if (c + 1) * sub <= 2 * D_MODEL:
            for hh in range(sub // HEAD_DIM):
                t = p[:, hh * HEAD_DIM:(hh + 1) * HEAD_DIM]
                o_ref[:, pl.ds(c * sub + hh * HEAD_DIM, HEAD_DIM)] = (
                    t * cos + pltpu.roll(t, HEAD_DIM // 2, 1) * sin)
        else:
            o_ref[:, pl.ds(c * sub, sub)] = p


def _qkv_proj(x, g, w, cos_tab, sin_tab, seq, cast=(), cast_li=0):
    m, k = x.shape
    n = w.shape[1]
    tm = QKV_PROJ_TM
    assert m % tm == 0 and seq % tm == 0 and (2 * D_MODEL) % RMS_MATMUL_SUB_TN == 0
    tiles_per_seq = seq // tm
    cast_in_specs, cast_out_specs, cast_out_shapes = _weight_cast_plan(
        cast, cast_li, m // tm, 1, WEIGHT_CAST_STEPS)
    table = pl.BlockSpec((tm, HEAD_DIM), lambda i, j: (i % tiles_per_seq, 0))
    outs = pl.pallas_call(
        functools.partial(_qkv_proj_kernel, n_cast=len(cast)),
        grid=(m // tm, 1),
        in_specs=[pl.BlockSpec((tm, k), lambda i, j: (i, 0)),
                  pl.BlockSpec((1, k), lambda i, j: (0, 0)),
                  pl.BlockSpec((k, n), lambda i, j: (0, 0), pipeline_mode=pl.Buffered(1)),
                  table, table, *cast_in_specs],
        out_specs=[pl.BlockSpec((tm, n), lambda i, j: (i, 0)), *cast_out_specs],
        out_shape=[jax.ShapeDtypeStruct((m, n), F32), *cast_out_shapes],
        compiler_params=_compiler_params(("arbitrary", "arbitrary")),
        name="qkv_proj",
    )(x, g, w, cos_tab, sin_tab, *cast)
    return outs[0], outs[1:]


def _even_mixer_kernel(x_hbm, g_pre_ref, w_in_ref, ws_ref, bias_ref, cw_ref, w_out_ref,
                       g_post_ref, *rest, tiles_per_seq, n_cast):
    cast_in = rest[:n_cast]
    o_ref = rest[n_cast]
    cast_out = rest[n_cast + 1:2 * n_cast + 1]
    proj_ref, z_ref, xbuf_ref, x_sem = rest[2 * n_cast + 1:]
    tm = o_ref.shape[0]
    i = pl.program_id(0)
    n_tiles = pl.num_programs(0)

    def x_copy(t):
        slot = t % X_RING
        return pltpu.make_async_copy(x_hbm.at[pl.ds(t * tm, tm), :], xbuf_ref.at[slot],
                                     x_sem.at[slot])

    @pl.when(i == 0)
    def _():
        z_ref[...] = jnp.zeros_like(z_ref)
        x_copy(0).start()

    @pl.when((i == 0) & (n_tiles > 1))
    def _():
        x_copy(1).start()

    @pl.when(i + 2 < n_tiles)
    def _():
        x_copy(i + 2).start()

    x_copy(i).wait()
    x_ref = xbuf_ref.at[i % X_RING]

    _cast_chunks(cast_in, cast_out)

    hn = _rmsnorm(x_ref[...], g_pre_ref[...]).astype(BF16)
    sub = RMS_MATMUL_SUB_TN
    for c in range(w_in_ref.shape[1] // sub):
        cols = pl.ds(c * sub, sub)
        p = jnp.dot(hn, w_in_ref[:, cols], preferred_element_type=F32)
        proj_ref[:, cols] = jax.nn.gelu(p) if (c + 1) * sub <= 2 * A_WIDTH else p
    au_ref, av_ref, gb_ref, gc_ref, bx_ref = (
        proj_ref.at[:, pl.ds(s * A_WIDTH, A_WIDTH)] for s in range(5))

    av = av_ref[...]
    mu = jnp.mean(av, axis=-1, keepdims=True)
    cen = av - mu
    var = jnp.mean(cen * cen, axis=-1, keepdims=True)
    vn = (cen * lax.rsqrt(var + LN_EPS)).astype(BF16)
    row = lax.broadcasted_iota(jnp.int32, (CHUNK, CHUNK), 0)
    col = lax.broadcasted_iota(jnp.int32, (CHUNK, CHUNK), 1)
    causal = col <= row
    a_cols = []
    for g in range(A_GROUPS):
        cs = slice(g * A_GROUP_DIM, (g + 1) * A_GROUP_DIM)
        w_causal = jnp.where(causal, ws_ref[g], 0.0).astype(BF16)
        chunks = []
        for c in range(tm // CHUNK):
            rs = slice(c * CHUNK, (c + 1) * CHUNK)
            mixed = jnp.dot(w_causal, vn[rs, cs], preferred_element_type=F32) + bias_ref[:, cs]
            chunks.append((au_ref[rs, cs] * mixed).astype(BF16))
        a_cols.append(jnp.concatenate(chunks, axis=0))

    z = gc_ref[...] * bx_ref[...]
    at_seq_start = (i % tiles_per_seq) == 0
    z_before = jnp.where(at_seq_start, 0.0, z_ref[tm:tm + CONV_HALO, :])
    z_ref[0:CONV_HALO, :] = z_before
    z_ref[CONV_HALO:CONV_HALO + tm, :] = z
    y = (cw_ref[2:3, :] * z
         + cw_ref[1:2, :] * z_ref[CONV_HALO - 1:CONV_HALO - 1 + tm, :]
         + cw_ref[0:1, :] * z_ref[CONV_HALO - 2:CONV_HALO - 2 + tm, :])
    b_out = (gb_ref[...] * y).astype(BF16)

    mixed_ab = jnp.concatenate(a_cols + [b_out], axis=1)
    f = jnp.dot(mixed_ab, w_out_ref[...], preferred_element_type=F32)
    o_ref[...] = x_ref[...] + _rmsnorm(f, g_post_ref[...])


def _even_mixer(h, g_pre, w_in, w_spatial, bias_full, conv_w, w_out, g_post, seq,
                cast=(), cast_li=0):
    m, d = h.shape
    n_in = w_in.shape[1]
    tm = EVEN_MIXER_TM
    assert seq % tm == 0 and tm % CHUNK == 0 and A_WIDTH == B_WIDTH
    assert n_in == 2 * A_WIDTH + 3 * B_WIDTH and (2 * A_WIDTH) % RMS_MATMUL_SUB_TN == 0
    cast_in_specs, cast_out_specs, cast_out_shapes = _weight_cast_plan(
        cast, cast_li, m // tm, 1, WEIGHT_CAST_STEPS)
    const = lambda shape: pl.BlockSpec(shape, lambda i, j: (0,) * len(shape))
    resident = lambda rows, cols: pl.BlockSpec((rows, cols), lambda i, j: (0, 0),
                                               pipeline_mode=pl.Buffered(1))
    kern = functools.partial(_even_mixer_kernel, tiles_per_seq=seq // tm, n_cast=len(cast))
    outs = pl.pallas_call(
        kern,
        grid=(m // tm, 1),
        in_specs=[pl.BlockSpec(memory_space=pl.ANY),
                  const((1, d)),
                  resident(d, n_in),
                  const((A_GROUPS, CHUNK, CHUNK)),
                  const((CHUNK, A_WIDTH)),
                  const((CONV_WIDTH, B_WIDTH)),
                  resident(d, d),
                  const((1, d)),
                  *cast_in_specs],
        out_specs=[pl.BlockSpec((tm, d), lambda i, j: (i, 0)), *cast_out_specs],
        out_shape=[jax.ShapeDtypeStruct((m, d), F32), *cast_out_shapes],
        scratch_shapes=[pltpu.VMEM((tm, n_in), F32),
                        pltpu.VMEM((CONV_HALO + tm, B_WIDTH), F32),
                        pltpu.VMEM((X_RING, tm, d), F32),
                        pltpu.SemaphoreType.DMA((X_RING,))],
        compiler_params=_compiler_params(("arbitrary", "arbitrary")),
        name="even_mixer",
    )(h, g_pre, w_in, w_spatial, bias_full, conv_w, w_out, g_post, *cast)
    return outs[0], outs[1:]


def _attn_band_bias():
    blk = ATTN_BLOCK
    tables = []
    for _, d in DILATED_BRANCHES:
        pieces = ATTN_PERM // d
        rows = blk // pieces
        i = np.arange(blk)
        run = i // rows
        if pieces == ATTN_PERM:
            run = ATTN_PERM_STEP * (run % ATTN_PERM_STEP) + run // ATTN_PERM_STEP
        uq = pieces * (i % rows) + run
        uk = np.concatenate([uq, blk + uq])
        step = uq[:, None] + blk - uk[None, :]
        valid = (step >= 0) & (step <= blk)
        first = valid & (uk[None, :] >= blk)
        tables.append(np.stack([np.where(valid, 0.0, -np.inf), np.where(first, 0.0, -np.inf)]))
    return np.stack(tables).astype(np.float32)


def _attn_block(qb, k_prev, k_cur, v_prev, v_cur, bias, old):
    kw = jnp.concatenate([k_prev, k_cur], axis=0)
    vw = jnp.concatenate([v_prev, v_cur], axis=0)
    s = lax.dot_general(qb, kw, (((1,), (1,)), ((), ())), preferred_element_type=F32) + bias
    m_blk = jnp.max(s, axis=-1, keepdims=True)
    m_new = jnp.broadcast_to(m_blk, (s.shape[0], HEAD_DIM))
    if old is not None:
        m_new = jnp.maximum(old[0], m_new)
    p = jnp.exp2(s - jnp.concatenate([m_new, m_new], axis=1)).astype(BF16)
    v_ones = jnp.concatenate([vw, jnp.ones_like(vw)], axis=1)
    pv = jnp.dot(p, v_ones, preferred_element_type=F32)
    a_new, l_new = pv[:, :HEAD_DIM], pv[:, HEAD_DIM:]
    if old is not None:
        corr = jnp.exp2(old[0] - m_new)
        l_new = old[1] * corr + l_new
        a_new = old[2] * corr + a_new
    return m_new, l_new, a_new


def _attn_kernel(q_ref, k_ref, v_ref, bias_ref, o_ref, qp_ref, kp_ref, vp_ref,
                 qh_ref, kh_ref, vh_ref, acc_ref, m_ref, l_ref, tmp_ref, *, seq, dilations):
    blk = ATTN_BLOCK
    lp = seq // ATTN_PERM
    step = ATTN_PERM_STEP
    lq = seq // step
    q_scale = HEAD_DIM ** -0.5 * math.log2(math.e)
    for src_ref, dst_ref, half_ref, scale in ((q_ref, qp_ref, qh_ref, q_scale),
                                              (k_ref, kp_ref, kh_ref, None),
                                              (v_ref, vp_ref, vh_ref, None)):
        for lo in range(step):
            tmp_ref[lo * lq:(lo + 1) * lq, :] = src_ref[pl.ds(lo, lq, stride=step), :]
        for lo in range(step):
            for hi in range(step):
                run = lo * step + hi
                rows = tmp_ref[pl.ds(lo * lq + hi, lp, stride=step), :]
                if scale is not None:
                    rows = rows * scale
                dst_ref[run * lp:(run + 1) * lp, :] = rows
                half_ref[run * lp:(run + 1) * lp, :] = rows.astype(BF16)

    for bi, d in enumerate(dilations):
        pieces = ATTN_PERM // d
        rows = blk // pieces
        nb = lp // rows
        gn = min(nb, ATTN_GROUP)
        gs = min(ATTN_GROUP // gn, d)
        n_groups = nb // gn
        assert nb % gn == 0 and d % gs == 0
        packed = rows % BF16_SUBLANES == 0
        q_src, k_src, v_src = (qh_ref, kh_ref, vh_ref) if packed else (qp_ref, kp_ref, vp_ref)

        def starts(r_sub, n, d=d, pieces=pieces, rows=rows):
            return [pl.multiple_of((r_sub * pieces + a) * lp + n * rows, rows) for a in range(pieces)]

        def gather(ref, st, rows=rows):
            parts = [ref[pl.ds(s0, rows), :] for s0 in st]
            return parts[0] if len(parts) == 1 else jnp.concatenate(parts, axis=0)

        def gather_half(ref, st):
            return gather(ref, st).astype(BF16)

        def scatter(ref, st, val, rows=rows):
            for a, s0 in enumerate(st):
                ref[pl.ds(s0, rows), :] = val[a * rows:(a + 1) * rows]

        def body(it, carry, bi=bi, gn=gn, gs=gs, n_groups=n_groups,
                 q_src=q_src, k_src=k_src, v_src=v_src):
            sg = it // n_groups
            n0 = (it % n_groups) * gn
            work = []
            for si in range(gs):
                r_sub = sg * gs + si
                st_prev = starts(r_sub, jnp.maximum(n0 - 1, 0))
                kb = [gather_half(k_src, st_prev)]
                vb = [gather_half(v_src, st_prev)]
                for j in range(gn):
                    st = starts(r_sub, n0 + j)
                    kb.append(gather_half(k_src, st))
                    vb.append(gather_half(v_src, st))
                    if j > 0:
                        bias = bias_ref[bi, 0]
                    elif n_groups == 1:
                        bias = bias_ref[bi, 1]
                    else:
                        bias = bias_ref[bi, jnp.where(n0 == 0, 1, 0)]
                    old = None if bi == 0 else (gather(m_ref, st), gather(l_ref, st),
                                                gather(acc_ref, st))
                    work.append((st, gather_half(q_src, st), kb[j], kb[j + 1],
                                 vb[j], vb[j + 1], bias, old))
            results = [(w[0],) + _attn_block(*w[1:]) for w in work]
            for st, m_new, l_new, a_new in results:
                scatter(m_ref, st, m_new)
                scatter(l_ref, st, l_new)
                scatter(acc_ref, st, a_new)
            return carry

        lax.fori_loop(0, (d // gs) * n_groups, body, 0)

    for lo in range(step):
        for hi in range(step):
            src = slice((lo * step + hi) * lp, (lo * step + hi + 1) * lp)
            tmp_ref[pl.ds(lo * lq + hi, lp, stride=step), :] = acc_ref[src, :] / l_ref[src, :]
    for lo in range(step):
        o_ref[pl.ds(lo, lq, stride=step), :] = tmp_ref[lo * lq:(lo + 1) * lq, :]


def _attention(qkv, batch, seq):
    dilations = tuple(d for _, d in DILATED_BRANCHES)
    for window, d in DILATED_BRANCHES:
        assert window // d == ATTN_BLOCK and seq % (d * ATTN_BLOCK) == 0
        assert ATTN_PERM % d == 0 and ATTN_BLOCK % (ATTN_PERM // d) == 0
    qkv3 = qkv.reshape(batch, seq, 3 * D_MODEL)
    bias = jnp.asarray(_attn_band_bias())
    spec = lambda part: pl.BlockSpec((None, seq, HEAD_DIM),
                                     lambda b, h, part=part: (b, 0, part * N_HEADS + h))
    kern = functools.partial(_attn_kernel, seq=seq, dilations=dilations)
    out = pl.pallas_call(
        kern,
        grid=(batch, N_HEADS),
        in_specs=[spec(0), spec(1), spec(2),
                  pl.BlockSpec(bias.shape, lambda b, h: (0, 0, 0, 0))],
        out_specs=pl.BlockSpec((None, seq, HEAD_DIM), lambda b, h: (b, 0, h)),
        out_shape=jax.ShapeDtypeStruct((batch, seq, D_MODEL), F32),
        scratch_shapes=[pltpu.VMEM((seq, HEAD_DIM), F32)] * 3 + [pltpu.VMEM((seq, HEAD_DIM), BF16)] * 3
                       + [pltpu.VMEM((seq, HEAD_DIM), F32)] * 4,
        compiler_params=_compiler_params(("parallel", "parallel")),
        name="attention",
    )(qkv3, qkv3, qkv3, bias)
    return out.reshape(batch * seq, D_MODEL)


def _matmul_rms_res_kernel(a_ref, w_ref, g_ref, h_ref, o_ref):
    f = jnp.dot(a_ref[...].astype(BF16), w_ref[...], preferred_element_type=F32)
    o_ref[...] = h_ref[...] + _rmsnorm(f, g_ref[...])


def _matmul_rms_res(a, w, g, h):
    m, k = a.shape
    n = w.shape[1]
    tm = RES_TM
    assert m % tm == 0
    return pl.pallas_call(
        _matmul_rms_res_kernel,
        grid=(m // tm,),
        in_specs=[pl.BlockSpec((tm, k), lambda i: (i, 0)),
                  pl.BlockSpec((k, n), lambda i: (0, 0)),
                  pl.BlockSpec((1, n), lambda i: (0, 0)),
                  pl.BlockSpec((tm, n), lambda i: (i, 0))],
        out_specs=pl.BlockSpec((tm, n), lambda i: (i, 0)),
        out_shape=jax.ShapeDtypeStruct((m, n), F32),
        compiler_params=_compiler_params(("parallel",)),
        name="matmul_rms_res",
    )(a, w, g, h)


def _mlp_kernel(x_ref, g_pre_ref, w_up_ref, w_down_ref, g_post_ref, *rest, n_cast):
    cast_in = rest[:n_cast]
    o_ref = rest[n_cast]
    cast_out = rest[n_cast + 1:2 * n_cast + 1]
    hn_ref = rest[2 * n_cast + 1]
    k = pl.program_id(1)

    def ffn_step(first, last):
        sub = MLP_SUB
        tm = o_ref.shape[0]
        _cast_chunks(cast_in, cast_out)
        group = tm // MLP_LAST_SPLIT if last else tm
        for r0 in range(0, tm, group):
            rows = pl.ds(r0, group)
            if first:
                hn = _rmsnorm(x_ref[rows, :], g_pre_ref[...]).astype(BF16)
                hn_ref[rows, :] = hn
            else:
                hn = hn_ref[rows, :]
            for c in range(w_up_ref.shape[1] // sub):
                ffn = pl.ds(c * sub, sub)
                u = jnp.dot(hn, w_up_ref[:, ffn], preferred_element_type=F32)
                u = jnp.square(jnp.maximum(u, 0.0)).astype(BF16)
                for n in range(o_ref.shape[1] // sub):
                    cols = pl.ds(n * sub, sub)
                    part = jnp.dot(u, w_down_ref[ffn, cols], preferred_element_type=F32)
                    if first and c == 0:
                        o_ref[rows, cols] = part
                    else:
                        o_ref[rows, cols] += part
            if last:
                for r1 in range(r0, r0 + group, NORM_ROW_CHUNK):
                    chunk = pl.ds(r1, NORM_ROW_CHUNK)
                    o_ref[chunk, :] = x_ref[chunk, :] + _rmsnorm(o_ref[chunk, :], g_post_ref[...])

    n_steps = pl.num_programs(1)

    @pl.when(k == 0)
    def _():
        ffn_step(first=True, last=False)

    @pl.when((k > 0) & (k < n_steps - 1))
    def _():
        ffn_step(first=False, last=False)

    @pl.when(k == n_steps - 1)
    def _():
        ffn_step(first=False, last=True)


def _mlp(h, g_pre, w_up, w_down, g_post, cast=(), cast_li=0):
    m, d = h.shape
    f = w_up.shape[1]
    tm, tf = MLP_TM, MLP_TF
    assert m % tm == 0 and f % tf == 0
    cast_in_specs, cast_out_specs, cast_out_shapes = _weight_cast_plan(
        cast, cast_li, m // tm, f // tf, MLP_CAST_STEPS)
    outs = pl.pallas_call(
        functools.partial(_mlp_kernel, n_cast=len(cast)),
        grid=(m // tm, f // tf),
        in_specs=[pl.BlockSpec((tm, d), lambda i, k: (i, 0)),
                  pl.BlockSpec((1, d), lambda i, k: (0, 0)),
                  pl.BlockSpec((d, tf), lambda i, k: (0, k)),
                  pl.BlockSpec((tf, d), lambda i, k: (k, 0)),
                  pl.BlockSpec((1, d), lambda i, k: (0, 0)),
                  *cast_in_specs],
        out_specs=[pl.BlockSpec((tm, d), lambda i, k: (i, 0)), *cast_out_specs],
        out_shape=[jax.ShapeDtypeStruct((m, d), F32), *cast_out_shapes],
        scratch_shapes=[pltpu.VMEM((tm, d), BF16)],
        compiler_params=_compiler_params(("arbitrary", "arbitrary")),
        name="mlp",
    )(h, g_pre, w_up, w_down, g_post, *cast)
    return outs[0], outs[1:]


def _rope_tables(seq):
    half = HEAD_DIM // 2
    inv_freq = np.float32(ROPE_THETA) ** (-np.arange(half, dtype=np.float32) * np.float32(2.0)
                                          / np.float32(HEAD_DIM))
    ang = np.arange(seq, dtype=np.float32)[:, None] * inv_freq[None, :].astype(np.float32)
    cos = np.cos(ang).astype(np.float32)
    sin = np.sin(ang).astype(np.float32)
    return (jnp.asarray(np.concatenate([cos, cos], axis=-1)),
            jnp.asarray(np.concatenate([-sin, sin], axis=-1)))


def kernel(x, norm_mix_pre, norm_mix_post, norm_mlp_pre, norm_mlp_post, w_in_ab, w_spatial,
           b_spatial, conv_w, w_out_ab, w_qkv, w_o, w_up, w_down):
    batch, seq, d = x.shape
    assert d == D_MODEL
    depth = norm_mix_pre.shape[0]
    m = batch * seq
    h = x.reshape(m, d)
    cos_tab, sin_tab = _rope_tables(seq)

    def mixer_weights(layer):
        return (w_in_ab, w_out_ab) if layer % 2 == 0 else (w_qkv, w_o)

    mix_bf = tuple(w[0].astype(BF16) for w in mixer_weights(0))
    mlp_bf = None
    for layer in range(depth):
        g_pre = norm_mix_pre[layer][None, :]
        g_post = norm_mix_post[layer][None, :]
        idx = layer // 2
        cast = mixer_weights(layer + 1) if layer + 1 < depth else ()
        n_next = len(cast)
        if layer == 0:
            cast = cast + (w_up, w_down)
        cast_li = (layer + 1) // 2
        if layer % 2 == 0:
            bias_full = jnp.repeat(b_spatial[idx].T, A_GROUP_DIM, axis=1)
            h, cast_bf = _even_mixer(h, g_pre, mix_bf[0], w_spatial[idx], bias_full, conv_w[idx],
                                     mix_bf[1], g_post, seq, cast=cast, cast_li=cast_li)
        else:
            qkv, cast_bf = _qkv_proj(h, g_pre, mix_bf[0], cos_tab, sin_tab, seq,
                                     cast=cast, cast_li=cast_li)
            att = _attention(qkv, batch, seq)
            h = _matmul_rms_res(att, mix_bf[1], g_post, h)
        mix_bf = cast_bf[:n_next]
        if layer == 0:
            mlp_bf = cast_bf[n_next:]
        next_cast = (w_up, w_down) if layer + 1 < depth else ()
        h, next_bf = _mlp(h, norm_mlp_pre[layer][None, :], mlp_bf[0], mlp_bf[1],
                          norm_mlp_post[layer][None, :], cast=next_cast, cast_li=layer + 1)
        if next_bf:
            mlp_bf = next_bf
    return h.reshape(batch, seq, d)
```
